```python
import jax, jax.numpy as jnp
from jax import lax
import numpy as np

D_MODEL = 1024
BATCH = 2
SEQ = 8192
DEPTH = 4
DEC_BATCH = 32
DEC_SEQ = 8
PAST_LEN = 8192
PAGE_SIZE = 128

N_MIXERS = 3
HEAD_DIM = 64
N_HEADS = D_MODEL // HEAD_DIM
CHUNK = 128
D_A = 2 * D_MODEL
A_GROUPS = 8
A_GROUP_DIM = D_A // A_GROUPS
N_A_LAYERS = (DEPTH + N_MIXERS - 1) // N_MIXERS
Q_BLOCK = 128
D_DECAY_LORA = 64
D_AAA_LORA = 64
D_GATE_LORA = 160
D_FF = 2816
CONV_W = 3
NORM_EPS = 1e-6
LNX_EPS = 64e-5
FORGET_BIAS_LO = 1.0
FORGET_BIAS_HI = 4.0

kernel_name = 'hybrid_chunkmlp_fox_rwkv7_convffn_step'


def rms_norm(x, g):
    xf = x.astype(jnp.float32)
    y = xf * lax.rsqrt(jnp.mean(xf * xf, axis=-1, keepdims=True) + NORM_EPS)
    return (y * g.astype(jnp.float32)).astype(x.dtype)


def conv_ffn(x, conv_state, g, w_up, conv_w, conv_b, w_down):
    T = x.shape[1]
    h = rms_norm(x, g) @ w_up
    hp = jnp.concatenate([conv_state.astype(h.dtype), h], axis=1)
    hc = conv_b + sum(conv_w[i] * hp[:, i:i + T] for i in range(CONV_W))
    gate, val = jnp.split(hc, 2, axis=-1)
    out = (jax.nn.silu(gate) * val) @ w_down
    return out, hp[:, T:]


def chunk_mlp(xn, w_in, v_g, w_s, b_s, w_out):
    B, T, _ = xn.shape
    L = min(T, CHUNK)
    n = T // L
    h = jax.nn.gelu(xn @ w_in)
    u, v = jnp.split(h, 2, axis=-1)
    v = rms_norm(v.reshape(B, T, A_GROUPS, A_GROUP_DIM), v_g)
    mask = jnp.tril(jnp.ones((L, L), dtype=bool))
    ws = jnp.where(mask, w_s[:, :L, :L], 0)
    vc = v.reshape(B, n, L, A_GROUPS, A_GROUP_DIM)
    mixed = jnp.einsum('gts,bnsgc->bntgc', ws, vc) + b_s[:, :L].T[:, :, None]
    out = (u * mixed.reshape(B, T, D_A)) @ w_out
    return out, v.reshape(B, T, D_A)


def fox_project(xn, w_qkv, q_g, k_g, w_f, b_f, w_og):
    B, T, _ = xn.shape
    qkv = (xn @ w_qkv).reshape(B, T, 3, N_HEADS, HEAD_DIM)
    q = rms_norm(qkv[:, :, 0], q_g)
    k = rms_norm(qkv[:, :, 1], k_g)
    v = qkv[:, :, 2]
    logf = jax.nn.log_sigmoid((xn @ w_f + b_f).astype(jnp.float32))
    og = jax.nn.sigmoid(xn @ w_og)
    return q, k, v, logf, og


def fox_logits(q, k, c_q, c_k, q_pos, k_pos):
    s = jnp.einsum('bqhd,bkhd->bhqk', q, k, preferred_element_type=jnp.float32) * (HEAD_DIM ** -0.5)
    s = s + (jnp.transpose(c_q, (0, 2, 1))[:, :, :, None] - jnp.transpose(c_k, (0, 2, 1))[:, :, None, :])
    return jnp.where(q_pos[:, None] >= k_pos[None, :], s, -jnp.inf)


def fox_prompt(q, k, v, logf):
    B, T, H, Dh = q.shape
    c = jnp.cumsum(logf, axis=1)
    k_pos = jnp.arange(T)

    def block(i):
        start = i * Q_BLOCK
        qb = lax.dynamic_slice_in_dim(q, start, Q_BLOCK, axis=1)
        cb = lax.dynamic_slice_in_dim(c, start, Q_BLOCK, axis=1)
        s = fox_logits(qb, k, cb, c, start + jnp.arange(Q_BLOCK), k_pos)
        p = jax.nn.softmax(s, axis=-1).astype(v.dtype)
        return jnp.einsum('bhqk,bkhd->bqhd', p, v)

    o = lax.map(block, jnp.arange(T // Q_BLOCK))
    return jnp.moveaxis(o, 0, 1).reshape(B, T, H, Dh)


def fox_sample(q, k, v, logf, k_past, v_past, logf_past):
    T = q.shape[1]
    P = k_past.shape[1]
    c_past = jnp.cumsum(logf_past.astype(jnp.float32), axis=1)
    c_new = c_past[:, -1:] + jnp.cumsum(logf, axis=1)
    q_pos = P + jnp.arange(T)
    s = jnp.concatenate([
        fox_logits(q, k_past, c_new, c_past, q_pos, jnp.arange(P)),
        fox_logits(q, k, c_new, c_new, q_pos, q_pos)], axis=-1)
    p = jax.nn.softmax(s, axis=-1).astype(v.dtype)
    return (jnp.einsum('bhqk,bkhd->bqhd', p[..., :P], v_past)
            + jnp.einsum('bhqk,bkhd->bqhd', p[..., P:], v))


def rwkv7_mix(xn, shift0, S0, mu, w_r, w_k, w_v, w0, w1, w2, a0, a1, a2, g1, g2,
              k_k, k_a, r_k, lnx_g, lnx_b, w_o):
    B, T, D = xn.shape
    x_prev = jnp.concatenate([shift0[:, None].astype(xn.dtype), xn[:, :-1]], axis=1)
    xx = x_prev - xn
    xr, xw, xk, xv, xa, xg = (xn + xx * mu[i] for i in range(6))
    r = xr @ w_r
    k = xk @ w_k
    v = xv @ w_v
    w_log = -jax.nn.softplus(-(w0 + jnp.tanh(xw @ w1) @ w2)) - 0.5
    decay = jnp.exp(-jnp.exp(w_log.astype(jnp.float32)))
    a = jax.nn.sigmoid(a0 + (xa @ a1) @ a2)
    g = jax.nn.sigmoid(xg @ g1) @ g2

    def heads(t):
        return t.reshape(B, T, N_HEADS, HEAD_DIM).astype(jnp.float32)

    kk = heads(k * k_k)
    kk = kk / jnp.maximum(jnp.sqrt(jnp.sum(kk * kk, axis=-1, keepdims=True)), 1e-12)
    k = k * (1 + (a - 1) * k_a)
    r_h, k_h, v_h, a_h, w_h = heads(r), heads(k), heads(v), heads(a), heads(decay)

    def step(S, inp):
        r_t, w_t, k_t, v_t, kk_t, a_t = inp
        sa = jnp.einsum('bhvk,bhk->bhv', S, -kk_t)
        S = (S * w_t[:, :, None, :] + sa[..., None] * (kk_t * a_t)[:, :, None, :]
             + v_t[..., None] * k_t[:, :, None, :])
        y = jnp.einsum('bhvk,bhk->bhv', S, r_t)
        return S, y

    seq = tuple(jnp.moveaxis(t, 1, 0) for t in (r_h, w_h, k_h, v_h, kk, a_h))
    S, y = lax.scan(step, S0.astype(jnp.float32), seq)
    y = jnp.moveaxis(y, 0, 1)
    mean = jnp.mean(y, axis=-1, keepdims=True)
    var = jnp.mean(jnp.square(y - mean), axis=-1, keepdims=True)
    y = ((y - mean) * lax.rsqrt(var + LNX_EPS)).reshape(B, T, D) * lnx_g.astype(jnp.float32) + lnx_b.astype(jnp.float32)
    y = y + (jnp.sum(r_h * k_h * r_k.astype(jnp.float32), axis=-1, keepdims=True) * v_h).reshape(B, T, D)
    out = (y.astype(xn.dtype) * g) @ w_o
    return out, xn[:, -1], S.astype(xn.dtype)


def setup_inputs(seed: int = 0) -> dict:
    key = jax.random.key(seed)
    ks = iter(jax.random.split(key, 64))

    def nrm(shape, scale):
        return jax.random.normal(next(ks), shape, jnp.float32) * scale

    def unif(shape, lo, hi):
        return jax.random.uniform(next(ks), shape, jnp.float32, lo, hi)

    D = D_MODEL
    n_pages = PAST_LEN // PAGE_SIZE
    n_pool = (DEC_BATCH * n_pages * 5) // 4
    page_table = jax.random.permutation(next(ks), n_pool)[:DEC_BATCH * n_pages].reshape(DEC_BATCH, n_pages).astype(jnp.int32)
    return {
        'x_prompt': nrm((BATCH, SEQ, D), 1.0),
        'x_sample': nrm((DEC_BATCH, DEC_SEQ, D), 1.0),
        'cache_k': nrm((n_pool, PAGE_SIZE, N_HEADS, HEAD_DIM), 1.0),
        'cache_v': nrm((n_pool, PAGE_SIZE, N_HEADS, HEAD_DIM), 1.0),
        'cache_logf': jax.nn.log_sigmoid(nrm((n_pool, PAGE_SIZE, N_HEADS), 1.0) + 3.0),
        'page_table': page_table,
        'state_wkv': nrm((DEC_BATCH, N_HEADS, HEAD_DIM, HEAD_DIM), 0.5),
        'state_shift': nrm((DEC_BATCH, D), 1.0),
        'state_conv': nrm((DEPTH, DEC_BATCH, CONV_W - 1, 2 * D_FF), 1.0),
        'mix_norm': 1.0 + nrm((DEPTH, D), 0.02),
        'a_w_in': nrm((N_A_LAYERS, D, 2 * D_A), D ** -0.5),
        'a_v_norm': 1.0 + nrm((N_A_LAYERS, A_GROUPS, A_GROUP_DIM), 0.02),
        'a_w_s': nrm((N_A_LAYERS, A_GROUPS, CHUNK, CHUNK), CHUNK ** -0.5),
        'a_b_s': 1.0 + nrm((N_A_LAYERS, A_GROUPS, CHUNK), 0.1),
        'a_w_out': nrm((N_A_LAYERS, D_A, D), D_A ** -0.5),
        'f_w_qkv': nrm((D, 3 * D), D ** -0.5),
        'f_q_norm': 1.0 + nrm((HEAD_DIM,), 0.02),
        'f_k_norm': 1.0 + nrm((HEAD_DIM,), 0.02),
        'f_w_fgate': nrm((D, N_HEADS), D ** -0.5),
        'f_b_fgate': unif((N_HEADS,), FORGET_BIAS_LO, FORGET_BIAS_HI),
        'f_w_ogate': nrm((D, D), D ** -0.5),
        'f_w_out': nrm((D, D), D ** -0.5),
        'r_mu': unif((6, D), 0.0, 1.0),
        'r_w_r': nrm((D, D), D ** -0.5),
        'r_w_k': nrm((D, D), D ** -0.5),
        'r_w_v': nrm((D, D), D ** -0.5),
        'r_w0': unif((D,), -5.0, -1.0),
        'r_w1': nrm((D, D_DECAY_LORA), D ** -0.5),
        'r_w2': nrm((D_DECAY_LORA, D), 0.1 * D_DECAY_LORA ** -0.5),
        'r_a0': nrm((D,), 0.1),
        'r_a1': nrm((D, D_AAA_LORA), D ** -0.5),
        'r_a2': nrm((D_AAA_LORA, D), 0.5 * D_AAA_LORA ** -0.5),
        'r_g1': nrm((D, D_GATE_LORA), D ** -0.5),
        'r_g2': nrm((D_GATE_LORA, D), D_GATE_LORA ** -0.5),
        'r_k_k': 0.85 + nrm((D,), 0.05),
        'r_k_a': 1.0 + nrm((D,), 0.05),
        'r_r_k': nrm((N_HEADS, HEAD_DIM), 0.1),
        'r_lnx_g': 1.0 + nrm((D,), 0.02),
        'r_lnx_b': nrm((D,), 0.02),
        'r_w_o': nrm((D, D), D ** -0.5),
        'ffn_norm': 1.0 + nrm((DEPTH, D), 0.02),
        'ffn_w_up': nrm((DEPTH, D, 2 * D_FF), D ** -0.5),
        'ffn_conv_w': nrm((DEPTH, CONV_W, 2 * D_FF), CONV_W ** -0.5),
        'ffn_conv_b': nrm((DEPTH, 2 * D_FF), 0.02),
        'ffn_w_down': nrm((DEPTH, D_FF, D), D_FF ** -0.5),
    }


def reference(x_prompt, x_sample, cache_k, cache_v, cache_logf, page_table, state_wkv, state_shift, state_conv,
              mix_norm, a_w_in, a_v_norm, a_w_s, a_b_s, a_w_out,
              f_w_qkv, f_q_norm, f_k_norm, f_w_fgate, f_b_fgate, f_w_ogate, f_w_out,
              r_mu, r_w_r, r_w_k, r_w_v, r_w0, r_w1, r_w2, r_a0, r_a1, r_a2, r_g1, r_g2,
              r_k_k, r_k_a, r_r_k, r_lnx_g, r_lnx_b, r_w_o,
              ffn_norm, ffn_w_up, ffn_conv_w, ffn_conv_b, ffn_w_down):
    B = x_prompt.shape[0]
    DB = x_sample.shape[0]
    n_pages = page_table.shape[1]

    def gather(pool):
        return pool[page_table].reshape((DB, n_pages * pool.shape[1]) + pool.shape[2:])

    xp, xs = x_prompt, x_sample
    conv_p, conv_s, chunk_v_s = [], [], []
    for i in range(DEPTH):
        kind = i % N_MIXERS
        hp = rms_norm(xp, mix_norm[i])
        hs = rms_norm(xs, mix_norm[i])
        if kind == 0:
            j = i // N_MIXERS
            wts = (a_w_in[j], a_v_norm[j], a_w_s[j], a_b_s[j], a_w_out[j])
            mp, _ = chunk_mlp(hp, *wts)
            ms, v_rows = chunk_mlp(hs, *wts)
            chunk_v_s.append(v_rows)
        elif kind == 1:
            fw = (f_w_qkv, f_q_norm, f_k_norm, f_w_fgate, f_b_fgate, f_w_ogate)
            q_p, k_p, v_p, logf_p, og_p = fox_project(hp, *fw)
            mp = (fox_prompt(q_p, k_p, v_p, logf_p).reshape(B, -1, D_MODEL) * og_p) @ f_w_out
            q_s, k_s, v_s, logf_s, og_s = fox_project(hs, *fw)
            att_s = fox_sample(q_s, k_s, v_s, logf_s, gather(cache_k), gather(cache_v), gather(cache_logf))
            ms = (att_s.reshape(DB, -1, D_MODEL) * og_s) @ f_w_out
        else:
            rw = (r_mu, r_w_r, r_w_k, r_w_v, r_w0, r_w1, r_w2, r_a0, r_a1, r_a2, r_g1, r_g2,
                  r_k_k, r_k_a, r_r_k, r_lnx_g, r_lnx_b, r_w_o)
            mp, shift_p, wkv_p = rwkv7_mix(hp, jnp.zeros((B, D_MODEL), hp.dtype),
                                           jnp.zeros((B, N_HEADS, HEAD_DIM, HEAD_DIM), jnp.float32), *rw)
            ms, shift_s, wkv_s = rwkv7_mix(hs, state_shift, state_wkv, *rw)
        xp = xp + mp
        xs = xs + ms
        fp, cp = conv_ffn(xp, jnp.zeros((B, CONV_W - 1, 2 * D_FF), xp.dtype),
                          ffn_norm[i], ffn_w_up[i], ffn_conv_w[i], ffn_conv_b[i], ffn_w_down[i])
        fs, cs = conv_ffn(xs, state_conv[i], ffn_norm[i], ffn_w_up[i], ffn_conv_w[i], ffn_conv_b[i], ffn_w_down[i])
        xp = xp + fp
        xs = xs + fs
        conv_p.append(cp)
        conv_s.append(cs)
    return (xp, xs, k_p, v_p, logf_p, wkv_p, shift_p, jnp.stack(conv_p),
            k_s, v_s, logf_s, wkv_s, shift_s, jnp.stack(conv_s), jnp.stack(chunk_v_s))
```

```python
import functools
import math

import jax
import jax.numpy as jnp
from jax import lax
from jax.experimental import pallas as pl
from jax.experimental.pallas import tpu as pltpu

F32 = jnp.float32
BF = jnp.bfloat16

HEAD_DIM = 64
LANES = 128
PAIR = LANES // HEAD_DIM
A_GROUPS = 8
CHUNK = 128
RWKV_CHUNK = 64
NORM_EPS = 1e-6
LNX_EPS = 64e-5
NEG_BIG = -1e30
VMEM_LIMIT = 56 * 1024 * 1024


def _tiles(n_tokens):
    tm = min(n_tokens, 512)
    return dict(tm=tm, tq=min(n_tokens, 512))


def _dot(a, b):
    return jnp.dot(a, b, preferred_element_type=F32)


def _dot_nt(a, b):
    return lax.dot_general(a, b, (((1,), (1,)), ((), ())), preferred_element_type=F32)


def _dot_tn(a, b):
    return lax.dot_general(a, b, (((0,), (0,)), ((), ())), preferred_element_type=F32)


def _split3(x):
    hi = x.astype(BF)
    r1 = x - hi.astype(F32)
    mid = r1.astype(BF)
    lo = (r1 - mid.astype(F32)).astype(BF)
    return hi, mid, lo


def _dot_exact_lhs01(m01, x):
    hi, mid, lo = _split3(x)
    return _dot(m01, hi) + _dot(m01, mid) + _dot(m01, lo)


def _rms(x, g):
    return x * lax.rsqrt(jnp.mean(x * x, axis=-1, keepdims=True) + NORM_EPS) * g


def _iota(shape, axis):
    return lax.broadcasted_iota(jnp.int32, shape, axis)


def _const_spec(shape, index=None):
    nd = len(shape)
    idx = tuple(index) if index is not None else (0,) * nd
    return pl.BlockSpec(shape, lambda *_: idx, pipeline_mode=pl.Buffered(1))


def _params(sem):
    return pltpu.CompilerParams(dimension_semantics=sem, vmem_limit_bytes=VMEM_LIMIT)


def _ffn_kernel(x_ref, st_ref, g_ref, wu_ref, cw_ref, cb_ref, wd_ref, y_ref, ns_ref, *scratch, seg, fc):
    x = x_ref[...]
    tm = x.shape[0]
    d_ff = wd_ref.shape[0]
    xn = _rms(x, g_ref[...]).astype(BF)
    row = _iota((tm, fc), 0)
    if seg is None:
        carry_ref, = scratch

        @pl.when(pl.program_id(1) == 0)
        def _():
            carry_ref[...] = st_ref[...]
    else:
        pos = row % seg
    acc = jnp.zeros(y_ref.shape, F32)
    for c in range(d_ff // fc):
        halves = []
        for off in (c * fc, d_ff + c * fc):
            cols = slice(off, off + fc)
            h = _dot(xn, wu_ref[:, cols])
            if seg is None:
                c0 = carry_ref[0:1, cols]
                c1 = carry_ref[1:2, cols]
                hm1 = jnp.where(row == 0, c1, pltpu.roll(h, 1, 0))
                hm2 = jnp.where(row == 0, c0, jnp.where(row == 1, c1, pltpu.roll(h, 2, 0)))
                carry_ref[:, cols] = h[tm - 2:tm, :]
            else:
                hm1 = jnp.where(pos == 0, st_ref[1, :, cols], pltpu.roll(h, 1, 0))
                hm2 = jnp.where(pos < 2, st_ref[0, :, cols], pltpu.roll(h, 2, 0))
                ns_ref[:, :, cols] = h.reshape(tm // seg, seg, fc)[:, seg - 2:, :]
            halves.append(cb_ref[:, cols] + cw_ref[0:1, cols] * hm2 + cw_ref[1:2, cols] * hm1
                          + cw_ref[2:3, cols] * h)
        gate, val = halves
        act = (gate * jax.nn.sigmoid(gate) * val).astype(BF)
        acc = acc + _dot(act, wd_ref[c * fc:(c + 1) * fc, :])
    y_ref[...] = x + acc
    if seg is None:
        ns_ref[...] = carry_ref[...]


def _conv_ffn(x, state, layer, g, w_up, conv_w, conv_b, w_down, *, seg=None):
    B, T, D = x.shape
    F2 = w_up.shape[-1]
    tm = _tiles(T)["tm"]
    fc = 256
    assert T % tm == 0 and (F2 // 2) % fc == 0
    if seg is None:
        st_spec = pl.BlockSpec((None, 2, F2), lambda b, t: (b, 0, 0))
        ns_shape = jax.ShapeDtypeStruct((B, 2, F2), F32)
        ns_spec = pl.BlockSpec((None, 2, F2), lambda b, t: (b, 0, 0))
        scratch = [pltpu.VMEM((2, F2), F32)]
    else:
        assert B == 1 and tm == T and T % seg == 0
        st_spec = pl.BlockSpec((2, T, F2), lambda b, t: (0, 0, 0))
        ns_shape = jax.ShapeDtypeStruct((T // seg, 2, F2), F32)
        ns_spec = pl.BlockSpec((T // seg, 2, F2), lambda b, t: (0, 0, 0))
        scratch = []
    return pl.pallas_call(
        functools.partial(_ffn_kernel, seg=seg, fc=fc),
        grid=(B, T // tm),
        in_specs=[
            pl.BlockSpec((None, tm, D), lambda b, t: (b, t, 0)),
            st_spec,
            _const_spec((None, 1, D), (layer, 0, 0)),
            _const_spec((None, D, F2), (layer, 0, 0)),
            _const_spec((None, 3, F2), (layer, 0, 0)),
            _const_spec((None, 1, F2), (layer, 0, 0)),
            _const_spec((None, F2 // 2, D), (layer, 0, 0)),
        ],
        out_specs=[pl.BlockSpec((None, tm, D), lambda b, t: (b, t, 0)), ns_spec],
        out_shape=[jax.ShapeDtypeStruct((B, T, D), F32), ns_shape],
        scratch_shapes=scratch,
        compiler_params=_params(("arbitrary", "arbitrary")),
        name="conv_ffn",
    )(x, state, g, w_up, conv_w, conv_b, w_down)


def _cmlp_kernel(x_ref, g_ref, win_ref, vg_ref, ws_ref, bs_ref, wout_ref, y_ref, *v_out, seq, chunk):
    x = x_ref[...]
    tm = x.shape[0]
    d_a = wout_ref.shape[0]
    gd = d_a // A_GROUPS
    xn = _rms(x, g_ref[...]).astype(BF)
    r = _iota((chunk, chunk), 0)
    c = _iota((chunk, chunk), 1)
    keep = (c <= r) if seq == chunk else ((r // seq == c // seq) & (c <= r))
    acc = jnp.zeros(y_ref.shape, F32)
    for g in range(A_GROUPS):
        cols = slice(g * gd, (g + 1) * gd)
        u = jax.nn.gelu(_dot(xn, win_ref[:, cols]))
        v = jax.nn.gelu(_dot(xn, win_ref[:, d_a + g * gd:d_a + (g + 1) * gd]))
        v = v * lax.rsqrt(jnp.mean(v * v, axis=-1, keepdims=True) + NORM_EPS) * vg_ref[:, cols]
        if v_out:
            v_out[0][:, cols] = v
        wm = jnp.where(keep, ws_ref[g], 0.0).astype(BF)
        vb = v.astype(BF)
        parts = [_dot(wm, vb[j * chunk:(j + 1) * chunk, :]) + bs_ref[g] for j in range(tm // chunk)]
        mixed = parts[0] if len(parts) == 1 else jnp.concatenate(parts, axis=0)
        acc = acc + _dot((u * mixed).astype(BF), wout_ref[cols, :])
    y_ref[...] = x + acc


def _chunk_mlp(x, j, g, w_in, v_g, w_s, b_s, w_out, *, seq, emit_v):
    B, T, D = x.shape
    chunk = w_s.shape[-1]
    d_a = w_out.shape[1]
    tm = _tiles(T)["tm"]
    assert T % tm == 0 and tm % chunk == 0
    tok = pl.BlockSpec((None, tm, D), lambda b, t: (b, t, 0))
    out_specs = [tok]
    out_shape = [jax.ShapeDtypeStruct((B, T, D), F32)]
    if emit_v:
        out_specs.append(pl.BlockSpec((None, tm, d_a), lambda b, t: (b, t, 0)))
        out_shape.append(jax.ShapeDtypeStruct((B, T, d_a), F32))
    return pl.pallas_call(
        functools.partial(_cmlp_kernel, seq=seq, chunk=chunk),
        grid=(B, T // tm),
        in_specs=[
            tok,
            _const_spec((None, 1, D), (g[1], 0, 0)),
            _const_spec((None, D, 2 * d_a), (j, 0, 0)),
            _const_spec((None, 1, d_a), (j, 0, 0)),
            _const_spec((A_GROUPS, chunk, chunk)),
            _const_spec((A_GROUPS, chunk, 1)),
            _const_spec((None, d_a, D), (j, 0, 0)),
        ],
        out_specs=out_specs,
        out_shape=out_shape,
        compiler_params=_params(("arbitrary", "arbitrary")),
        name="chunk_mlp",
    )(x, g[0], w_in, v_g, w_s, b_s, w_out)


def _fox_proj_kernel(x_ref, g_ref, w_ref, qg_ref, kg_ref, bf_ref, bd_ref,
                     q_ref, kf_ref, kb_ref, vf_ref, vb_ref, og_ref, lf_ref, c_ref, *scratch, seg):
    x = x_ref[...]
    tm, D = x.shape
    n_heads = D // HEAD_DIM
    xn = _rms(x, g_ref[...]).astype(BF)
    bd = bd_ref[...]

    def head_norm(t, gain):
        ms = _dot((t * t).astype(BF), bd) * (1.0 / HEAD_DIM)
        return t * lax.rsqrt(ms + NORM_EPS) * gain

    q = head_norm(_dot(xn, w_ref[:, 0:D]), qg_ref[...])
    q_ref[...] = (q * (HEAD_DIM ** -0.5)).astype(BF)
    k = head_norm(_dot(xn, w_ref[:, D:2 * D]), kg_ref[...])
    kf_ref[...] = k
    kb_ref[...] = k.astype(BF)
    v = _dot(xn, w_ref[:, 2 * D:3 * D])
    vf_ref[...] = v
    vb_ref[...] = v.astype(BF)
    og_ref[...] = jax.nn.sigmoid(_dot(xn, w_ref[:, 3 * D:4 * D])).astype(BF)
    lf = jax.nn.log_sigmoid(_dot(xn, w_ref[:, 4 * D:4 * D + LANES]) + bf_ref[...])
    lf_ref[...] = lf[:, :n_heads]
    r = _iota((tm, tm), 0)
    c = _iota((tm, tm), 1)
    if seg is None:
        carry_ref, = scratch

        @pl.when(pl.program_id(1) == 0)
        def _():
            carry_ref[...] = jnp.zeros_like(carry_ref)

        cs = _dot_exact_lhs01((c <= r).astype(BF), lf) + carry_ref[...]
        carry_ref[...] = cs[tm - 1:tm, :]
    else:
        cs = _dot_exact_lhs01(((r // seg == c // seg) & (c <= r)).astype(BF), lf)
    c_ref[...] = cs[:, :n_heads]


def _fox_proj(x, g, w_all, q_g, k_g, b_f, bd, *, seg=None):
    B, T, D = x.shape
    H = D // HEAD_DIM
    tm = _tiles(T)["tm"]
    assert T % tm == 0 and (seg is None or (B == 1 and tm == T))
    tok = lambda n: pl.BlockSpec((None, tm, n), lambda b, t: (b, t, 0))
    return pl.pallas_call(
        functools.partial(_fox_proj_kernel, seg=seg),
        grid=(B, T // tm),
        in_specs=[tok(D), _const_spec((None, 1, D), (g[1], 0, 0)), _const_spec(w_all.shape),
                  _const_spec((1, D)), _const_spec((1, D)), _const_spec((1, LANES)), _const_spec((D, D))],
        out_specs=[tok(D), tok(D), tok(D), tok(D), tok(D), tok(D), tok(H), tok(H)],
        out_shape=[jax.ShapeDtypeStruct((B, T, D), BF), jax.ShapeDtypeStruct((B, T, D), F32),
                   jax.ShapeDtypeStruct((B, T, D), BF), jax.ShapeDtypeStruct((B, T, D), F32),
                   jax.ShapeDtypeStruct((B, T, D), BF), jax.ShapeDtypeStruct((B, T, D), BF),
                   jax.ShapeDtypeStruct((B, T, H), F32), jax.ShapeDtypeStruct((B, T, H), F32)],
        scratch_shapes=[] if seg is not None else [pltpu.VMEM((1, LANES), F32)],
        compiler_params=_params(("arbitrary", "arbitrary")),
        name="fox_proj",
    )(x, g[0], w_all, q_g, k_g, b_f, bd)


def _fox_attn_kernel(q_ref, k_ref, v_ref, cq_ref, ck_ref, og_ref, o_ref, qm_ref, m_ref, l_ref, acc_ref, *, tq):
    qi = pl.program_id(2)
    ki = pl.program_id(3)
    first_half = _iota((tq, LANES), 1) < HEAD_DIM

    @pl.when(ki == 0)
    def _():
        q = q_ref[...]
        qm_ref[0] = jnp.where(first_half, q, jnp.zeros_like(q))
        qm_ref[1] = jnp.where(first_half, jnp.zeros_like(q), q)
        m_ref[...] = jnp.full_like(m_ref, NEG_BIG)
        l_ref[...] = jnp.zeros_like(l_ref)
        acc_ref[...] = jnp.zeros_like(acc_ref)

    def update(masked):
        k = k_ref[...]
        v = v_ref[...]
        for h in range(PAIR):
            s = _dot_nt(qm_ref[h], k) + (cq_ref[:, h:h + 1] - ck_ref[h:h + 1, :])
            if masked:
                s = jnp.where(_iota((tq, tq), 0) >= _iota((tq, tq), 1), s, NEG_BIG)
            m_prev = m_ref[h]
            m_new = jnp.maximum(m_prev, jnp.max(s, axis=-1, keepdims=True))
            alpha = jnp.exp(m_prev - m_new)
            p = jnp.exp(s - m_new)
            l_ref[h] = alpha * l_ref[h] + jnp.sum(p, axis=-1, keepdims=True)
            acc_ref[h] = alpha * acc_ref[h] + _dot(p.astype(BF), v)
            m_ref[h] = m_new

    @pl.when(ki < qi)
    def _():
        update(False)

    @pl.when(ki == qi)
    def _():
        update(True)
        o = jnp.where(first_half, acc_ref[0] / l_ref[0], acc_ref[1] / l_ref[1])
        o_ref[...] = (o * og_ref[...].astype(F32)).astype(BF)


def _fox_attn(q, kb, vb, c, og):
    B, T, D = q.shape
    n_pairs = D // LANES
    tq = _tiles(T)["tq"]
    assert T % tq == 0
    c_col = c.reshape(B, T, n_pairs, PAIR).transpose(0, 2, 1, 3)
    c_row = c_col.transpose(0, 1, 3, 2)
    qspec = pl.BlockSpec((None, tq, LANES), lambda b, p, qi, ki: (b, qi, p))
    kspec = pl.BlockSpec((None, tq, LANES), lambda b, p, qi, ki: (b, jnp.minimum(ki, qi), p))
    return pl.pallas_call(
        functools.partial(_fox_attn_kernel, tq=tq),
        grid=(B, n_pairs, T // tq, T // tq),
        in_specs=[qspec, kspec, kspec,
                  pl.BlockSpec((None, None, tq, PAIR), lambda b, p, qi, ki: (b, p, qi, 0)),
                  pl.BlockSpec((None, None, PAIR, tq), lambda b, p, qi, ki: (b, p, 0, jnp.minimum(ki, qi))),
                  qspec],
        out_specs=qspec,
        out_shape=jax.ShapeDtypeStruct((B, T, D), BF),
        scratch_shapes=[pltpu.VMEM((PAIR, tq, LANES), BF), pltpu.VMEM((PAIR, tq, 1), F32),
                        pltpu.VMEM((PAIR, tq, 1), F32), pltpu.VMEM((PAIR, tq, LANES), F32)],
        compiler_params=_params(("arbitrary",) * 4),
        name="fox_attn",
    )(q, kb, vb, c_col, c_row, og)


def _fox_decode_kernel(pt_ref, q_ref, kn_ref, vn_ref, lfn_ref, kp_ref, vp_ref, lfp_ref, og_ref, o_ref,
                       qbd_ref, m_ref, l_ref, acc_ref, carry_ref, *, n_new):
    j = pl.program_id(1)
    D = q_ref.shape[-1]
    n_heads = D // HEAD_DIM
    rows = n_heads * n_new
    page = kp_ref.shape[0]
    assert rows == LANES and page == LANES
    rr = _iota((rows, LANES), 0)
    ll = _iota((rows, LANES), 1)
    expand = (_iota((rows, n_heads), 0) // n_new == _iota((rows, n_heads), 1)).astype(BF)

    def spread(lf):
        return [_dot_nt(expand, piece).astype(BF) for piece in _split3(lf)]

    def online_update(s, v):
        m_prev = m_ref[...]
        m_new = jnp.maximum(m_prev, jnp.max(s, axis=-1, keepdims=True))
        alpha = jnp.exp(m_prev - m_new)
        p = jnp.exp(s - m_new)
        l_ref[...] = alpha * l_ref[...] + jnp.sum(p, axis=-1, keepdims=True)
        acc_ref[...] = alpha * acc_ref[...] + _dot(p.astype(BF), v)
        m_ref[...] = m_new

    @pl.when(j == 0)
    def _():
        q = q_ref[...].astype(F32)
        q_rows = jnp.concatenate([q] * n_heads, axis=0)
        own = _iota((rows, D), 1) // HEAD_DIM == _iota((rows, D), 0) // n_new
        qbd_ref[...] = jnp.where(own, q_rows, 0.0).astype(BF)
        m_ref[...] = jnp.full_like(m_ref, NEG_BIG)
        l_ref[...] = jnp.zeros_like(l_ref)
        acc_ref[...] = jnp.zeros_like(acc_ref)
        pad = page - n_new
        kn = jnp.concatenate([kn_ref[...], jnp.zeros((pad, D), F32)], axis=0).astype(BF)
        vn = jnp.concatenate([vn_ref[...], jnp.zeros((pad, D), F32)], axis=0).astype(BF)
        lfn = jnp.concatenate([lfn_ref[...], jnp.zeros((pad, n_heads), F32)], axis=0)
        incl = (_iota((page, page), 0) <= _iota((page, page), 1)).astype(BF)
        cn = sum(_dot(piece, incl) for piece in spread(lfn))
        t_of_row = rr % n_new
        cn_q = jnp.sum(jnp.where(ll == t_of_row, cn, 0.0), axis=-1, keepdims=True)
        s = _dot_nt(qbd_ref[...], kn) + (cn_q - cn)
        s = jnp.where(ll <= t_of_row, s, NEG_BIG)
        online_update(s, vn)
        carry_ref[...] = cn_q

    @pl.when(j > 0)
    def _():
        kp = kp_ref[...].astype(BF)
        vp = vp_ref[...].astype(BF)
        pieces = spread(lfp_ref[...])
        later = (_iota((page, page), 0) > _iota((page, page), 1)).astype(BF)
        suffix = sum(_dot(piece, later) for piece in pieces)
        total = sum(jnp.sum(piece.astype(F32), axis=-1, keepdims=True) for piece in pieces)
        s = _dot_nt(qbd_ref[...], kp) + (suffix + carry_ref[...])
        online_update(s, vp)
        carry_ref[...] = carry_ref[...] + total

    @pl.when(j == pl.num_programs(1) - 1)
    def _():
        o = acc_ref[...] / l_ref[...]
        tiles = []
        for p in range(D // LANES):
            lo = o[(PAIR * p) * n_new:(PAIR * p + 1) * n_new, p * LANES:(p + 1) * LANES]
            hi = o[(PAIR * p + 1) * n_new:(PAIR * p + 2) * n_new, p * LANES:(p + 1) * LANES]
            tiles.append(jnp.where(_iota((n_new, LANES), 1) < HEAD_DIM, lo, hi))
        o_ref[...] = (jnp.concatenate(tiles, axis=1) * og_ref[...].astype(F32)).astype(BF)


def _fox_decode(q, k_new, v_new, lf_new, og, cache_k, cache_v, cache_lf, page_table):
    DB, n_new, D = q.shape
    H = D // HEAD_DIM
    n_pool, page = cache_k.shape[:2]
    n_pages = page_table.shape[1]
    ck = cache_k.reshape(n_pool, page, D)
    cv = cache_v.reshape(n_pool, page, D)
    new = lambda n: pl.BlockSpec((None, n_new, n), lambda b, j, pt: (b, 0, 0))
    paged = lambda n: pl.BlockSpec((None, page, n), lambda b, j, pt: (pt[b, n_pages - jnp.maximum(j, 1)], 0, 0))
    rows = H * n_new
    return pl.pallas_call(
        functools.partial(_fox_decode_kernel, n_new=n_new),
        grid_spec=pltpu.PrefetchScalarGridSpec(
            num_scalar_prefetch=1,
            grid=(DB, n_pages + 1),
            in_specs=[new(D), new(D), new(D), new(H), paged(D), paged(D), paged(H), new(D)],
            out_specs=new(D),
            scratch_shapes=[pltpu.VMEM((rows, D), BF), pltpu.VMEM((rows, 1), F32), pltpu.VMEM((rows, 1), F32),
                            pltpu.VMEM((rows, D), F32), pltpu.VMEM((rows, 1), F32)],
        ),
        out_shape=jax.ShapeDtypeStruct((DB, n_new, D), BF),
        compiler_params=_params(("arbitrary", "arbitrary")),
        name="fox_decode",
    )(page_table, q, k_new, v_new, lf_new, ck, cv, cache_lf, og)


def _resid_mm_kernel(x_ref, a_ref, w_ref, y_ref):
    y_ref[...] = x_ref[...] + _dot(a_ref[...], w_ref[...])


def _resid_mm(x, a, w):
    B, T, D = x.shape
    K = a.shape[-1]
    tm = _tiles(T)["tm"]
    return pl.pallas_call(
        _resid_mm_kernel,
        grid=(B, T // tm),
        in_specs=[pl.BlockSpec((None, tm, D), lambda b, t: (b, t, 0)),
                  pl.BlockSpec((None, tm, K), lambda b, t: (b, t, 0)),
                  _const_spec((K, D))],
        out_specs=pl.BlockSpec((None, tm, D), lambda b, t: (b, t, 0)),
        out_shape=jax.ShapeDtypeStruct((B, T, D), F32),
        compiler_params=_params(("arbitrary", "arbitrary")),
        name="resid_mm",
    )(x, a, w)


def _rwkv_proj_kernel(x_ref, sh_ref, g_ref, mu_ref, wr_ref, wk_ref, wv_ref, w0_ref, w1_ref, w2_ref,
                      a0_ref, a1_ref, a2_ref, g1_ref, g2_ref, kk_ref, ka_ref, rk_ref, bd_ref,
                      r_o, lw_o, k_o, v_o, kk_o, a_o, g_o, bonus_o, shift_o, *scratch, seg):
    x = x_ref[...]
    tm, D = x.shape
    xn = _rms(x, g_ref[...])
    row = _iota((tm, D), 0)
    if seg is None:
        carry_ref, = scratch

        @pl.when(pl.program_id(1) == 0)
        def _():
            carry_ref[...] = sh_ref[...]

        prev = jnp.where(row == 0, carry_ref[...], pltpu.roll(xn, 1, 0))
        carry_ref[...] = xn[tm - 1:tm, :]
        shift_o[...] = xn[tm - 1:tm, :]
    else:
        prev = jnp.where(row % seg == 0, sh_ref[...], pltpu.roll(xn, 1, 0))
        shift_o[...] = xn.reshape(tm // seg, seg, D)[:, seg - 1, :]
    xx = prev - xn
    xr, xw, xk, xv, xa, xg = ((xn + xx * mu_ref[i:i + 1, :]).astype(BF) for i in range(6))
    r = _dot(xr, wr_ref[...])
    k = _dot(xk, wk_ref[...])
    v = _dot(xv, wv_ref[...])
    lora_w = _dot(jnp.tanh(_dot(xw, w1_ref[...])).astype(BF), w2_ref[...])
    w_log = -jax.nn.softplus(-(w0_ref[...] + lora_w)) - 0.5
    lw_o[...] = -jnp.exp(w_log)
    a = jax.nn.sigmoid(a0_ref[...] + _dot(_dot(xa, a1_ref[...]).astype(BF), a2_ref[...]))
    g_o[...] = _dot(jax.nn.sigmoid(_dot(xg, g1_ref[...])).astype(BF), g2_ref[...]).astype(BF)
    bd = bd_ref[...]
    kk = k * kk_ref[...]
    kk = kk / jnp.maximum(jnp.sqrt(_dot((kk * kk).astype(BF), bd)), 1e-12)
    k = k * (1.0 + (a - 1.0) * ka_ref[...])
    r_o[...] = r
    k_o[...] = k
    v_o[...] = v.astype(BF)
    kk_o[...] = kk
    a_o[...] = a
    bonus_o[...] = _dot((r * k * rk_ref[...]).astype(BF), bd) * v


def _rwkv_proj(x, shift, g, w, bd, *, seg=None):
    B, T, D = x.shape
    tm = _tiles(T)["tm"]
    assert T % tm == 0
    tok = pl.BlockSpec((None, tm, D), lambda b, t: (b, t, 0))
    if seg is None:
        sh_spec = pl.BlockSpec((None, 1, D), lambda b, t: (b, 0, 0))
        shift_shape = jax.ShapeDtypeStruct((B, 1, D), F32)
        shift_spec = pl.BlockSpec((None, 1, D), lambda b, t: (b, 0, 0))
        scratch = [pltpu.VMEM((1, D), F32)]
    else:
        assert B == 1 and tm == T
        sh_spec = pl.BlockSpec((None, T, D), lambda b, t: (0, 0, 0))
        shift_shape = jax.ShapeDtypeStruct((T // seg, D), F32)
        shift_spec = pl.BlockSpec((T // seg, D), lambda b, t: (0, 0))
        scratch = []
    vec = _const_spec((1, D))
    mats = [w[n] for n in ("w_r", "w_k", "w_v")]
    consts = [w["mu"], *mats, w["w0"], w["w1"], w["w2"], w["a0"], w["a1"], w["a2"], w["g1"], w["g2"],
              w["k_k"], w["k_a"], w["r_k"], bd]
    return pl.pallas_call(
        functools.partial(_rwkv_proj_kernel, seg=seg),
        grid=(B, T // tm),
        in_specs=[tok, sh_spec, _const_spec((None, 1, D), (g[1], 0, 0))] + [_const_spec(c.shape) for c in consts],
        out_specs=[tok] * 8 + [shift_spec],
        out_shape=[jax.ShapeDtypeStruct((B, T, D), dt) for dt in (F32, F32, F32, BF, F32, F32, BF, F32)]
        + [shift_shape],
        scratch_shapes=scratch,
        compiler_params=_params(("arbitrary", "arbitrary")),
        name="rwkv_proj",
    )(x, shift, g[0], *consts)


def _rwkv_scan_kernel(r_ref, lw_ref, k_ref, v_ref, kk_ref, a_ref, s0_ref, y_ref, sT_ref, S_ref, *, C):
    D = r_ref.shape[-1]

    @pl.when(pl.program_id(1) == 0)
    def _():
        S_ref[...] = s0_ref[...]

    lw = lw_ref[...]
    tri = (_iota((C, C), 1) <= _iota((C, C), 0)).astype(BF)
    cum = _dot_exact_lhs01(tri, lw)
    cend = cum[C - 1:C, :]
    kk = kk_ref[...]
    kka = kk * a_ref[...]
    k = k_ref[...]
    inv_p = jnp.exp(-cum)
    to_end = jnp.exp(cend - cum)
    at = (-(kk * jnp.exp(cum - lw))).astype(BF)
    bt = (kka * inv_p).astype(BF)
    kt = (k * inv_p).astype(BF)
    rt = (r_ref[...] * jnp.exp(cum)).astype(BF)
    kh = (k * to_end).astype(BF)
    bh = (kka * to_end).astype(BF)
    vb = v_ref[...]
    p_end = jnp.exp(cend)

    first_half = _iota((C, LANES), 1) < HEAD_DIM
    n2 = PAIR * C
    ri = _iota((n2, n2), 0)
    ci = _iota((n2, n2), 1)
    same = ri // C == ci // C
    strict = same & (ci < ri)
    incl = same & (ci <= ri)
    eye = (ri == ci).astype(F32)
    same_head = _iota((LANES, LANES), 0) // HEAD_DIM == _iota((LANES, LANES), 1) // HEAD_DIM

    def stack(t):
        z = jnp.zeros_like(t)
        return jnp.concatenate([jnp.where(first_half, t, z), jnp.where(first_half, z, t)], axis=0)

    for pr in range(D // LANES):
        lanes = slice(pr * LANES, (pr + 1) * LANES)
        at_s, rt_s, kt_s, bt_s, v_s = (stack(t[:, lanes]) for t in (at, rt, kt, bt, vb))
        aa = _dot_nt(jnp.concatenate([at_s, rt_s], axis=0), jnp.concatenate([kt_s, bt_s], axis=0))
        a_ak = jnp.where(strict, aa[:n2, :n2], 0.0).astype(BF)
        a_ab = jnp.where(strict, aa[:n2, n2:], 0.0)
        a_rk = jnp.where(incl, aa[n2:, :n2], 0.0).astype(BF)
        a_rb = jnp.where(incl, aa[n2:, n2:], 0.0).astype(BF)
        inv = eye + a_ab
        npow = a_ab
        for _ in range(int(math.log2(C)) - 1):
            nb = npow.astype(BF)
            npow = _dot(nb, nb)
            inv = inv + _dot(inv.astype(BF), npow.astype(BF))
        S = S_ref[pr]
        Sb = S.astype(BF)
        u = _dot(inv.astype(BF), (_dot_nt(at_s, Sb) + _dot(a_ak, v_s)).astype(BF))
        y = _dot_nt(rt_s, Sb) + _dot(a_rk, v_s) + _dot(a_rb, u.astype(BF))
        y_ref[:, lanes] = y[:C] + y[C:]
        u_pair = (u[:C] + u[C:]).astype(BF)
        dS = _dot_tn(vb[:, lanes], kh[:, lanes]) + _dot_tn(u_pair, bh[:, lanes])
        S_ref[pr] = S * p_end[:, lanes] + jnp.where(same_head, dS, 0.0)

    @pl.when(pl.program_id(1) == pl.num_programs(1) - 1)
    def _():
        sT_ref[...] = S_ref[...]


def _rwkv_scan(r, lw, k, v, kk, a, s0_bd):
    B, T, D = r.shape
    C = RWKV_CHUNK
    assert T % C == 0
    n_pairs = D // LANES
    tok = pl.BlockSpec((None, C, D), lambda b, t: (b, t, 0))
    st = pl.BlockSpec((None, n_pairs, LANES, LANES), lambda b, t: (b, 0, 0, 0))
    return pl.pallas_call(
        functools.partial(_rwkv_scan_kernel, C=C),
        grid=(B, T // C),
        in_specs=[tok] * 6 + [st],
        out_specs=[tok, st],
        out_shape=[jax.ShapeDtypeStruct((B, T, D), F32), jax.ShapeDtypeStruct(s0_bd.shape, F32)],
        scratch_shapes=[pltpu.VMEM((n_pairs, LANES, LANES), F32)],
        compiler_params=_params(("arbitrary", "arbitrary")),
        name="rwkv_scan",
    )(r, lw, k, v, kk, a, s0_bd)


def _state_to_bd(s):
    B, H = s.shape[:2]
    sp = s.reshape(B, H // PAIR, PAIR, HEAD_DIM, HEAD_DIM)
    eye = jnp.eye(PAIR, dtype=s.dtype)
    bd = sp[:, :, :, :, None, :] * eye[None, None, :, None, :, None]
    return bd.reshape(B, H // PAIR, LANES, LANES)


def _bd_to_state(bd):
    B, P = bd.shape[:2]
    t = bd.reshape(B, P, PAIR, HEAD_DIM, PAIR, HEAD_DIM)
    return jnp.stack([t[:, :, h, :, h, :] for h in range(PAIR)], axis=2).reshape(B, P * PAIR, HEAD_DIM, HEAD_DIM)


def _rwkv_post_kernel(x_ref, y_ref, bonus_ref, g_ref, lng_ref, lnb_ref, bd_ref, wo_ref, o_ref):
    y = y_ref[...]
    bd = bd_ref[...]
    d = y - _dot(y.astype(BF), bd) * (1.0 / HEAD_DIM)
    var = _dot((d * d).astype(BF), bd) * (1.0 / HEAD_DIM)
    yn = d * lax.rsqrt(var + LNX_EPS) * lng_ref[...] + lnb_ref[...] + bonus_ref[...]
    o_ref[...] = x_ref[...] + _dot((yn * g_ref[...].astype(F32)).astype(BF), wo_ref[...])


def _rwkv_post(x, y, bonus, g, ln_g, ln_b, bd, w_o):
    B, T, D = x.shape
    tm = _tiles(T)["tm"]
    tok = pl.BlockSpec((None, tm, D), lambda b, t: (b, t, 0))
    return pl.pallas_call(
        _rwkv_post_kernel,
        grid=(B, T // tm),
        in_specs=[tok, tok, tok, tok, _const_spec((1, D)), _const_spec((1, D)), _const_spec((D, D)),
                  _const_spec((D, D))],
        out_specs=tok,
        out_shape=jax.ShapeDtypeStruct((B, T, D), F32),
        compiler_params=_params(("arbitrary", "arbitrary")),
        name="rwkv_post",
    )(x, y, bonus, g, ln_g, ln_b, bd, w_o)


def _pad_cols(w, n):
    return jnp.pad(w, ((0, 0), (0, n - w.shape[1])))


def _pad_rows(w, n):
    return jnp.pad(w, ((0, n - w.shape[0]), (0, 0)))


def kernel(x_prompt, x_sample, cache_k, cache_v, cache_logf, page_table, state_wkv, state_shift, state_conv,
           mix_norm, a_w_in, a_v_norm, a_w_s, a_b_s, a_w_out,
           f_w_qkv, f_q_norm, f_k_norm, f_w_fgate, f_b_fgate, f_w_ogate, f_w_out,
           r_mu, r_w_r, r_w_k, r_w_v, r_w0, r_w1, r_w2, r_a0, r_a1, r_a2, r_g1, r_g2,
           r_k_k, r_k_a, r_r_k, r_lnx_g, r_lnx_b, r_w_o,
           ffn_norm, ffn_w_up, ffn_conv_w, ffn_conv_b, ffn_w_down):
    B, T, D = x_prompt.shape
    DB, n_new, _ = x_sample.shape
    H = D // HEAD_DIM
    depth = ffn_w_up.shape[0]
    F2 = ffn_w_up.shape[-1]
    M = DB * n_new
    row = lambda v: v.reshape(1, -1).astype(F32)

    mix_g = mix_norm.reshape(depth, 1, D)
    ffn_g = ffn_norm.reshape(depth, 1, D)
    ffn_up, ffn_down = ffn_w_up.astype(BF), ffn_w_down.astype(BF)
    ffn_cb = ffn_conv_b.reshape(depth, 1, F2)
    a_in, a_out = a_w_in.astype(BF), a_w_out.astype(BF)
    a_vg = a_v_norm.reshape(a_v_norm.shape[0], 1, -1)
    head_ones = (jnp.arange(D)[:, None] // HEAD_DIM == jnp.arange(D)[None, :] // HEAD_DIM).astype(BF)
    f_all = jnp.concatenate([f_w_qkv, f_w_ogate, _pad_cols(f_w_fgate, LANES)], axis=1).astype(BF)
    f_qg, f_kg = row(jnp.tile(f_q_norm, H)), row(jnp.tile(f_k_norm, H))
    f_bf = _pad_cols(row(f_b_fgate), LANES)
    f_out = f_w_out.astype(BF)
    lora = lambda w1, w2, n: (_pad_cols(w1, n).astype(BF), _pad_rows(w2, n).astype(BF))
    rw = dict(mu=r_mu, w_r=r_w_r.astype(BF), w_k=r_w_k.astype(BF), w_v=r_w_v.astype(BF),
              w0=row(r_w0), a0=row(r_a0), k_k=row(r_k_k), k_a=row(r_k_a), r_k=row(r_r_k))
    rw["w1"], rw["w2"] = lora(r_w1, r_w2, LANES)
    rw["a1"], rw["a2"] = lora(r_a1, r_a2, LANES)
    rw["g1"], rw["g2"] = lora(r_g1, r_g2, 2 * LANES)
    r_out = r_w_o.astype(BF)

    xp = x_prompt
    xs = x_sample.reshape(1, M, D)
    conv_p, conv_s, chunk_v_s = [], [], []
    outs = {}
    for i in range(depth):
        kind = i % 3
        g = (mix_g, i)
        if kind == 0:
            j = i // 3
            ws_p = a_w_s[j]
            bs_p = a_b_s[j][:, :, None]
            xp, = _chunk_mlp(xp, j, g, a_in, a_vg, ws_p, bs_p, a_out, seq=CHUNK, emit_v=False)
            reps = M // n_new
            ws_s = jnp.tile(a_w_s[j][:, :n_new, :n_new], (1, reps, reps))
            bs_s = jnp.tile(a_b_s[j][:, :n_new], (1, reps))[:, :, None]
            xs, v_rows = _chunk_mlp(xs, j, g, a_in, a_vg, ws_s, bs_s, a_out, seq=n_new, emit_v=True)
            chunk_v_s.append(v_rows.reshape(DB, n_new, -1))
        elif kind == 1:
            q, kf, kb, vf, vb, og, lf, c = _fox_proj(xp, g, f_all, f_qg, f_kg, f_bf, head_ones)
            outs["k_p"], outs["v_p"], outs["logf_p"] = (kf.reshape(B, T, H, HEAD_DIM), vf.reshape(B, T, H, HEAD_DIM), lf)
            xp = _resid_mm(xp, _fox_attn(q, kb, vb, c, og), f_out)
            q, kf, _, vf, _, og, lf, _ = _fox_proj(xs, g, f_all, f_qg, f_kg, f_bf, head_ones, seg=n_new)
            shp = (DB, n_new, D)
            outs["k_s"], outs["v_s"] = kf.reshape(DB, n_new, H, HEAD_DIM), vf.reshape(DB, n_new, H, HEAD_DIM)
            outs["logf_s"] = lf.reshape(DB, n_new, H)
            att = _fox_decode(q.reshape(shp), kf.reshape(shp), vf.reshape(shp), outs["logf_s"], og.reshape(shp),
                              cache_k, cache_v, cache_logf, page_table)
            xs = _resid_mm(xs, att.reshape(1, M, D), f_out)
        else:
            r, lw, k, v, kk, a, gate, bonus, shift = _rwkv_proj(xp, jnp.zeros((B, 1, D), F32), g, rw, head_ones)
            y, s_bd = _rwkv_scan(r, lw, k, v, kk, a, jnp.zeros((B, D // LANES, LANES, LANES), F32))
            outs["wkv_p"], outs["shift_p"] = _bd_to_state(s_bd), shift.reshape(B, D)
            xp = _rwkv_post(xp, y, bonus, gate, row(r_lnx_g), row(r_lnx_b), head_ones, r_out)
            sh = jnp.pad(state_shift[:, None, :], ((0, 0), (0, n_new - 1), (0, 0))).reshape(1, M, D)
            r, lw, k, v, kk, a, gate, bonus, shift = _rwkv_proj(xs, sh, g, rw, head_ones, seg=n_new)
            padded = [jnp.pad(t.reshape(DB, n_new, D), ((0, 0), (0, RWKV_CHUNK - n_new), (0, 0)))
                      for t in (r, lw, k, v, kk, a)]
            y, s_bd = _rwkv_scan(*padded, _state_to_bd(state_wkv.astype(F32)))
            outs["wkv_s"], outs["shift_s"] = _bd_to_state(s_bd), shift
            xs = _rwkv_post(xs, y[:, :n_new].reshape(1, M, D), bonus, gate, row(r_lnx_g), row(r_lnx_b), head_ones,
                            r_out)
        ffn_w = (ffn_g, ffn_up, ffn_conv_w, ffn_cb, ffn_down)
        xp, cp = _conv_ffn(xp, jnp.zeros((B, 2, F2), F32), i, *ffn_w)
        st = state_conv[i]
        e2 = jnp.pad(st, ((0, 0), (0, n_new - 2), (0, 0))).reshape(M, F2)
        e1 = jnp.pad(st[:, 1:2], ((0, 0), (0, n_new - 1), (0, 0))).reshape(M, F2)
        xs, cs = _conv_ffn(xs, jnp.stack([e2, e1]), i, *ffn_w, seg=n_new)
        conv_p.append(cp)
        conv_s.append(cs)
    return (xp, xs.reshape(DB, n_new, D), outs["k_p"], outs["v_p"], outs["logf_p"], outs["wkv_p"], outs["shift_p"],
            jnp.stack(conv_p), outs["k_s"], outs["v_s"], outs["logf_s"], outs["wkv_s"], outs["shift_s"],
            jnp.stack(conv_s), jnp.stack(chunk_v_s))
```

```python
import functools
import math

import jax
import jax.numpy as jnp
import numpy as np
from jax import lax
from jax.experimental import pallas as pl
from jax.experimental.pallas import tpu as pltpu

F32 = jnp.float32
BF = jnp.bfloat16

HEAD_DIM = 64
LANES = 128
PAIR = LANES // HEAD_DIM
A_GROUPS = 8
CHUNK = 128
RWKV_CHUNK = 64
NORM_EPS = 1e-6
LNX_EPS = 64e-5
NEG_BIG = -1e30
LOG2E = math.log2(math.e)
N_PIECES = 3
DECODE_PAGES = 4
VMEM_LIMIT = 56 * 1024 * 1024


def _tiles(n_tokens):
    tm = min(n_tokens, 512)
    return dict(tm=tm, tq=min(n_tokens, 512))


def _dot(a, b):
    return jnp.dot(a, b, preferred_element_type=F32)


def _dot_nt(a, b):
    return lax.dot_general(a, b, (((1,), (1,)), ((), ())), preferred_element_type=F32)


def _dot_tn(a, b):
    return lax.dot_general(a, b, (((0,), (0,)), ((), ())), preferred_element_type=F32)


def _split3(x):
    hi = x.astype(BF)
    r1 = x - hi.astype(F32)
    mid = r1.astype(BF)
    lo = (r1 - mid.astype(F32)).astype(BF)
    return hi, mid, lo


def _dot_exact_lhs01(m01, x):
    hi, mid, lo = _split3(x)
    return _dot(m01, hi) + _dot(m01, mid) + _dot(m01, lo)


def _rms(x, g):
    return x * lax.rsqrt(jnp.mean(x * x, axis=-1, keepdims=True) + NORM_EPS) * g


def _iota(shape, axis):
    return lax.broadcasted_iota(jnp.int32, shape, axis)


def _const_spec(shape, index=None):
    nd = len(shape)
    idx = tuple(index) if index is not None else (0,) * nd
    return pl.BlockSpec(shape, lambda *_: idx, pipeline_mode=pl.Buffered(1))


def _params(sem):
    return pltpu.CompilerParams(dimension_semantics=sem, vmem_limit_bytes=VMEM_LIMIT)


def _ffn_kernel(x_ref, st_ref, g_ref, wu_ref, cw_ref, cb_ref, wd_ref, y_ref, ns_ref, *scratch, seg, fc):
    x = x_ref[...]
    tm = x.shape[0]
    d_ff = wd_ref.shape[0]
    xn = _rms(x, g_ref[...]).astype(BF)
    row = _iota((tm, fc), 0)
    if seg is None:
        carry_ref, = scratch

        @pl.when(pl.program_id(1) == 0)
        def _():
            carry_ref[...] = st_ref[...]
    else:
        pos = row % seg
    acc = jnp.zeros(y_ref.shape, F32)
    for c in range(d_ff // fc):
        halves = []
        for off in (c * fc, d_ff + c * fc):
            cols = slice(off, off + fc)
            h = _dot(xn, wu_ref[:, cols])
            if seg is None:
                c0 = carry_ref[0:1, cols]
                c1 = carry_ref[1:2, cols]
                hm1 = jnp.where(row == 0, c1, pltpu.roll(h, 1, 0))
                hm2 = jnp.where(row == 0, c0, jnp.where(row == 1, c1, pltpu.roll(h, 2, 0)))
                carry_ref[:, cols] = h[tm - 2:tm, :]
            else:
                hm1 = jnp.where(pos == 0, st_ref[1, :, cols], pltpu.roll(h, 1, 0))
                hm2 = jnp.where(pos < 2, st_ref[0, :, cols], pltpu.roll(h, 2, 0))
                ns_ref[:, :, cols] = h.reshape(tm // seg, seg, fc)[:, seg - 2:, :]
            halves.append(cb_ref[:, cols] + cw_ref[0:1, cols] * hm2 + cw_ref[1:2, cols] * hm1
                          + cw_ref[2:3, cols] * h)
        gate, val = halves
        act = (gate * jax.nn.sigmoid(gate) * val).astype(BF)
        acc = acc + _dot(act, wd_ref[c * fc:(c + 1) * fc, :])
    y_ref[...] = x + acc
    if seg is None:
        ns_ref[...] = carry_ref[...]


def _conv_ffn(x, state, layer, g, w_up, conv_w, conv_b, w_down, *, seg=None):
    B, T, D = x.shape
    F2 = w_up.shape[-1]
    tm = _tiles(T)["tm"]
    fc = 256
    assert T % tm == 0 and (F2 // 2) % fc == 0
    if seg is None:
        st_spec = pl.BlockSpec((None, 2, F2), lambda b, t: (b, 0, 0))
        ns_shape = jax.ShapeDtypeStruct((B, 2, F2), F32)
        ns_spec = pl.BlockSpec((None, 2, F2), lambda b, t: (b, 0, 0))
        scratch = [pltpu.VMEM((2, F2), F32)]
    else:
        assert B == 1 and tm == T and T % seg == 0
        st_spec = pl.BlockSpec((2, T, F2), lambda b, t: (0, 0, 0))
        ns_shape = jax.ShapeDtypeStruct((T // seg, 2, F2), F32)
        ns_spec = pl.BlockSpec((T // seg, 2, F2), lambda b, t: (0, 0, 0))
        scratch = []
    return pl.pallas_call(
        functools.partial(_ffn_kernel, seg=seg, fc=fc),
        grid=(B, T // tm),
        in_specs=[
            pl.BlockSpec((None, tm, D), lambda b, t: (b, t, 0)),
            st_spec,
            _const_spec((None, 1, D), (layer, 0, 0)),
            _const_spec((None, D, F2), (layer, 0, 0)),
            _const_spec((None, 3, F2), (layer, 0, 0)),
            _const_spec((None, 1, F2), (layer, 0, 0)),
            _const_spec((None, F2 // 2, D), (layer, 0, 0)),
        ],
        out_specs=[pl.BlockSpec((None, tm, D), lambda b, t: (b, t, 0)), ns_spec],
        out_shape=[jax.ShapeDtypeStruct((B, T, D), F32), ns_shape],
        scratch_shapes=scratch,
        compiler_params=_params(("arbitrary", "arbitrary")),
        name="conv_ffn",
    )(x, state, g, w_up, conv_w, conv_b, w_down)


def _cmlp_kernel(x_ref, g_ref, win_ref, vg_ref, ws_ref, bs_ref, wout_ref, y_ref, *v_out, seq, chunk):
    x = x_ref[...]
    tm = x.shape[0]
    d_a = wout_ref.shape[0]
    gd = d_a // A_GROUPS
    xn = _rms(x, g_ref[...]).astype(BF)
    r = _iota((chunk, chunk), 0)
    c = _iota((chunk, chunk), 1)
    keep = (c <= r) if seq == chunk else ((r // seq == c // seq) & (c <= r))
    acc = jnp.zeros(y_ref.shape, F32)
    for g in range(A_GROUPS):
        cols = slice(g * gd, (g + 1) * gd)
        u = jax.nn.gelu(_dot(xn, win_ref[:, cols]))
        v = jax.nn.gelu(_dot(xn, win_ref[:, d_a + g * gd:d_a + (g + 1) * gd]))
        v = v * lax.rsqrt(jnp.mean(v * v, axis=-1, keepdims=True) + NORM_EPS) * vg_ref[:, cols]
        if v_out:
            v_out[0][:, cols] = v
        wm = jnp.where(keep, ws_ref[g], 0.0).astype(BF)
        vb = v.astype(BF)
        parts = [_dot(wm, vb[j * chunk:(j + 1) * chunk, :]) + bs_ref[g] for j in range(tm // chunk)]
        mixed = parts[0] if len(parts) == 1 else jnp.concatenate(parts, axis=0)
        acc = acc + _dot((u * mixed).astype(BF), wout_ref[cols, :])
    y_ref[...] = x + acc


def _chunk_mlp(x, j, g, w_in, v_g, w_s, b_s, w_out, *, seq, emit_v):
    B, T, D = x.shape
    chunk = w_s.shape[-1]
    d_a = w_out.shape[1]
    tm = _tiles(T)["tm"]
    assert T % tm == 0 and tm % chunk == 0
    tok = pl.BlockSpec((None, tm, D), lambda b, t: (b, t, 0))
    out_specs = [tok]
    out_shape = [jax.ShapeDtypeStruct((B, T, D), F32)]
    if emit_v:
        out_specs.append(pl.BlockSpec((None, tm, d_a), lambda b, t: (b, t, 0)))
        out_shape.append(jax.ShapeDtypeStruct((B, T, d_a), F32))
    return pl.pallas_call(
        functools.partial(_cmlp_kernel, seq=seq, chunk=chunk),
        grid=(B, T // tm),
        in_specs=[
            tok,
            _const_spec((None, 1, D), (g[1], 0, 0)),
            _const_spec((None, D, 2 * d_a), (j, 0, 0)),
            _const_spec((None, 1, d_a), (j, 0, 0)),
            _const_spec((A_GROUPS, chunk, chunk)),
            _const_spec((A_GROUPS, chunk, 1)),
            _const_spec((None, d_a, D), (j, 0, 0)),
        ],
        out_specs=out_specs,
        out_shape=out_shape,
        compiler_params=_params(("arbitrary", "arbitrary")),
        name="chunk_mlp",
    )(x, g[0], w_in, v_g, w_s, b_s, w_out)


def _aug_placement(n_heads):
    assert N_PIECES * n_heads < LANES
    w = np.zeros((LANES, 2, n_heads, LANES), np.float32)
    one = N_PIECES * n_heads
    for h in range(n_heads):
        for p in range(N_PIECES):
            w[p * n_heads + h, 0, h, HEAD_DIM + p] = 1.0
            w[one, 0, h, HEAD_DIM + N_PIECES + p] = 1.0
            w[one, 1, h, HEAD_DIM + p] = 1.0
            w[p * n_heads + h, 1, h, HEAD_DIM + N_PIECES + p] = -1.0
    return jnp.asarray(w.reshape(LANES, 2 * n_heads * LANES), BF)


def _fox_proj_kernel(x_ref, g_ref, w_ref, qg_ref, kg_ref, bf_ref, bd_ref, *rest, seg):
    x = x_ref[...]
    tm, D = x.shape
    H = D // HEAD_DIM
    xn = _rms(x, g_ref[...]).astype(BF)
    bd = bd_ref[...]

    def head_norm(t, gain):
        ms = _dot((t * t).astype(BF), bd) * (1.0 / HEAD_DIM)
        return t * lax.rsqrt(ms + NORM_EPS) * gain

    q = head_norm(_dot(xn, w_ref[:, 0:D]), qg_ref[...])
    k = head_norm(_dot(xn, w_ref[:, D:2 * D]), kg_ref[...])
    v = _dot(xn, w_ref[:, 2 * D:3 * D])
    og = jax.nn.sigmoid(_dot(xn, w_ref[:, 3 * D:4 * D])).astype(BF)
    lf = jax.nn.log_sigmoid(_dot(xn, w_ref[:, 4 * D:4 * D + LANES]) + bf_ref[...])
    if seg is not None:
        q_ref, kf_ref, vf_ref, og_ref, lf_ref = rest
        q_ref[...] = (q * (HEAD_DIM ** -0.5)).astype(BF)
    else:
        place_ref, qa_ref, ka_ref, va_ref, kf_ref, vf_ref, og_ref, lf_ref, carry_ref = rest
    kf_ref[...] = k
    vf_ref[...] = v
    og_ref[...] = og
    lf_ref[...] = lf[:, :H]
    if seg is not None:
        return

    @pl.when(pl.program_id(1) == 0)
    def _():
        carry_ref[...] = jnp.zeros_like(carry_ref)

    tri = (_iota((tm, tm), 1) <= _iota((tm, tm), 0)).astype(BF)
    cs = _dot_exact_lhs01(tri, lf) + carry_ref[...]
    carry_ref[...] = cs[tm - 1:tm, :]
    hi, mid, lo = (p.astype(F32) for p in _split3(cs * LOG2E))
    lane = _iota((tm, LANES), 1)
    pieces = jnp.where(lane < H, hi,
                       jnp.where(lane < 2 * H, pltpu.roll(mid, H, 1),
                                 jnp.where(lane < 3 * H, pltpu.roll(lo, 2 * H, 1),
                                           jnp.where(lane == 3 * H, 1.0, 0.0))))
    extras = _dot(pieces.astype(BF), place_ref[...])
    qs = q * (HEAD_DIM ** -0.5 * LOG2E)
    first = lane < HEAD_DIM
    one_lane = jnp.where(lane == HEAD_DIM, 1.0, 0.0)
    for h in range(H):
        pair = slice((h // PAIR) * LANES, (h // PAIR + 1) * LANES)

        def head_tile(t):
            tile = t[:, pair]
            return pltpu.roll(tile, HEAD_DIM, 1) if h % PAIR else tile

        qa_ref[h] = jnp.where(first, head_tile(qs), extras[:, h * LANES:(h + 1) * LANES]).astype(BF)
        ka_ref[h] = jnp.where(first, head_tile(k), extras[:, (H + h) * LANES:(H + h + 1) * LANES]).astype(BF)
        va_ref[h] = jnp.where(first, head_tile(v), one_lane).astype(BF)


def _fox_proj(x, g, w_all, q_g, k_g, b_f, bd, *, seg=None):
    B, T, D = x.shape
    H = D // HEAD_DIM
    tm = _tiles(T)["tm"]
    assert T % tm == 0 and (seg is None or (B == 1 and tm == T))
    tok = lambda n: pl.BlockSpec((None, tm, n), lambda b, t: (b, t, 0))
    consts = [w_all, q_g, k_g, b_f, bd]
    common_specs = [tok(D), tok(D), tok(D), tok(H)]
    common_shape = [jax.ShapeDtypeStruct((B, T, D), F32), jax.ShapeDtypeStruct((B, T, D), F32),
                    jax.ShapeDtypeStruct((B, T, D), BF), jax.ShapeDtypeStruct((B, T, H), F32)]
    if seg is None:
        consts.append(_aug_placement(H))
        heads = pl.BlockSpec((None, H, tm, LANES), lambda b, t: (b, 0, t, 0))
        out_specs = [heads] * 3 + common_specs
        out_shape = [jax.ShapeDtypeStruct((B, H, T, LANES), BF)] * 3 + common_shape
        scratch = [pltpu.VMEM((1, LANES), F32)]
    else:
        out_specs = [tok(D)] + common_specs
        out_shape = [jax.ShapeDtypeStruct((B, T, D), BF)] + common_shape
        scratch = []
    return pl.pallas_call(
        functools.partial(_fox_proj_kernel, seg=seg),
        grid=(B, T // tm),
        in_specs=[tok(D), _const_spec((None, 1, D), (g[1], 0, 0))] + [_const_spec(c.shape) for c in consts],
        out_specs=out_specs,
        out_shape=out_shape,
        scratch_shapes=scratch,
        compiler_params=_params(("arbitrary", "arbitrary")),
        name="fox_proj",
    )(x, g[0], *consts)


def _fox_attn_kernel(qi_ref, ki_ref, q_ref, k_ref, v_ref, og_ref, o_ref, m_ref, acc_ref, *, tq):
    step = pl.program_id(2)
    qi = qi_ref[step]
    ki = ki_ref[step]

    @pl.when(ki == 0)
    def _():
        m_ref[...] = jnp.full_like(m_ref, NEG_BIG)
        acc_ref[...] = jnp.zeros_like(acc_ref)

    def update(masked):
        s = [_dot_nt(q_ref[h], k_ref[h]) for h in range(PAIR)]
        if masked:
            visible = _iota((tq, tq), 0) >= _iota((tq, tq), 1)
            s = [jnp.where(visible, x, NEG_BIG) for x in s]
        m_prev = [m_ref[h] for h in range(PAIR)]
        m_new = [jnp.maximum(mp, jnp.max(x, axis=-1, keepdims=True)) for mp, x in zip(m_prev, s)]
        p = [jnp.exp2(x - pltpu.repeat(mn, tq // LANES, 1)).astype(BF) for x, mn in zip(s, m_new)]
        for h in range(PAIR):
            acc_ref[h] = jnp.exp2(m_prev[h] - m_new[h]) * acc_ref[h] + _dot(p[h], v_ref[h])
            m_ref[h] = m_new[h]

    @pl.when(ki < qi)
    def _():
        update(False)

    @pl.when(ki == qi)
    def _():
        update(True)
        o = [acc_ref[h] for h in range(PAIR)]
        o = [x / x[:, HEAD_DIM:HEAD_DIM + 1] for x in o]
        assert PAIR == 2
        pair = jnp.where(_iota((tq, LANES), 1) < HEAD_DIM, o[0], pltpu.roll(o[1], HEAD_DIM, 1))
        o_ref[...] = (pair * og_ref[...].astype(F32)).astype(BF)


def _fox_attn(qa, ka, va, og):
    B, H, T, _ = qa.shape
    D = og.shape[-1]
    tq = _tiles(T)["tq"]
    assert T % tq == 0 and tq % LANES == 0
    nq = T // tq
    qi_tab = np.asarray([q for q in range(nq) for _ in range(q + 1)], np.int32)
    ki_tab = np.asarray([k for q in range(nq) for k in range(q + 1)], np.int32)
    qspec = pl.BlockSpec((None, PAIR, tq, LANES), lambda b, p, s, qt, kt: (b, p, qt[s], 0))
    kspec = pl.BlockSpec((None, PAIR, tq, LANES), lambda b, p, s, qt, kt: (b, p, kt[s], 0))
    ospec = pl.BlockSpec((None, tq, LANES), lambda b, p, s, qt, kt: (b, qt[s], p))
    return pl.pallas_call(
        functools.partial(_fox_attn_kernel, tq=tq),
        grid_spec=pltpu.PrefetchScalarGridSpec(
            num_scalar_prefetch=2,
            grid=(B, H // PAIR, len(qi_tab)),
            in_specs=[qspec, kspec, kspec, ospec],
            out_specs=ospec,
            scratch_shapes=[pltpu.VMEM((PAIR, tq, LANES), F32), pltpu.VMEM((PAIR, tq, LANES), F32)],
        ),
        out_shape=jax.ShapeDtypeStruct((B, T, D), BF),
        compiler_params=_params(("arbitrary",) * 3),
        name="fox_attn",
    )(jnp.asarray(qi_tab), jnp.asarray(ki_tab), qa, ka, va, og)


def _fox_decode_kernel(pt_ref, q_ref, kn_ref, vn_ref, lfn_ref, *rest, n_new, group):
    kp_refs, vp_refs, lfp_refs = rest[:group], rest[group:2 * group], rest[2 * group:3 * group]
    og_ref, o_ref, qbd_ref, m_ref, l_ref, acc_ref, carry_ref = rest[3 * group:]
    j = pl.program_id(1)
    D = q_ref.shape[-1]
    n_heads = D // HEAD_DIM
    rows = n_heads * n_new
    page = kp_refs[0].shape[0]
    assert rows == LANES and page == LANES
    rr = _iota((rows, LANES), 0)
    ll = _iota((rows, LANES), 1)
    expand = (_iota((rows, n_heads), 0) // n_new == _iota((rows, n_heads), 1)).astype(BF)

    def spread(lf):
        return [_dot_nt(expand, piece).astype(BF) for piece in _split3(lf)]

    def online_update(s, v):
        m_prev = m_ref[...]
        m_new = jnp.maximum(m_prev, jnp.max(s, axis=-1, keepdims=True))
        alpha = jnp.exp(m_prev - m_new)
        p = jnp.exp(s - m_new)
        l_ref[...] = alpha * l_ref[...] + jnp.sum(p, axis=-1, keepdims=True)
        acc_ref[...] = alpha * acc_ref[...] + _dot(p.astype(BF), v)
        m_ref[...] = m_new

    @pl.when(j == 0)
    def _():
        q = q_ref[...].astype(F32)
        q_rows = jnp.concatenate([q] * n_heads, axis=0)
        own = _iota((rows, D), 1) // HEAD_DIM == _iota((rows, D), 0) // n_new
        qbd_ref[...] = jnp.where(own, q_rows, 0.0).astype(BF)
        m_ref[...] = jnp.full_like(m_ref, NEG_BIG)
        l_ref[...] = jnp.zeros_like(l_ref)
        acc_ref[...] = jnp.zeros_like(acc_ref)
        pad = page - n_new
        kn = jnp.concatenate([kn_ref[...], jnp.zeros((pad, D), F32)], axis=0).astype(BF)
        vn = jnp.concatenate([vn_ref[...], jnp.zeros((pad, D), F32)], axis=0).astype(BF)
        lfn = jnp.concatenate([lfn_ref[...], jnp.zeros((pad, n_heads), F32)], axis=0)
        incl = (_iota((page, page), 0) <= _iota((page, page), 1)).astype(BF)
        cn = sum(_dot(piece, incl) for piece in spread(lfn))
        t_of_row = rr % n_new
        cn_q = jnp.sum(jnp.where(ll == t_of_row, cn, 0.0), axis=-1, keepdims=True)
        s = _dot_nt(qbd_ref[...], kn) + (cn_q - cn)
        s = jnp.where(ll <= t_of_row, s, NEG_BIG)
        online_update(s, vn)
        carry_ref[...] = cn_q

    @pl.when(j > 0)
    def _():
        kp = jnp.concatenate([r[...].astype(BF) for r in kp_refs], axis=0)
        vp = jnp.concatenate([r[...].astype(BF) for r in vp_refs], axis=0)
        pieces = spread(jnp.concatenate([r[...] for r in lfp_refs], axis=0))
        n_keys = group * page
        later = (_iota((n_keys, n_keys), 0) > _iota((n_keys, n_keys), 1)).astype(BF)
        suffix = sum(_dot(piece, later) for piece in pieces)
        total = sum(jnp.sum(piece.astype(F32), axis=-1, keepdims=True) for piece in pieces)
        s = _dot_nt(qbd_ref[...], kp) + (suffix + carry_ref[...])
        online_update(s, vp)
        carry_ref[...] = carry_ref[...] + total

    @pl.when(j == pl.num_programs(1) - 1)
    def _():
        o = acc_ref[...] / l_ref[...]
        tiles = []
        for p in range(D // LANES):
            lo = o[(PAIR * p) * n_new:(PAIR * p + 1) * n_new, p * LANES:(p + 1) * LANES]
            hi = o[(PAIR * p + 1) * n_new:(PAIR * p + 2) * n_new, p * LANES:(p + 1) * LANES]
            tiles.append(jnp.where(_iota((n_new, LANES), 1) < HEAD_DIM, lo, hi))
        o_ref[...] = (jnp.concatenate(tiles, axis=1) * og_ref[...].astype(F32)).astype(BF)


def _fox_decode(q, k_new, v_new, lf_new, og, cache_k, cache_v, cache_lf, page_table):
    DB, n_new, D = q.shape
    H = D // HEAD_DIM
    n_pool, page = cache_k.shape[:2]
    n_pages = page_table.shape[1]
    ck = cache_k.reshape(n_pool, page, D)
    cv = cache_v.reshape(n_pool, page, D)
    group = DECODE_PAGES
    assert n_pages % group == 0
    new = lambda n: pl.BlockSpec((None, n_new, n), lambda b, j, pt: (b, 0, 0))

    def paged(n):
        return [pl.BlockSpec((None, page, n),
                             lambda b, j, pt, g=g: (pt[b, n_pages - group * jnp.maximum(j, 1) + g], 0, 0))
                for g in range(group)]

    rows = H * n_new
    return pl.pallas_call(
        functools.partial(_fox_decode_kernel, n_new=n_new, group=group),
        grid_spec=pltpu.PrefetchScalarGridSpec(
            num_scalar_prefetch=1,
            grid=(DB, n_pages // group + 1),
            in_specs=[new(D), new(D), new(D), new(H), *paged(D), *paged(D), *paged(H), new(D)],
            out_specs=new(D),
            scratch_shapes=[pltpu.VMEM((rows, D), BF), pltpu.VMEM((rows, 1), F32), pltpu.VMEM((rows, 1), F32),
                            pltpu.VMEM((rows, D), F32), pltpu.VMEM((rows, 1), F32)],
        ),
        out_shape=jax.ShapeDtypeStruct((DB, n_new, D), BF),
        compiler_params=_params(("arbitrary", "arbitrary")),
        name="fox_decode",
    )(page_table, q, k_new, v_new, lf_new, *[ck] * group, *[cv] * group, *[cache_lf] * group, og)


def _resid_mm_kernel(x_ref, a_ref, w_ref, y_ref):
    y_ref[...] = x_ref[...] + _dot(a_ref[...], w_ref[...])


def _resid_mm(x, a, w):
    B, T, D = x.shape
    K = a.shape[-1]
    tm = _tiles(T)["tm"]
    return pl.pallas_call(
        _resid_mm_kernel,
        grid=(B, T // tm),
        in_specs=[pl.BlockSpec((None, tm, D), lambda b, t: (b, t, 0)),
                  pl.BlockSpec((None, tm, K), lambda b, t: (b, t, 0)),
                  _const_spec((K, D))],
        out_specs=pl.BlockSpec((None, tm, D), lambda b, t: (b, t, 0)),
        out_shape=jax.ShapeDtypeStruct((B, T, D), F32),
        compiler_params=_params(("arbitrary", "arbitrary")),
        name="resid_mm",
    )(x, a, w)


def _rwkv_proj_kernel(x_ref, sh_ref, g_ref, mu_ref, wr_ref, wk_ref, wv_ref, w0_ref, w1_ref, w2_ref,
                      a0_ref, a1_ref, a2_ref, g1_ref, g2_ref, kk_ref, ka_ref, rk_ref, bd_ref,
                      r_o, lw_o, k_o, v_o, kk_o, a_o, g_o, bonus_o, shift_o, *scratch, seg):
    x = x_ref[...]
    tm, D = x.shape
    xn = _rms(x, g_ref[...])
    row = _iota((tm, D), 0)
    if seg is None:
        carry_ref, = scratch

        @pl.when(pl.program_id(1) == 0)
        def _():
            carry_ref[...] = sh_ref[...]

        prev = jnp.where(row == 0, carry_ref[...], pltpu.roll(xn, 1, 0))
        carry_ref[...] = xn[tm - 1:tm, :]
        shift_o[...] = xn[tm - 1:tm, :]
    else:
        prev = jnp.where(row % seg == 0, sh_ref[...], pltpu.roll(xn, 1, 0))
        shift_o[...] = xn.reshape(tm // seg, seg, D)[:, seg - 1, :]
    xx = prev - xn
    xr, xw, xk, xv, xa, xg = ((xn + xx * mu_ref[i:i + 1, :]).astype(BF) for i in range(6))
    r = _dot(xr, wr_ref[...])
    k = _dot(xk, wk_ref[...])
    v = _dot(xv, wv_ref[...])
    lora_w = _dot(jnp.tanh(_dot(xw, w1_ref[...])).astype(BF), w2_ref[...])
    w_log = -jax.nn.softplus(-(w0_ref[...] + lora_w)) - 0.5
    lw_o[...] = -jnp.exp(w_log)
    a = jax.nn.sigmoid(a0_ref[...] + _dot(_dot(xa, a1_ref[...]).astype(BF), a2_ref[...]))
    g_o[...] = _dot(jax.nn.sigmoid(_dot(xg, g1_ref[...])).astype(BF), g2_ref[...]).astype(BF)
    bd = bd_ref[...]
    kk = k * kk_ref[...]
    kk = kk / jnp.maximum(jnp.sqrt(_dot((kk * kk).astype(BF), bd)), 1e-12)
    k = k * (1.0 + (a - 1.0) * ka_ref[...])
    r_o[...] = r
    k_o[...] = k
    v_o[...] = v.astype(BF)
    kk_o[...] = kk
    a_o[...] = a
    bonus_o[...] = _dot((r * k * rk_ref[...]).astype(BF), bd) * v


def _rwkv_proj(x, shift, g, w, bd, *, seg=None):
    B, T, D = x.shape
    tm = _tiles(T)["tm"]
    assert T % tm == 0
    tok = pl.BlockSpec((None, tm, D), lambda b, t: (b, t, 0))
    if seg is None:
        sh_spec = pl.BlockSpec((None, 1, D), lambda b, t: (b, 0, 0))
        shift_shape = jax.ShapeDtypeStruct((B, 1, D), F32)
        shift_spec = pl.BlockSpec((None, 1, D), lambda b, t: (b, 0, 0))
        scratch = [pltpu.VMEM((1, D), F32)]
    else:
        assert B == 1 and tm == T
        sh_spec = pl.BlockSpec((None, T, D), lambda b, t: (0, 0, 0))
        shift_shape = jax.ShapeDtypeStruct((T // seg, D), F32)
        shift_spec = pl.BlockSpec((T // seg, D), lambda b, t: (0, 0))
        scratch = []
    vec = _const_spec((1, D))
    mats = [w[n] for n in ("w_r", "w_k", "w_v")]
    consts = [w["mu"], *mats, w["w0"], w["w1"], w["w2"], w["a0"], w["a1"], w["a2"], w["g1"], w["g2"],
              w["k_k"], w["k_a"], w["r_k"], bd]
    return pl.pallas_call(
        functools.partial(_rwkv_proj_kernel, seg=seg),
        grid=(B, T // tm),
        in_specs=[tok, sh_spec, _const_spec((None, 1, D), (g[1], 0, 0))] + [_const_spec(c.shape) for c in consts],
        out_specs=[tok] * 8 + [shift_spec],
        out_shape=[jax.ShapeDtypeStruct((B, T, D), dt) for dt in (F32, F32, F32, BF, F32, F32, BF, F32)]
        + [shift_shape],
        scratch_shapes=scratch,
        compiler_params=_params(("arbitrary", "arbitrary")),
        name="rwkv_proj",
    )(x, shift, g[0], *consts)


def _rwkv_scan_kernel(r_ref, lw_ref, k_ref, v_ref, kk_ref, a_ref, s0_ref, y_ref, sT_ref, S_ref, *, C):
    D = r_ref.shape[-1]

    @pl.when(pl.program_id(1) == 0)
    def _():
        S_ref[...] = s0_ref[...]

    lw = lw_ref[...]
    tri = (_iota((C, C), 1) <= _iota((C, C), 0)).astype(BF)
    cum = _dot_exact_lhs01(tri, lw)
    cend = cum[C - 1:C, :]
    kk = kk_ref[...]
    kka = kk * a_ref[...]
    k = k_ref[...]
    inv_p = jnp.exp(-cum)
    to_end = jnp.exp(cend - cum)
    at = (-(kk * jnp.exp(cum - lw))).astype(BF)
    bt = (kka * inv_p).astype(BF)
    kt = (k * inv_p).astype(BF)
    rt = (r_ref[...] * jnp.exp(cum)).astype(BF)
    kh = (k * to_end).astype(BF)
    bh = (kka * to_end).astype(BF)
    vb = v_ref[...]
    p_end = jnp.exp(cend)

    first_half = _iota((C, LANES), 1) < HEAD_DIM
    n2 = PAIR * C
    ri = _iota((n2, n2), 0)
    ci = _iota((n2, n2), 1)
    same = ri // C == ci // C
    strict = same & (ci < ri)
    incl = same & (ci <= ri)
    eye = (ri == ci).astype(F32)
    same_head = _iota((LANES, LANES), 0) // HEAD_DIM == _iota((LANES, LANES), 1) // HEAD_DIM

    def stack(t):
        z = jnp.zeros_like(t)
        return jnp.concatenate([jnp.where(first_half, t, z), jnp.where(first_half, z, t)], axis=0)

    pairs = range(D // LANES)
    lanes = [slice(pr * LANES, (pr + 1) * LANES) for pr in pairs]
    at_s, rt_s, kt_s, bt_s, v_s = ([stack(t[:, ln]) for ln in lanes] for t in (at, rt, kt, bt, vb))
    aa = [_dot_nt(jnp.concatenate([at_s[pr], rt_s[pr]], axis=0), jnp.concatenate([kt_s[pr], bt_s[pr]], axis=0))
          for pr in pairs]
    a_ak = [jnp.where(strict, x[:n2, :n2], 0.0).astype(BF) for x in aa]
    a_ab = [jnp.where(strict, x[:n2, n2:], 0.0) for x in aa]
    a_rk = [jnp.where(incl, x[n2:, :n2], 0.0).astype(BF) for x in aa]
    a_rb = [jnp.where(incl, x[n2:, n2:], 0.0).astype(BF) for x in aa]
    inv = [eye + x for x in a_ab]
    npow = a_ab
    for _ in range(int(math.log2(C)) - 1):
        nb = [x.astype(BF) for x in npow]
        npow = [_dot(x, x) for x in nb]
        inv = [i + _dot(i.astype(BF), n.astype(BF)) for i, n in zip(inv, npow)]
    S = [S_ref[pr] for pr in pairs]
    Sb = [x.astype(BF) for x in S]
    rhs = [(_dot_nt(at_s[pr], Sb[pr]) + _dot(a_ak[pr], v_s[pr])).astype(BF) for pr in pairs]
    u = [_dot(inv[pr].astype(BF), rhs[pr]) for pr in pairs]
    y = [_dot_nt(rt_s[pr], Sb[pr]) + _dot(a_rk[pr], v_s[pr]) + _dot(a_rb[pr], u[pr].astype(BF)) for pr in pairs]
    for pr in pairs:
        y_ref[:, lanes[pr]] = y[pr][:C] + y[pr][C:]
        u_pair = (u[pr][:C] + u[pr][C:]).astype(BF)
        dS = _dot_tn(vb[:, lanes[pr]], kh[:, lanes[pr]]) + _dot_tn(u_pair, bh[:, lanes[pr]])
        S_ref[pr] = S[pr] * p_end[:, lanes[pr]] + jnp.where(same_head, dS, 0.0)

    @pl.when(pl.program_id(1) == pl.num_programs(1) - 1)
    def _():
        sT_ref[...] = S_ref[...]


def _rwkv_scan(r, lw, k, v, kk, a, s0_bd):
    B, T, D = r.shape
    C = RWKV_CHUNK
    assert T % C == 0
    n_pairs = D // LANES
    tok = pl.BlockSpec((None, C, D), lambda b, t: (b, t, 0))
    st = pl.BlockSpec((None, n_pairs, LANES, LANES), lambda b, t: (b, 0, 0, 0))
    return pl.pallas_call(
        functools.partial(_rwkv_scan_kernel, C=C),
        grid=(B, T // C),
        in_specs=[tok] * 6 + [st],
        out_specs=[tok, st],
        out_shape=[jax.ShapeDtypeStruct((B, T, D), F32), jax.ShapeDtypeStruct(s0_bd.shape, F32)],
        scratch_shapes=[pltpu.VMEM((n_pairs, LANES, LANES), F32)],
        compiler_params=_params(("arbitrary", "arbitrary")),
        name="rwkv_scan",
    )(r, lw, k, v, kk, a, s0_bd)


def _state_to_bd(s):
    B, H = s.shape[:2]
    sp = s.reshape(B, H // PAIR, PAIR, HEAD_DIM, HEAD_DIM)
    eye = jnp.eye(PAIR, dtype=s.dtype)
    bd = sp[:, :, :, :, None, :] * eye[None, None, :, None, :, None]
    return bd.reshape(B, H // PAIR, LANES, LANES)


def _bd_to_state(bd):
    B, P = bd.shape[:2]
    t = bd.reshape(B, P, PAIR, HEAD_DIM, PAIR, HEAD_DIM)
    return jnp.stack([t[:, :, h, :, h, :] for h in range(PAIR)], axis=2).reshape(B, P * PAIR, HEAD_DIM, HEAD_DIM)


def _rwkv_post_kernel(x_ref, y_ref, bonus_ref, g_ref, lng_ref, lnb_ref, bd_ref, wo_ref, o_ref):
    y = y_ref[...]
    bd = bd_ref[...]
    d = y - _dot(y.astype(BF), bd) * (1.0 / HEAD_DIM)
    var = _dot((d * d).astype(BF), bd) * (1.0 / HEAD_DIM)
    yn = d * lax.rsqrt(var + LNX_EPS) * lng_ref[...] + lnb_ref[...] + bonus_ref[...]
    o_ref[...] = x_ref[...] + _dot((yn * g_ref[...].astype(F32)).astype(BF), wo_ref[...])


def _rwkv_post(x, y, bonus, g, ln_g, ln_b, bd, w_o):
    B, T, D = x.shape
    tm = _tiles(T)["tm"]
    tok = pl.BlockSpec((None, tm, D), lambda b, t: (b, t, 0))
    return pl.pallas_call(
        _rwkv_post_kernel,
        grid=(B, T // tm),
        in_specs=[tok, tok, tok, tok, _const_spec((1, D)), _const_spec((1, D)), _const_spec((D, D)),
                  _const_spec((D, D))],
        out_specs=tok,
        out_shape=jax.ShapeDtypeStruct((B, T, D), F32),
        compiler_params=_params(("arbitrary", "arbitrary")),
        name="rwkv_post",
    )(x, y, bonus, g, ln_g, ln_b, bd, w_o)


def _pad_cols(w, n):
    return jnp.pad(w, ((0, 0), (0, n - w.shape[1])))


def _pad_rows(w, n):
    return jnp.pad(w, ((0, n - w.shape[0]), (0, 0)))


def kernel(x_prompt, x_sample, cache_k, cache_v, cache_logf, page_table, state_wkv, state_shift, state_conv,
           mix_norm, a_w_in, a_v_norm, a_w_s, a_b_s, a_w_out,
           f_w_qkv, f_q_norm, f_k_norm, f_w_fgate, f_b_fgate, f_w_ogate, f_w_out,
           r_mu, r_w_r, r_w_k, r_w_v, r_w0, r_w1, r_w2, r_a0, r_a1, r_a2, r_g1, r_g2,
           r_k_k, r_k_a, r_r_k, r_lnx_g, r_lnx_b, r_w_o,
           ffn_norm, ffn_w_up, ffn_conv_w, ffn_conv_b, ffn_w_down):
    B, T, D = x_prompt.shape
    DB, n_new, _ = x_sample.shape
    H = D // HEAD_DIM
    depth = ffn_w_up.shape[0]
    F2 = ffn_w_up.shape[-1]
    M = DB * n_new
    row = lambda v: v.reshape(1, -1).astype(F32)

    mix_g = mix_norm.reshape(depth, 1, D)
    ffn_g = ffn_norm.reshape(depth, 1, D)
    ffn_up, ffn_down = ffn_w_up.astype(BF), ffn_w_down.astype(BF)
    ffn_cb = ffn_conv_b.reshape(depth, 1, F2)
    a_in, a_out = a_w_in.astype(BF), a_w_out.astype(BF)
    a_vg = a_v_norm.reshape(a_v_norm.shape[0], 1, -1)
    head_ones = (jnp.arange(D)[:, None] // HEAD_DIM == jnp.arange(D)[None, :] // HEAD_DIM).astype(BF)
    f_all = jnp.concatenate([f_w_qkv, f_w_ogate, _pad_cols(f_w_fgate, LANES)], axis=1).astype(BF)
    f_qg, f_kg = row(jnp.tile(f_q_norm, H)), row(jnp.tile(f_k_norm, H))
    f_bf = _pad_cols(row(f_b_fgate), LANES)
    f_out = f_w_out.astype(BF)
    lora = lambda w1, w2, n: (_pad_cols(w1, n).astype(BF), _pad_rows(w2, n).astype(BF))
    rw = dict(mu=r_mu, w_r=r_w_r.astype(BF), w_k=r_w_k.astype(BF), w_v=r_w_v.astype(BF),
              w0=row(r_w0), a0=row(r_a0), k_k=row(r_k_k), k_a=row(r_k_a), r_k=row(r_r_k))
    rw["w1"], rw["w2"] = lora(r_w1, r_w2, LANES)
    rw["a1"], rw["a2"] = lora(r_a1, r_a2, LANES)
    rw["g1"], rw["g2"] = lora(r_g1, r_g2, 2 * LANES)
    r_out = r_w_o.astype(BF)

    xp = x_prompt
    xs = x_sample.reshape(1, M, D)
    conv_p, conv_s, chunk_v_s = [], [], []
    outs = {}
    for i in range(depth):
        kind = i % 3
        g = (mix_g, i)
        if kind == 0:
            j = i // 3
            ws_p = a_w_s[j]
            bs_p = a_b_s[j][:, :, None]
            xp, = _chunk_mlp(xp, j, g, a_in, a_vg, ws_p, bs_p, a_out, seq=CHUNK, emit_v=False)
            reps = M // n_new
            ws_s = jnp.tile(a_w_s[j][:, :n_new, :n_new], (1, reps, reps))
            bs_s = jnp.tile(a_b_s[j][:, :n_new], (1, reps))[:, :, None]
            xs, v_rows = _chunk_mlp(xs, j, g, a_in, a_vg, ws_s, bs_s, a_out, seq=n_new, emit_v=True)
            chunk_v_s.append(v_rows.reshape(DB, n_new, -1))
        elif kind == 1:
            qa, ka, va, kf, vf, og, lf = _fox_proj(xp, g, f_all, f_qg, f_kg, f_bf, head_ones)
            outs["k_p"], outs["v_p"], outs["logf_p"] = (kf.reshape(B, T, H, HEAD_DIM), vf.reshape(B, T, H, HEAD_DIM), lf)
            xp = _resid_mm(xp, _fox_attn(qa, ka, va, og), f_out)
            q, kf, vf, og, lf = _fox_proj(xs, g, f_all, f_qg, f_kg, f_bf, head_ones, seg=n_new)
            shp = (DB, n_new, D)
            outs["k_s"], outs["v_s"] = kf.reshape(DB, n_new, H, HEAD_DIM), vf.reshape(DB, n_new, H, HEAD_DIM)
            outs["logf_s"] = lf.reshape(DB, n_new, H)
            att = _fox_decode(q.reshape(shp), kf.reshape(shp), vf.reshape(shp), outs["logf_s"], og.reshape(shp),
                              cache_k, cache_v, cache_logf, page_table)
            xs = _resid_mm(xs, att.reshape(1, M, D), f_out)
        else:
            r, lw, k, v, kk, a, gate, bonus, shift = _rwkv_proj(xp, jnp.zeros((B, 1, D), F32), g, rw, head_ones)
            y, s_bd = _rwkv_scan(r, lw, k, v, kk, a, jnp.zeros((B, D // LANES, LANES, LANES), F32))
            outs["wkv_p"], outs["shift_p"] = _bd_to_state(s_bd), shift.reshape(B, D)
            xp = _rwkv_post(xp, y, bonus, gate, row(r_lnx_g), row(r_lnx_b), head_ones, r_out)
            sh = jnp.pad(state_shift[:, None, :], ((0, 0), (0, n_new - 1), (0, 0))).reshape(1, M, D)
            r, lw, k, v, kk, a, gate, bonus, shift = _rwkv_proj(xs, sh, g, rw, head_ones, seg=n_new)
            padded = [jnp.pad(t.reshape(DB, n_new, D), ((0, 0), (0, RWKV_CHUNK - n_new), (0, 0)))
                      for t in (r, lw, k, v, kk, a)]
            y, s_bd = _rwkv_scan(*padded, _state_to_bd(state_wkv.astype(F32)))
            outs["wkv_s"], outs["shift_s"] = _bd_to_state(s_bd), shift
            xs = _rwkv_post(xs, y[:, :n_new].reshape(1, M, D), bonus, gate, row(r_lnx_g), row(r_lnx_b), head_ones,
                            r_out)
        ffn_w = (ffn_g, ffn_up, ffn_conv_w, ffn_cb, ffn_down)
        xp, cp = _conv_ffn(xp, jnp.zeros((B, 2, F2), F32), i, *ffn_w)
        st = state_conv[i]
        e2 = jnp.pad(st, ((0, 0), (0, n_new - 2), (0, 0))).reshape(M, F2)
        e1 = jnp.pad(st[:, 1:2], ((0, 0), (0, n_new - 1), (0, 0))).reshape(M, F2)
        xs, cs = _conv_ffn(xs, jnp.stack([e2, e1]), i, *ffn_w, seg=n_new)
        conv_p.append(cp)
        conv_s.append(cs)
    return (xp, xs.reshape(DB, n_new, D), outs["k_p"], outs["v_p"], outs["logf_p"], outs["wkv_p"], outs["shift_p"],
            jnp.stack(conv_p), outs["k_s"], outs["v_s"], outs["logf_s"], outs["wkv_s"], outs["shift_s"],
            jnp.stack(conv_s), jnp.stack(chunk_v_s))
```

```python
import functools
import math

import jax
import jax.numpy as jnp
import numpy as np
from jax import lax
from jax.experimental import pallas as pl
from jax.experimental.pallas import tpu as pltpu

F32 = jnp.float32
BF = jnp.bfloat16

HEAD_DIM = 64
LANES = 128
SUBLANES = 8
PAIR = LANES // HEAD_DIM
A_GROUPS = 8
CHUNK = 128
RWKV_CHUNK = 64
NORM_EPS = 1e-6
LNX_EPS = 64e-5
NEG_BIG = -1e30
LOG2E = math.log2(math.e)
N_PIECES = 3
DECODE_PAGES = 4
VMEM_LIMIT = 56 * 1024 * 1024


def _tiles(n_tokens):
    tm = min(n_tokens, 512)
    return dict(tm=tm, tq=min(n_tokens, 1024))


def _dot(a, b):
    return jnp.dot(a, b, preferred_element_type=F32)


def _dot_nt(a, b):
    return lax.dot_general(a, b, (((1,), (1,)), ((), ())), preferred_element_type=F32)


def _dot_tn(a, b):
    return lax.dot_general(a, b, (((0,), (0,)), ((), ())), preferred_element_type=F32)


def _split3(x):
    hi = x.astype(BF)
    r1 = x - hi.astype(F32)
    mid = r1.astype(BF)
    lo = (r1 - mid.astype(F32)).astype(BF)
    return hi, mid, lo


def _dot_exact_lhs01(m01, x):
    hi, mid, lo = _split3(x)
    return _dot(m01, hi) + _dot(m01, mid) + _dot(m01, lo)


def _rms(x, g):
    return x * lax.rsqrt(jnp.mean(x * x, axis=-1, keepdims=True) + NORM_EPS) * g


def _iota(shape, axis):
    return lax.broadcasted_iota(jnp.int32, shape, axis)


def _const_spec(shape, index=None):
    nd = len(shape)
    idx = tuple(index) if index is not None else (0,) * nd
    return pl.BlockSpec(shape, lambda *_: idx, pipeline_mode=pl.Buffered(1))


def _params(sem):
    return pltpu.CompilerParams(dimension_semantics=sem, vmem_limit_bytes=VMEM_LIMIT)


def _ffn_kernel(x_ref, st_ref, g_ref, wu_ref, cw_ref, cb_ref, wd_ref, y_ref, ns_ref, hbuf_ref, act_ref, *, seg, fc,
                down_group):
    x = x_ref[...]
    tm = x.shape[0]
    d_ff = wd_ref.shape[0]
    xn = _rms(x, g_ref[...]).astype(BF)
    top = SUBLANES
    prev = slice(top - 2, top)
    if seg is None:
        @pl.when(pl.program_id(1) == 0)
        def _():
            hbuf_ref[prev, :] = st_ref[...]
    else:
        hbuf_ref[prev, :] = jnp.zeros((2, hbuf_ref.shape[1]), F32)
        pos = _iota((tm, fc), 0) % seg
    acc = jnp.zeros(y_ref.shape, F32)
    n_chunks = d_ff // fc
    offsets = lambda c: (c * fc, d_ff + c * fc)
    up = lambda c: [_dot(xn, wu_ref[:, off:off + fc]) for off in offsets(c)]
    h_next = up(0)
    for c in range(n_chunks):
        h_cur, h_next = h_next, (up(c + 1) if c + 1 < n_chunks else None)
        halves = []
        for off, h in zip(offsets(c), h_cur):
            cols = slice(off, off + fc)
            hbuf_ref[top:top + tm, cols] = h
            hm1 = hbuf_ref[top - 1:top - 1 + tm, cols]
            hm2 = hbuf_ref[top - 2:top - 2 + tm, cols]
            if seg is not None:
                hm1 = jnp.where(pos == 0, st_ref[1, :, cols], hm1)
                hm2 = jnp.where(pos < 2, st_ref[0, :, cols], hm2)
                ns_ref[:, :, cols] = h.reshape(tm // seg, seg, fc)[:, seg - 2:, :]
            halves.append(cb_ref[:, cols] + cw_ref[0:1, cols] * hm2 + cw_ref[1:2, cols] * hm1
                          + cw_ref[2:3, cols] * h)
        gate, val = halves
        act_ref[:, c * fc:(c + 1) * fc] = (gate * jax.nn.sigmoid(gate) * val).astype(BF)
        if (c + 1) % down_group == 0 or c + 1 == n_chunks:
            rows = slice((c // down_group) * down_group * fc, (c + 1) * fc)
            acc = acc + _dot(act_ref[:, rows], wd_ref[rows, :])
    y_ref[...] = x + acc
    if seg is None:
        last = hbuf_ref[top + tm - 2:top + tm, :]
        ns_ref[...] = last
        hbuf_ref[prev, :] = last


def _conv_ffn(x, state, layer, g, w_up, conv_w, conv_b, w_down, *, seg=None):
    B, T, D = x.shape
    F2 = w_up.shape[-1]
    tm = _tiles(T)["tm"]
    fc = 256
    assert T % tm == 0 and (F2 // 2) % fc == 0
    if seg is None:
        st_spec = pl.BlockSpec((None, 2, F2), lambda b, t: (b, 0, 0))
        ns_shape = jax.ShapeDtypeStruct((B, 2, F2), F32)
        ns_spec = pl.BlockSpec((None, 2, F2), lambda b, t: (b, 0, 0))
    else:
        assert B == 1 and tm == T and T % seg == 0
        st_spec = pl.BlockSpec((2, T, F2), lambda b, t: (0, 0, 0))
        ns_shape = jax.ShapeDtypeStruct((T // seg, 2, F2), F32)
        ns_spec = pl.BlockSpec((T // seg, 2, F2), lambda b, t: (0, 0, 0))
    return pl.pallas_call(
        functools.partial(_ffn_kernel, seg=seg, fc=fc, down_group=6),
        grid=(B, T // tm),
        in_specs=[
            pl.BlockSpec((None, tm, D), lambda b, t: (b, t, 0)),
            st_spec,
            _const_spec((None, 1, D), (layer, 0, 0)),
            _const_spec((None, D, F2), (layer, 0, 0)),
            _const_spec((None, 3, F2), (layer, 0, 0)),
            _const_spec((None, 1, F2), (layer, 0, 0)),
            _const_spec((None, F2 // 2, D), (layer, 0, 0)),
        ],
        out_specs=[pl.BlockSpec((None, tm, D), lambda b, t: (b, t, 0)), ns_spec],
        out_shape=[jax.ShapeDtypeStruct((B, T, D), F32), ns_shape],
        scratch_shapes=[pltpu.VMEM((SUBLANES + tm, F2), F32), pltpu.VMEM((tm, F2 // 2), BF)],
        compiler_params=_params(("arbitrary", "arbitrary")),
        name="conv_ffn",
    )(x, state, g, w_up, conv_w, conv_b, w_down)


def _cmlp_kernel(x_ref, g_ref, win_ref, vg_ref, ws_ref, bs_ref, wout_ref, y_ref, *rest, seq, chunk):
    *v_out, z_ref = rest
    out_group = A_GROUPS // 2
    x = x_ref[...]
    tm = x.shape[0]
    d_a = wout_ref.shape[0]
    gd = d_a // A_GROUPS
    xn = _rms(x, g_ref[...]).astype(BF)
    r = _iota((chunk, chunk), 0)
    c = _iota((chunk, chunk), 1)
    keep = (c <= r) if seq == chunk else ((r // seq == c // seq) & (c <= r))
    acc = jnp.zeros(y_ref.shape, F32)
    proj = lambda g: [_dot(xn, win_ref[:, off + g * gd:off + (g + 1) * gd]) for off in (0, d_a)]
    uv_next = proj(0)
    for g in range(A_GROUPS):
        (u, v), uv_next = uv_next, (proj(g + 1) if g + 1 < A_GROUPS else None)
        cols = slice(g * gd, (g + 1) * gd)
        u = jax.nn.gelu(u)
        v = jax.nn.gelu(v)
        v = v * lax.rsqrt(jnp.mean(v * v, axis=-1, keepdims=True) + NORM_EPS) * vg_ref[:, cols]
        if v_out:
            v_out[0][:, cols] = v
        wm = jnp.where(keep, ws_ref[g], 0.0).astype(BF)
        vb = v.astype(BF)
        parts = [_dot(wm, vb[j * chunk:(j + 1) * chunk, :]) + bs_ref[g] for j in range(tm // chunk)]
        mixed = parts[0] if len(parts) == 1 else jnp.concatenate(parts, axis=0)
        z_ref[:, cols] = (u * mixed).astype(BF)
        if (g + 1) % out_group == 0:
            rows = slice((g + 1 - out_group) * gd, (g + 1) * gd)
            acc = acc + _dot(z_ref[:, rows], wout_ref[rows, :])
    y_ref[...] = x + acc


def _chunk_mlp(x, j, g, w_in, v_g, w_s, b_s, w_out, *, seq, emit_v):
    B, T, D = x.shape
    chunk = w_s.shape[-1]
    d_a = w_out.shape[1]
    tm = _tiles(T)["tm"]
    assert T % tm == 0 and tm % chunk == 0
    tok = pl.BlockSpec((None, tm, D), lambda b, t: (b, t, 0))
    out_specs = [tok]
    out_shape = [jax.ShapeDtypeStruct((B, T, D), F32)]
    if emit_v:
        out_specs.append(pl.BlockSpec((None, tm, d_a), lambda b, t: (b, t, 0)))
        out_shape.append(jax.ShapeDtypeStruct((B, T, d_a), F32))
    return pl.pallas_call(
        functools.partial(_cmlp_kernel, seq=seq, chunk=chunk),
        grid=(B, T // tm),
        in_specs=[
            tok,
            _const_spec((None, 1, D), (g[1], 0, 0)),
            _const_spec((None, D, 2 * d_a), (j, 0, 0)),
            _const_spec((None, 1, d_a), (j, 0, 0)),
            _const_spec((A_GROUPS, chunk, chunk)),
            _const_spec((A_GROUPS, chunk, 1)),
            _const_spec((None, d_a, D), (j, 0, 0)),
        ],
        out_specs=out_specs,
        out_shape=out_shape,
        scratch_shapes=[pltpu.VMEM((tm, d_a), BF)],
        compiler_params=_params(("arbitrary", "arbitrary")),
        name="chunk_mlp",
    )(x, g[0], w_in, v_g, w_s, b_s, w_out)


def _aug_placement(n_heads):
    assert N_PIECES * n_heads < LANES
    w = np.zeros((LANES, 2, n_heads, LANES), np.float32)
    one = N_PIECES * n_heads
    for h in range(n_heads):
        for p in range(N_PIECES):
            w[p * n_heads + h, 0, h, HEAD_DIM + p] = 1.0
            w[one, 0, h, HEAD_DIM + N_PIECES + p] = 1.0
            w[one, 1, h, HEAD_DIM + p] = 1.0
            w[p * n_heads + h, 1, h, HEAD_DIM + N_PIECES + p] = -1.0
    return jnp.asarray(w.reshape(LANES, 2 * n_heads * LANES), BF)


def _fox_proj_kernel(x_ref, g_ref, w_ref, qg_ref, kg_ref, bf_ref, bd_ref, *rest, seg):
    x = x_ref[...]
    tm, D = x.shape
    H = D // HEAD_DIM
    xn = _rms(x, g_ref[...]).astype(BF)
    bd = bd_ref[...]

    def head_norm(t, gain):
        ms = _dot((t * t).astype(BF), bd) * (1.0 / HEAD_DIM)
        return t * lax.rsqrt(ms + NORM_EPS) * gain

    q = head_norm(_dot(xn, w_ref[:, 0:D]), qg_ref[...])
    k = head_norm(_dot(xn, w_ref[:, D:2 * D]), kg_ref[...])
    v = _dot(xn, w_ref[:, 2 * D:3 * D])
    og = jax.nn.sigmoid(_dot(xn, w_ref[:, 3 * D:4 * D])).astype(BF)
    lf = jax.nn.log_sigmoid(_dot(xn, w_ref[:, 4 * D:4 * D + LANES]) + bf_ref[...])
    if seg is not None:
        q_ref, kf_ref, vf_ref, og_ref, lf_ref = rest
        q_ref[...] = (q * (HEAD_DIM ** -0.5)).astype(BF)
    else:
        place_ref, qa_ref, ka_ref, va_ref, kf_ref, vf_ref, og_ref, lf_ref, carry_ref = rest
    kf_ref[...] = k
    vf_ref[...] = v
    og_ref[...] = og
    lf_ref[...] = lf[:, :H]
    if seg is not None:
        return

    @pl.when(pl.program_id(1) == 0)
    def _():
        carry_ref[...] = jnp.zeros_like(carry_ref)

    tri = (_iota((tm, tm), 1) <= _iota((tm, tm), 0)).astype(BF)
    cs = _dot_exact_lhs01(tri, lf) + carry_ref[...]
    carry_ref[...] = cs[tm - 1:tm, :]
    hi, mid, lo = (p.astype(F32) for p in _split3(cs * LOG2E))
    lane = _iota((tm, LANES), 1)
    pieces = jnp.where(lane < H, hi,
                       jnp.where(lane < 2 * H, pltpu.roll(mid, H, 1),
                                 jnp.where(lane < 3 * H, pltpu.roll(lo, 2 * H, 1),
                                           jnp.where(lane == 3 * H, 1.0, 0.0))))
    extras = _dot(pieces.astype(BF), place_ref[...])
    qs = q * (HEAD_DIM ** -0.5 * LOG2E)
    first = lane < HEAD_DIM
    one_lane = jnp.where(lane == HEAD_DIM, 1.0, 0.0)
    for h in range(H):
        pair = slice((h // PAIR) * LANES, (h // PAIR + 1) * LANES)

        def head_tile(t):
            tile = t[:, pair]
            return pltpu.roll(tile, HEAD_DIM, 1) if h % PAIR else tile

        qa_ref[h] = jnp.where(first, head_tile(qs), extras[:, h * LANES:(h + 1) * LANES]).astype(BF)
        ka_ref[h] = jnp.where(first, head_tile(k), extras[:, (H + h) * LANES:(H + h + 1) * LANES]).astype(BF)
        va_ref[h] = jnp.where(first, head_tile(v), one_lane).astype(BF)


def _fox_proj(x, g, w_all, q_g, k_g, b_f, bd, *, seg=None):
    B, T, D = x.shape
    H = D // HEAD_DIM
    tm = _tiles(T)["tm"]
    assert T % tm == 0 and (seg is None or (B == 1 and tm == T))
    tok = lambda n: pl.BlockSpec((None, tm, n), lambda b, t: (b, t, 0))
    consts = [w_all, q_g, k_g, b_f, bd]
    common_specs = [tok(D), tok(D), tok(D), tok(H)]
    common_shape = [jax.ShapeDtypeStruct((B, T, D), F32), jax.ShapeDtypeStruct((B, T, D), F32),
                    jax.ShapeDtypeStruct((B, T, D), BF), jax.ShapeDtypeStruct((B, T, H), F32)]
    if seg is None:
        consts.append(_aug_placement(H))
        heads = pl.BlockSpec((None, H, tm, LANES), lambda b, t: (b, 0, t, 0))
        out_specs = [heads] * 3 + common_specs
        out_shape = [jax.ShapeDtypeStruct((B, H, T, LANES), BF)] * 3 + common_shape
        scratch = [pltpu.VMEM((1, LANES), F32)]
    else:
        out_specs = [tok(D)] + common_specs
        out_shape = [jax.ShapeDtypeStruct((B, T, D), BF)] + common_shape
        scratch = []
    return pl.pallas_call(
        functools.partial(_fox_proj_kernel, seg=seg),
        grid=(B, T // tm),
        in_specs=[tok(D), _const_spec((None, 1, D), (g[1], 0, 0))] + [_const_spec(c.shape) for c in consts],
        out_specs=out_specs,
        out_shape=out_shape,
        scratch_shapes=scratch,
        compiler_params=_params(("arbitrary", "arbitrary")),
        name="fox_proj",
    )(x, g[0], *consts)


def _fox_attn_kernel(qi_ref, ki_ref, q_ref, k_ref, v_ref, og_ref, o_ref, m_ref, acc_ref, *, tq):
    step = pl.program_id(2)
    qi = qi_ref[step]
    ki = ki_ref[step]

    @pl.when(ki == 0)
    def _():
        m_ref[...] = jnp.full_like(m_ref, NEG_BIG)
        acc_ref[...] = jnp.zeros_like(acc_ref)

    def update(masked):
        s = [_dot_nt(q_ref[h], k_ref[h]) for h in range(PAIR)]
        if masked:
            visible = _iota((tq, tq), 0) >= _iota((tq, tq), 1)
            s = [jnp.where(visible, x, NEG_BIG) for x in s]
        m_prev = [m_ref[h] for h in range(PAIR)]
        m_new = [jnp.maximum(mp, jnp.max(x, axis=-1, keepdims=True)) for mp, x in zip(m_prev, s)]
        p = [jnp.exp2(x - pltpu.repeat(mn, tq // LANES, 1)).astype(BF) for x, mn in zip(s, m_new)]
        for h in range(PAIR):
            acc_ref[h] = jnp.exp2(m_prev[h] - m_new[h]) * acc_ref[h] + _dot(p[h], v_ref[h])
            m_ref[h] = m_new[h]

    @pl.when(ki < qi)
    def _():
        update(False)

    @pl.when(ki == qi)
    def _():
        update(True)
        o = [acc_ref[h] for h in range(PAIR)]
        o = [x / x[:, HEAD_DIM:HEAD_DIM + 1] for x in o]
        assert PAIR == 2
        pair = jnp.where(_iota((tq, LANES), 1) < HEAD_DIM, o[0], pltpu.roll(o[1], HEAD_DIM, 1))
        o_ref[...] = (pair * og_ref[...].astype(F32)).astype(BF)


def _fox_attn(qa, ka, va, og):
    B, H, T, _ = qa.shape
    D = og.shape[-1]
    tq = _tiles(T)["tq"]
    assert T % tq == 0 and tq % LANES == 0
    nq = T // tq
    qi_tab = np.asarray([q for q in range(nq) for _ in range(q + 1)], np.int32)
    ki_tab = np.asarray([k for q in range(nq) for k in range(q + 1)], np.int32)
    qspec = pl.BlockSpec((None, PAIR, tq, LANES), lambda b, p, s, qt, kt: (b, p, qt[s], 0))
    kspec = pl.BlockSpec((None, PAIR, tq, LANES), lambda b, p, s, qt, kt: (b, p, kt[s], 0))
    ospec = pl.BlockSpec((None, tq, LANES), lambda b, p, s, qt, kt: (b, qt[s], p))
    return pl.pallas_call(
        functools.partial(_fox_attn_kernel, tq=tq),
        grid_spec=pltpu.PrefetchScalarGridSpec(
            num_scalar_prefetch=2,
            grid=(B, H // PAIR, len(qi_tab)),
            in_specs=[qspec, kspec, kspec, ospec],
            out_specs=ospec,
            scratch_shapes=[pltpu.VMEM((PAIR, tq, LANES), F32), pltpu.VMEM((PAIR, tq, LANES), F32)],
        ),
        out_shape=jax.ShapeDtypeStruct((B, T, D), BF),
        compiler_params=_params(("arbitrary",) * 3),
        name="fox_attn",
    )(jnp.asarray(qi_tab), jnp.asarray(ki_tab), qa, ka, va, og)


def _fox_decode_kernel(pt_ref, q_ref, kn_ref, vn_ref, lfn_ref, *rest, n_new, group):
    kp_refs, vp_refs, lfp_refs = rest[:group], rest[group:2 * group], rest[2 * group:3 * group]
    og_ref, o_ref, qbd_ref, m_ref, l_ref, acc_ref, carry_ref = rest[3 * group:]
    j = pl.program_id(1)
    D = q_ref.shape[-1]
    n_heads = D // HEAD_DIM
    rows = n_heads * n_new
    page = lfp_refs[0].shape[0]
    wide = n_heads * LANES
    assert rows == LANES and page == LANES
    rr = _iota((rows, LANES), 0)
    ll = _iota((rows, LANES), 1)
    expand = (_iota((rows, n_heads), 0) // n_new == _iota((rows, n_heads), 1)).astype(BF)

    def widen(x):
        zeros = jnp.zeros((x.shape[0], LANES - HEAD_DIM), x.dtype)
        return jnp.concatenate([t for h in range(n_heads) for t in (x[:, h * HEAD_DIM:(h + 1) * HEAD_DIM], zeros)],
                               axis=1)

    def load_wide(ref):
        zeros = jnp.zeros((page, LANES - HEAD_DIM), F32)
        tiles = [t for h in range(n_heads) for t in (ref[pl.ds(h, page, stride=n_heads), :], zeros)]
        return jnp.concatenate(tiles, axis=1).astype(BF)

    def spread(lf):
        return [_dot_nt(expand, piece).astype(BF) for piece in _split3(lf)]

    def online_update(s, v):
        m_prev = m_ref[...]
        m_new = jnp.maximum(m_prev, jnp.max(s, axis=-1, keepdims=True))
        alpha = jnp.exp(m_prev - m_new)
        p = jnp.exp(s - m_new)
        l_ref[...] = alpha * l_ref[...] + jnp.sum(p, axis=-1, keepdims=True)
        acc_ref[...] = alpha * acc_ref[...] + _dot(p.astype(BF), v)
        m_ref[...] = m_new

    @pl.when(j == 0)
    def _():
        q_rows = jnp.concatenate([widen(q_ref[...].astype(F32))] * n_heads, axis=0)
        own = _iota((rows, wide), 1) // LANES == _iota((rows, wide), 0) // n_new
        qbd_ref[...] = jnp.where(own, q_rows, 0.0).astype(BF)
        m_ref[...] = jnp.full_like(m_ref, NEG_BIG)
        l_ref[...] = jnp.zeros_like(l_ref)
        acc_ref[...] = jnp.zeros_like(acc_ref)
        pad = page - n_new
        kn = jnp.concatenate([widen(kn_ref[...]), jnp.zeros((pad, wide), F32)], axis=0).astype(BF)
        vn = jnp.concatenate([widen(vn_ref[...]), jnp.zeros((pad, wide), F32)], axis=0).astype(BF)
        lfn = jnp.concatenate([lfn_ref[...], jnp.zeros((pad, n_heads), F32)], axis=0)
        incl = (_iota((page, page), 0) <= _iota((page, page), 1)).astype(BF)
        cn = sum(_dot(piece, incl) for piece in spread(lfn))
        t_of_row = rr % n_new
        cn_q = jnp.sum(jnp.where(ll == t_of_row, cn, 0.0), axis=-1, keepdims=True)
        s = _dot_nt(qbd_ref[...], kn) + (cn_q - cn)
        s = jnp.where(ll <= t_of_row, s, NEG_BIG)
        online_update(s, vn)
        carry_ref[...] = cn_q

    @pl.when(j > 0)
    def _():
        kp = jnp.concatenate([load_wide(r) for r in kp_refs], axis=0)
        vp = jnp.concatenate([load_wide(r) for r in vp_refs], axis=0)
        pieces = spread(jnp.concatenate([r[...] for r in lfp_refs], axis=0))
        n_keys = group * page
        later = (_iota((n_keys, n_keys), 0) > _iota((n_keys, n_keys), 1)).astype(BF)
        suffix = sum(_dot(piece, later) for piece in pieces)
        total = sum(jnp.sum(piece.astype(F32), axis=-1, keepdims=True) for piece in pieces)
        s = _dot_nt(qbd_ref[...], kp) + (suffix + carry_ref[...])
        online_update(s, vp)
        carry_ref[...] = carry_ref[...] + total

    @pl.when(j == pl.num_programs(1) - 1)
    def _():
        o = acc_ref[...] / l_ref[...]
        own_tile = lambda h: o[h * n_new:(h + 1) * n_new, h * LANES:(h + 1) * LANES]
        assert PAIR == 2
        first = _iota((n_new, LANES), 1) < HEAD_DIM
        tiles = [jnp.where(first, own_tile(h), pltpu.roll(own_tile(h + 1), HEAD_DIM, 1))
                 for h in range(0, n_heads, PAIR)]
        o_ref[...] = (jnp.concatenate(tiles, axis=1) * og_ref[...].astype(F32)).astype(BF)


def _fox_decode(q, k_new, v_new, lf_new, og, cache_k, cache_v, cache_lf, page_table):
    DB, n_new, D = q.shape
    H = D // HEAD_DIM
    n_pool, page = cache_k.shape[:2]
    n_pages = page_table.shape[1]
    ck = cache_k.reshape(n_pool, page * H, HEAD_DIM)
    cv = cache_v.reshape(n_pool, page * H, HEAD_DIM)
    group = DECODE_PAGES
    assert n_pages % group == 0
    new = lambda n: pl.BlockSpec((None, n_new, n), lambda b, j, pt: (b, 0, 0))

    def paged(*shape):
        return [pl.BlockSpec((None,) + shape,
                             lambda b, j, pt, g=g: (pt[b, n_pages - group * jnp.maximum(j, 1) + g], 0, 0))
                for g in range(group)]

    rows = H * n_new
    kv_page = (page * H, HEAD_DIM)
    return pl.pallas_call(
        functools.partial(_fox_decode_kernel, n_new=n_new, group=group),
        grid_spec=pltpu.PrefetchScalarGridSpec(
            num_scalar_prefetch=1,
            grid=(DB, n_pages // group + 1),
            in_specs=[new(D), new(D), new(D), new(H), *paged(*kv_page), *paged(*kv_page), *paged(page, H), new(D)],
            out_specs=new(D),
            scratch_shapes=[pltpu.VMEM((rows, H * LANES), BF), pltpu.VMEM((rows, 1), F32), pltpu.VMEM((rows, 1), F32),
                            pltpu.VMEM((rows, H * LANES), F32), pltpu.VMEM((rows, 1), F32)],
        ),
        out_shape=jax.ShapeDtypeStruct((DB, n_new, D), BF),
        compiler_params=_params(("arbitrary", "arbitrary")),
        name="fox_decode",
    )(page_table, q, k_new, v_new, lf_new, *[ck] * group, *[cv] * group, *[cache_lf] * group, og)


def _resid_mm_kernel(x_ref, a_ref, w_ref, y_ref):
    y_ref[...] = x_ref[...] + _dot(a_ref[...], w_ref[...])


def _resid_mm(x, a, w):
    B, T, D = x.shape
    K = a.shape[-1]
    tm = _tiles(T)["tm"]
    return pl.pallas_call(
        _resid_mm_kernel,
        grid=(B, T // tm),
        in_specs=[pl.BlockSpec((None, tm, D), lambda b, t: (b, t, 0)),
                  pl.BlockSpec((None, tm, K), lambda b, t: (b, t, 0)),
                  _const_spec((K, D))],
        out_specs=pl.BlockSpec((None, tm, D), lambda b, t: (b, t, 0)),
        out_shape=jax.ShapeDtypeStruct((B, T, D), F32),
        compiler_params=_params(("arbitrary", "arbitrary")),
        name="resid_mm",
    )(x, a, w)


def _rwkv_proj_kernel(x_ref, sh_ref, g_ref, mu_ref, wr_ref, wk_ref, wv_ref, w0_ref, w1_ref, w2_ref,
                      a0_ref, a1_ref, a2_ref, g1_ref, g2_ref, kk_ref, ka_ref, rk_ref, bd_ref,
                      r_o, lw_o, k_o, v_o, kk_o, a_o, g_o, bonus_o, shift_o, *scratch, seg):
    x = x_ref[...]
    tm, D = x.shape
    xn = _rms(x, g_ref[...])
    row = _iota((tm, D), 0)
    if seg is None:
        carry_ref, = scratch

        @pl.when(pl.program_id(1) == 0)
        def _():
            carry_ref[...] = sh_ref[...]

        prev = jnp.where(row == 0, carry_ref[...], pltpu.roll(xn, 1, 0))
        carry_ref[...] = xn[tm - 1:tm, :]
        shift_o[...] = xn[tm - 1:tm, :]
    else:
        prev = jnp.where(row % seg == 0, sh_ref[...], pltpu.roll(xn, 1, 0))
        shift_o[...] = xn.reshape(tm // seg, seg, D)[:, seg - 1, :]
    xx = prev - xn
    xr, xw, xk, xv, xa, xg = ((xn + xx * mu_ref[i:i + 1, :]).astype(BF) for i in range(6))
    r = _dot(xr, wr_ref[...])
    k = _dot(xk, wk_ref[...])
    v = _dot(xv, wv_ref[...])
    lora_w = _dot(jnp.tanh(_dot(xw, w1_ref[...])).astype(BF), w2_ref[...])
    w_log = -jax.nn.softplus(-(w0_ref[...] + lora_w)) - 0.5
    lw_o[...] = -jnp.exp(w_log)
    a = jax.nn.sigmoid(a0_ref[...] + _dot(_dot(xa, a1_ref[...]).astype(BF), a2_ref[...]))
    g_o[...] = _dot(jax.nn.sigmoid(_dot(xg, g1_ref[...])).astype(BF), g2_ref[...]).astype(BF)
    bd = bd_ref[...]
    kk = k * kk_ref[...]
    kk = kk / jnp.maximum(jnp.sqrt(_dot((kk * kk).astype(BF), bd)), 1e-12)
    k = k * (1.0 + (a - 1.0) * ka_ref[...])
    r_o[...] = r
    k_o[...] = k
    v_o[...] = v.astype(BF)
    kk_o[...] = kk
    a_o[...] = a
    bonus_o[...] = _dot((r * k * rk_ref[...]).astype(BF), bd) * v


def _rwkv_proj(x, shift, g, w, bd, *, seg=None):
    B, T, D = x.shape
    tm = _tiles(T)["tm"]
    assert T % tm == 0
    tok = pl.BlockSpec((None, tm, D), lambda b, t: (b, t, 0))
    if seg is None:
        sh_spec = pl.BlockSpec((None, 1, D), lambda b, t: (b, 0, 0))
        shift_shape = jax.ShapeDtypeStruct((B, 1, D), F32)
        shift_spec = pl.BlockSpec((None, 1, D), lambda b, t: (b, 0, 0))
        scratch = [pltpu.VMEM((1, D), F32)]
    else:
        assert B == 1 and tm == T
        sh_spec = pl.BlockSpec((None, T, D), lambda b, t: (0, 0, 0))
        shift_shape = jax.ShapeDtypeStruct((T // seg, D), F32)
        shift_spec = pl.BlockSpec((T // seg, D), lambda b, t: (0, 0))
        scratch = []
    vec = _const_spec((1, D))
    mats = [w[n] for n in ("w_r", "w_k", "w_v")]
    consts = [w["mu"], *mats, w["w0"], w["w1"], w["w2"], w["a0"], w["a1"], w["a2"], w["g1"], w["g2"],
              w["k_k"], w["k_a"], w["r_k"], bd]
    return pl.pallas_call(
        functools.partial(_rwkv_proj_kernel, seg=seg),
        grid=(B, T // tm),
        in_specs=[tok, sh_spec, _const_spec((None, 1, D), (g[1], 0, 0))] + [_const_spec(c.shape) for c in consts],
        out_specs=[tok] * 8 + [shift_spec],
        out_shape=[jax.ShapeDtypeStruct((B, T, D), dt) for dt in (F32, F32, F32, BF, F32, F32, BF, F32)]
        + [shift_shape],
        scratch_shapes=scratch,
        compiler_params=_params(("arbitrary", "arbitrary")),
        name="rwkv_proj",
    )(x, shift, g[0], *consts)


def _rwkv_scan_kernel(r_ref, lw_ref, k_ref, v_ref, kk_ref, a_ref, s0_ref, y_ref, sT_ref, S_ref, *, C):
    D = r_ref.shape[-1]

    @pl.when(pl.program_id(1) == 0)
    def _():
        S_ref[...] = s0_ref[...]

    lw = lw_ref[...]
    tri = (_iota((C, C), 1) <= _iota((C, C), 0)).astype(BF)
    cum = _dot_exact_lhs01(tri, lw)
    cend = cum[C - 1:C, :]
    kk = kk_ref[...]
    kka = kk * a_ref[...]
    k = k_ref[...]
    inv_p = jnp.exp(-cum)
    to_end = jnp.exp(cend - cum)
    at = (-(kk * jnp.exp(cum - lw))).astype(BF)
    bt = (kka * inv_p).astype(BF)
    kt = (k * inv_p).astype(BF)
    rt = (r_ref[...] * jnp.exp(cum)).astype(BF)
    kh = (k * to_end).astype(BF)
    bh = (kka * to_end).astype(BF)
    vb = v_ref[...]
    p_end = jnp.exp(cend)

    first_half = _iota((C, LANES), 1) < HEAD_DIM
    n2 = PAIR * C
    ri = _iota((n2, n2), 0)
    ci = _iota((n2, n2), 1)
    same = ri // C == ci // C
    strict = same & (ci < ri)
    incl = same & (ci <= ri)
    eye = (ri == ci).astype(F32)
    same_head = _iota((LANES, LANES), 0) // HEAD_DIM == _iota((LANES, LANES), 1) // HEAD_DIM

    def stack(t):
        z = jnp.zeros_like(t)
        return jnp.concatenate([jnp.where(first_half, t, z), jnp.where(first_half, z, t)], axis=0)

    pairs = range(D // LANES)
    lanes = [slice(pr * LANES, (pr + 1) * LANES) for pr in pairs]
    at_s, rt_s, kt_s, bt_s, v_s = ([stack(t[:, ln]) for ln in lanes] for t in (at, rt, kt, bt, vb))
    aa = [_dot_nt(jnp.concatenate([at_s[pr], rt_s[pr]], axis=0), jnp.concatenate([kt_s[pr], bt_s[pr]], axis=0))
          for pr in pairs]
    a_ak = [jnp.where(strict, x[:n2, :n2], 0.0).astype(BF) for x in aa]
    a_ab = [jnp.where(strict, x[:n2, n2:], 0.0) for x in aa]
    a_rk = [jnp.where(incl, x[n2:, :n2], 0.0).astype(BF) for x in aa]
    a_rb = [jnp.where(incl, x[n2:, n2:], 0.0).astype(BF) for x in aa]
    inv = [eye + x for x in a_ab]
    npow = a_ab
    for _ in range(int(math.log2(C)) - 1):
        nb = [x.astype(BF) for x in npow]
        npow = [_dot(x, x) for x in nb]
        inv = [i + _dot(i.astype(BF), n.astype(BF)) for i, n in zip(inv, npow)]
    S = [S_ref[pr] for pr in pairs]
    Sb = [x.astype(BF) for x in S]
    rhs = [(_dot_nt(at_s[pr], Sb[pr]) + _dot(a_ak[pr], v_s[pr])).astype(BF) for pr in pairs]
    u = [_dot(inv[pr].astype(BF), rhs[pr]) for pr in pairs]
    y = [_dot_nt(rt_s[pr], Sb[pr]) + _dot(a_rk[pr], v_s[pr]) + _dot(a_rb[pr], u[pr].astype(BF)) for pr in pairs]
    for pr in pairs:
        y_ref[:, lanes[pr]] = y[pr][:C] + y[pr][C:]
        u_pair = (u[pr][:C] + u[pr][C:]).astype(BF)
        dS = _dot_tn(vb[:, lanes[pr]], kh[:, lanes[pr]]) + _dot_tn(u_pair, bh[:, lanes[pr]])
        S_ref[pr] = S[pr] * p_end[:, lanes[pr]] + jnp.where(same_head, dS, 0.0)

    @pl.when(pl.program_id(1) == pl.num_programs(1) - 1)
    def _():
        sT_ref[...] = S_ref[...]


def _rwkv_scan(r, lw, k, v, kk, a, s0_bd):
    B, T, D = r.shape
    C = RWKV_CHUNK
    assert T % C == 0
    n_pairs = D // LANES
    tok = pl.BlockSpec((None, C, D), lambda b, t: (b, t, 0))
    st = pl.BlockSpec((None, n_pairs, LANES, LANES), lambda b, t: (b, 0, 0, 0))
    return pl.pallas_call(
        functools.partial(_rwkv_scan_kernel, C=C),
        grid=(B, T // C),
        in_specs=[tok] * 6 + [st],
        out_specs=[tok, st],
        out_shape=[jax.ShapeDtypeStruct((B, T, D), F32), jax.ShapeDtypeStruct(s0_bd.shape, F32)],
        scratch_shapes=[pltpu.VMEM((n_pairs, LANES, LANES), F32)],
        compiler_params=_params(("arbitrary", "arbitrary")),
        name="rwkv_scan",
    )(r, lw, k, v, kk, a, s0_bd)


def _state_to_bd(s):
    B, H = s.shape[:2]
    sp = s.reshape(B, H // PAIR, PAIR, HEAD_DIM, HEAD_DIM)
    eye = jnp.eye(PAIR, dtype=s.dtype)
    bd = sp[:, :, :, :, None, :] * eye[None, None, :, None, :, None]
    return bd.reshape(B, H // PAIR, LANES, LANES)


def _bd_to_state(bd):
    B, P = bd.shape[:2]
    t = bd.reshape(B, P, PAIR, HEAD_DIM, PAIR, HEAD_DIM)
    return jnp.stack([t[:, :, h, :, h, :] for h in range(PAIR)], axis=2).reshape(B, P * PAIR, HEAD_DIM, HEAD_DIM)


def _rwkv_post_kernel(x_ref, y_ref, bonus_ref, g_ref, lng_ref, lnb_ref, bd_ref, wo_ref, o_ref):
    y = y_ref[...]
    bd = bd_ref[...]
    d = y - _dot(y.astype(BF), bd) * (1.0 / HEAD_DIM)
    var = _dot((d * d).astype(BF), bd) * (1.0 / HEAD_DIM)
    yn = d * lax.rsqrt(var + LNX_EPS) * lng_ref[...] + lnb_ref[...] + bonus_ref[...]
    o_ref[...] = x_ref[...] + _dot((yn * g_ref[...].astype(F32)).astype(BF), wo_ref[...])


def _rwkv_post(x, y, bonus, g, ln_g, ln_b, bd, w_o):
    B, T, D = x.shape
    tm = _tiles(T)["tm"]
    tok = pl.BlockSpec((None, tm, D), lambda b, t: (b, t, 0))
    return pl.pallas_call(
        _rwkv_post_kernel,
        grid=(B, T // tm),
        in_specs=[tok, tok, tok, tok, _const_spec((1, D)), _const_spec((1, D)), _const_spec((D, D)),
                  _const_spec((D, D))],
        out_specs=tok,
        out_shape=jax.ShapeDtypeStruct((B, T, D), F32),
        compiler_params=_params(("arbitrary", "arbitrary")),
        name="rwkv_post",
    )(x, y, bonus, g, ln_g, ln_b, bd, w_o)


def _pad_cols(w, n):
    return jnp.pad(w, ((0, 0), (0, n - w.shape[1])))


def _pad_rows(w, n):
    return jnp.pad(w, ((0, n - w.shape[0]), (0, 0)))


def kernel(x_prompt, x_sample, cache_k, cache_v, cache_logf, page_table, state_wkv, state_shift, state_conv,
           mix_norm, a_w_in, a_v_norm, a_w_s, a_b_s, a_w_out,
           f_w_qkv, f_q_norm, f_k_norm, f_w_fgate, f_b_fgate, f_w_ogate, f_w_out,
           r_mu, r_w_r, r_w_k, r_w_v, r_w0, r_w1, r_w2, r_a0, r_a1, r_a2, r_g1, r_g2,
           r_k_k, r_k_a, r_r_k, r_lnx_g, r_lnx_b, r_w_o,
           ffn_norm, ffn_w_up, ffn_conv_w, ffn_conv_b, ffn_w_down):
    B, T, D = x_prompt.shape
    DB, n_new, _ = x_sample.shape
    H = D // HEAD_DIM
    depth = ffn_w_up.shape[0]
    F2 = ffn_w_up.shape[-1]
    M = DB * n_new
    row = lambda v: v.reshape(1, -1).astype(F32)

    mix_g = mix_norm.reshape(depth, 1, D)
    ffn_g = ffn_norm.reshape(depth, 1, D)
    ffn_up, ffn_down = ffn_w_up.astype(BF), ffn_w_down.astype(BF)
    ffn_cb = ffn_conv_b.reshape(depth, 1, F2)
    a_in, a_out = a_w_in.astype(BF), a_w_out.astype(BF)
    a_vg = a_v_norm.reshape(a_v_norm.shape[0], 1, -1)
    head_ones = (jnp.arange(D)[:, None] // HEAD_DIM == jnp.arange(D)[None, :] // HEAD_DIM).astype(BF)
    f_all = jnp.concatenate([f_w_qkv, f_w_ogate, _pad_cols(f_w_fgate, LANES)], axis=1).astype(BF)
    f_qg, f_kg = row(jnp.tile(f_q_norm, H)), row(jnp.tile(f_k_norm, H))
    f_bf = _pad_cols(row(f_b_fgate), LANES)
    f_out = f_w_out.astype(BF)
    lora = lambda w1, w2, n: (_pad_cols(w1, n).astype(BF), _pad_rows(w2, n).astype(BF))
    rw = dict(mu=r_mu, w_r=r_w_r.astype(BF), w_k=r_w_k.astype(BF), w_v=r_w_v.astype(BF),
              w0=row(r_w0), a0=row(r_a0), k_k=row(r_k_k), k_a=row(r_k_a), r_k=row(r_r_k))
    rw["w1"], rw["w2"] = lora(r_w1, r_w2, LANES)
    rw["a1"], rw["a2"] = lora(r_a1, r_a2, LANES)
    rw["g1"], rw["g2"] = lora(r_g1, r_g2, 2 * LANES)
    r_out = r_w_o.astype(BF)

    xp = x_prompt
    xs = x_sample.reshape(1, M, D)
    conv_p, conv_s, chunk_v_s = [], [], []
    outs = {}
    for i in range(depth):
        kind = i % 3
        g = (mix_g, i)
        if kind == 0:
            j = i // 3
            ws_p = a_w_s[j]
            bs_p = a_b_s[j][:, :, None]
            xp, = _chunk_mlp(xp, j, g, a_in, a_vg, ws_p, bs_p, a_out, seq=CHUNK, emit_v=False)
            reps = M // n_new
            ws_s = jnp.tile(a_w_s[j][:, :n_new, :n_new], (1, reps, reps))
            bs_s = jnp.tile(a_b_s[j][:, :n_new], (1, reps))[:, :, None]
            xs, v_rows = _chunk_mlp(xs, j, g, a_in, a_vg, ws_s, bs_s, a_out, seq=n_new, emit_v=True)
            chunk_v_s.append(v_rows.reshape(DB, n_new, -1))
        elif kind == 1:
            qa, ka, va, kf, vf, og, lf = _fox_proj(xp, g, f_all, f_qg, f_kg, f_bf, head_ones)
            outs["k_p"], outs["v_p"], outs["logf_p"] = (kf.reshape(B, T, H, HEAD_DIM), vf.reshape(B, T, H, HEAD_DIM), lf)
            xp = _resid_mm(xp, _fox_attn(qa, ka, va, og), f_out)
            q, kf, vf, og, lf = _fox_proj(xs, g, f_all, f_qg, f_kg, f_bf, head_ones, seg=n_new)
            shp = (DB, n_new, D)
            outs["k_s"], outs["v_s"] = kf.reshape(DB, n_new, H, HEAD_DIM), vf.reshape(DB, n_new, H, HEAD_DIM)
            outs["logf_s"] = lf.reshape(DB, n_new, H)
            att = _fox_decode(q.reshape(shp), kf.reshape(shp), vf.reshape(shp), outs["logf_s"], og.reshape(shp),
                              cache_k, cache_v, cache_logf, page_table)
            xs = _resid_mm(xs, att.reshape(1, M, D), f_out)
        else:
            r, lw, k, v, kk, a, gate, bonus, shift = _rwkv_proj(xp, jnp.zeros((B, 1, D), F32), g, rw, head_ones)
            y, s_bd = _rwkv_scan(r, lw, k, v, kk, a, jnp.zeros((B, D // LANES, LANES, LANES), F32))
            outs["wkv_p"], outs["shift_p"] = _bd_to_state(s_bd), shift.reshape(B, D)
            xp = _rwkv_post(xp, y, bonus, gate, row(r_lnx_g), row(r_lnx_b), head_ones, r_out)
            sh = jnp.pad(state_shift[:, None, :], ((0, 0), (0, n_new - 1), (0, 0))).reshape(1, M, D)
            r, lw, k, v, kk, a, gate, bonus, shift = _rwkv_proj(xs, sh, g, rw, head_ones, seg=n_new)
            padded = [jnp.pad(t.reshape(DB, n_new, D), ((0, 0), (0, RWKV_CHUNK - n_new), (0, 0)))
                      for t in (r, lw, k, v, kk, a)]
            y, s_bd = _rwkv_scan(*padded, _state_to_bd(state_wkv.astype(F32)))
            outs["wkv_s"], outs["shift_s"] = _bd_to_state(s_bd), shift
            xs = _rwkv_post(xs, y[:, :n_new].reshape(1, M, D), bonus, gate, row(r_lnx_g), row(r_lnx_b), head_ones,
                            r_out)
        ffn_w = (ffn_g, ffn_up, ffn_conv_w, ffn_cb, ffn_down)
        xp, cp = _conv_ffn(xp, jnp.zeros((B, 2, F2), F32), i, *ffn_w)
        st = state_conv[i]
        e2 = jnp.pad(st, ((0, 0), (0, n_new - 2), (0, 0))).reshape(M, F2)
        e1 = jnp.pad(st[:, 1:2], ((0, 0), (0, n_new - 1), (0, 0))).reshape(M, F2)
        xs, cs = _conv_ffn(xs, jnp.stack([e2, e1]), i, *ffn_w, seg=n_new)
        conv_p.append(cp)
        conv_s.append(cs)
    return (xp, xs.reshape(DB, n_new, D), outs["k_p"], outs["v_p"], outs["logf_p"], outs["wkv_p"], outs["shift_p"],
            jnp.stack(conv_p), outs["k_s"], outs["v_s"], outs["logf_s"], outs["wkv_s"], outs["shift_s"],
            jnp.stack(conv_s), jnp.stack(chunk_v_s))
```

```python
import functools
import math

import jax
import jax.numpy as jnp
import numpy as np
from jax import lax
from jax.experimental import pallas as pl
from jax.experimental.pallas import tpu as pltpu

F32 = jnp.float32
BF = jnp.bfloat16

HEAD_DIM = 64
LANES = 128
SUBLANES = 8
PAIR = LANES // HEAD_DIM
A_GROUPS = 8
CHUNK = 128
RWKV_CHUNK = 64
NORM_EPS = 1e-6
LNX_EPS = 64e-5
NEG_BIG = -1e30
LOG2E = math.log2(math.e)
N_PIECES = 3
DECODE_PAGES = 4
VMEM_LIMIT = 56 * 1024 * 1024


def _tiles(n_tokens):
    tm = min(n_tokens, 512)
    return dict(tm=tm, tq=min(n_tokens, 1024))


def _dot(a, b):
    return jnp.dot(a, b, preferred_element_type=F32)


def _dot_nt(a, b):
    return lax.dot_general(a, b, (((1,), (1,)), ((), ())), preferred_element_type=F32)


def _dot_tn(a, b):
    return lax.dot_general(a, b, (((0,), (0,)), ((), ())), preferred_element_type=F32)


def _split3(x):
    hi = x.astype(BF)
    r1 = x - hi.astype(F32)
    mid = r1.astype(BF)
    lo = (r1 - mid.astype(F32)).astype(BF)
    return hi, mid, lo


def _dot_exact_lhs01(m01, x):
    hi, mid, lo = _split3(x)
    return _dot(m01, hi) + _dot(m01, mid) + _dot(m01, lo)


def _rms(x, g):
    return x * lax.rsqrt(jnp.mean(x * x, axis=-1, keepdims=True) + NORM_EPS) * g


def _iota(shape, axis):
    return lax.broadcasted_iota(jnp.int32, shape, axis)


def _const_spec(shape, index=None):
    nd = len(shape)
    idx = tuple(index) if index is not None else (0,) * nd
    return pl.BlockSpec(shape, lambda *_: idx, pipeline_mode=pl.Buffered(1))


def _params(sem):
    return pltpu.CompilerParams(dimension_semantics=sem, vmem_limit_bytes=VMEM_LIMIT)


def _ffn_kernel(x_ref, st_ref, g_ref, wu_ref, cw_ref, cb_ref, wd_ref, y_ref, ns_ref, hbuf_ref, act_ref, *, seg, fc,
                down_group):
    x = x_ref[...]
    tm = x.shape[0]
    d_ff = wd_ref.shape[0]
    xn = _rms(x, g_ref[...]).astype(BF)
    top = SUBLANES
    prev = slice(top - 2, top)
    if seg is None:
        @pl.when(pl.program_id(1) == 0)
        def _():
            hbuf_ref[prev, :] = st_ref[...]
    else:
        hbuf_ref[prev, :] = jnp.zeros((2, hbuf_ref.shape[1]), F32)
        pos = _iota((tm, fc), 0) % seg
    acc = jnp.zeros(y_ref.shape, F32)
    n_chunks = d_ff // fc
    offsets = lambda c: (c * fc, d_ff + c * fc)
    up = lambda c: [_dot(xn, wu_ref[:, off:off + fc]) for off in offsets(c)]
    h_next = up(0)
    for c in range(n_chunks):
        h_cur, h_next = h_next, (up(c + 1) if c + 1 < n_chunks else None)
        halves = []
        for off, h in zip(offsets(c), h_cur):
            cols = slice(off, off + fc)
            hbuf_ref[top:top + tm, cols] = h
            hm1 = hbuf_ref[top - 1:top - 1 + tm, cols]
            hm2 = hbuf_ref[top - 2:top - 2 + tm, cols]
            if seg is not None:
                hm1 = jnp.where(pos == 0, st_ref[1, :, cols], hm1)
                hm2 = jnp.where(pos < 2, st_ref[0, :, cols], hm2)
                ns_ref[:, :, cols] = h.reshape(tm // seg, seg, fc)[:, seg - 2:, :]
            halves.append(cb_ref[:, cols] + cw_ref[0:1, cols] * hm2 + cw_ref[1:2, cols] * hm1
                          + cw_ref[2:3, cols] * h)
        gate, val = halves
        act_ref[:, c * fc:(c + 1) * fc] = (gate * jax.nn.sigmoid(gate) * val).astype(BF)
        if (c + 1) % down_group == 0 or c + 1 == n_chunks:
            rows = slice((c // down_group) * down_group * fc, (c + 1) * fc)
            acc = acc + _dot(act_ref[:, rows], wd_ref[rows, :])
    y_ref[...] = x + acc
    if seg is None:
        last = hbuf_ref[top + tm - 2:top + tm, :]
        ns_ref[...] = last
        hbuf_ref[prev, :] = last


def _conv_ffn(x, state, layer, g, w_up, conv_w, conv_b, w_down, *, seg=None):
    B, T, D = x.shape
    F2 = w_up.shape[-1]
    tm = _tiles(T)["tm"]
    fc = 256
    assert T % tm == 0 and (F2 // 2) % fc == 0
    if seg is None:
        st_spec = pl.BlockSpec((None, 2, F2), lambda b, t: (b, 0, 0))
        ns_shape = jax.ShapeDtypeStruct((B, 2, F2), F32)
        ns_spec = pl.BlockSpec((None, 2, F2), lambda b, t: (b, 0, 0))
    else:
        assert B == 1 and tm == T and T % seg == 0
        st_spec = pl.BlockSpec((2, T, F2), lambda b, t: (0, 0, 0))
        ns_shape = jax.ShapeDtypeStruct((T // seg, 2, F2), F32)
        ns_spec = pl.BlockSpec((T // seg, 2, F2), lambda b, t: (0, 0, 0))
    return pl.pallas_call(
        functools.partial(_ffn_kernel, seg=seg, fc=fc, down_group=6),
        grid=(B, T // tm),
        in_specs=[
            pl.BlockSpec((None, tm, D), lambda b, t: (b, t, 0)),
            st_spec,
            _const_spec((None, 1, D), (layer, 0, 0)),
            _const_spec((None, D, F2), (layer, 0, 0)),
            _const_spec((None, 3, F2), (layer, 0, 0)),
            _const_spec((None, 1, F2), (layer, 0, 0)),
            _const_spec((None, F2 // 2, D), (layer, 0, 0)),
        ],
        out_specs=[pl.BlockSpec((None, tm, D), lambda b, t: (b, t, 0)), ns_spec],
        out_shape=[jax.ShapeDtypeStruct((B, T, D), F32), ns_shape],
        scratch_shapes=[pltpu.VMEM((SUBLANES + tm, F2), F32), pltpu.VMEM((tm, F2 // 2), BF)],
        compiler_params=_params(("arbitrary", "arbitrary")),
        name="conv_ffn",
    )(x, state, g, w_up, conv_w, conv_b, w_down)


def _cmlp_kernel(x_ref, g_ref, win_ref, vg_ref, ws_ref, bs_ref, wout_ref, y_ref, *rest, seq, chunk):
    *v_out, z_ref = rest
    out_group = A_GROUPS // 2
    x = x_ref[...]
    tm = x.shape[0]
    d_a = wout_ref.shape[0]
    gd = d_a // A_GROUPS
    xn = _rms(x, g_ref[...]).astype(BF)
    r = _iota((chunk, chunk), 0)
    c = _iota((chunk, chunk), 1)
    keep = (c <= r) if seq == chunk else ((r // seq == c // seq) & (c <= r))
    acc = jnp.zeros(y_ref.shape, F32)
    proj = lambda g: [_dot(xn, win_ref[:, off + g * gd:off + (g + 1) * gd]) for off in (0, d_a)]
    uv_next = proj(0)
    for g in range(A_GROUPS):
        (u, v), uv_next = uv_next, (proj(g + 1) if g + 1 < A_GROUPS else None)
        cols = slice(g * gd, (g + 1) * gd)
        u = jax.nn.gelu(u)
        v = jax.nn.gelu(v)
        v = v * lax.rsqrt(jnp.mean(v * v, axis=-1, keepdims=True) + NORM_EPS) * vg_ref[:, cols]
        if v_out:
            v_out[0][:, cols] = v
        wm = jnp.where(keep, ws_ref[g], 0.0).astype(BF)
        vb = v.astype(BF)
        parts = [_dot(wm, vb[j * chunk:(j + 1) * chunk, :]) + bs_ref[g] for j in range(tm // chunk)]
        mixed = parts[0] if len(parts) == 1 else jnp.concatenate(parts, axis=0)
        z_ref[:, cols] = (u * mixed).astype(BF)
        if (g + 1) % out_group == 0:
            rows = slice((g + 1 - out_group) * gd, (g + 1) * gd)
            acc = acc + _dot(z_ref[:, rows], wout_ref[rows, :])
    y_ref[...] = x + acc


def _chunk_mlp(x, j, g, w_in, v_g, w_s, b_s, w_out, *, seq, emit_v):
    B, T, D = x.shape
    chunk = w_s.shape[-1]
    d_a = w_out.shape[1]
    tm = _tiles(T)["tm"]
    assert T % tm == 0 and tm % chunk == 0
    tok = pl.BlockSpec((None, tm, D), lambda b, t: (b, t, 0))
    out_specs = [tok]
    out_shape = [jax.ShapeDtypeStruct((B, T, D), F32)]
    if emit_v:
        out_specs.append(pl.BlockSpec((None, tm, d_a), lambda b, t: (b, t, 0)))
        out_shape.append(jax.ShapeDtypeStruct((B, T, d_a), F32))
    return pl.pallas_call(
        functools.partial(_cmlp_kernel, seq=seq, chunk=chunk),
        grid=(B, T // tm),
        in_specs=[
            tok,
            _const_spec((None, 1, D), (g[1], 0, 0)),
            _const_spec((None, D, 2 * d_a), (j, 0, 0)),
            _const_spec((None, 1, d_a), (j, 0, 0)),
            _const_spec((A_GROUPS, chunk, chunk)),
            _const_spec((A_GROUPS, chunk, 1)),
            _const_spec((None, d_a, D), (j, 0, 0)),
        ],
        out_specs=out_specs,
        out_shape=out_shape,
        scratch_shapes=[pltpu.VMEM((tm, d_a), BF)],
        compiler_params=_params(("arbitrary", "arbitrary")),
        name="chunk_mlp",
    )(x, g[0], w_in, v_g, w_s, b_s, w_out)


def _aug_placement(n_heads):
    assert N_PIECES * n_heads < LANES
    w = np.zeros((LANES, 2, n_heads, LANES), np.float32)
    one = N_PIECES * n_heads
    for h in range(n_heads):
        for p in range(N_PIECES):
            w[p * n_heads + h, 0, h, HEAD_DIM + p] = 1.0
            w[one, 0, h, HEAD_DIM + N_PIECES + p] = 1.0
            w[one, 1, h, HEAD_DIM + p] = 1.0
            w[p * n_heads + h, 1, h, HEAD_DIM + N_PIECES + p] = -1.0
    return jnp.asarray(w.reshape(LANES, 2 * n_heads * LANES), BF)


def _fox_proj_kernel(x_ref, g_ref, w_ref, qg_ref, kg_ref, bf_ref, bd_ref, *rest, seg):
    x = x_ref[...]
    tm, D = x.shape
    H = D // HEAD_DIM
    xn = _rms(x, g_ref[...]).astype(BF)
    bd = bd_ref[...]

    def head_norm(t, gain):
        ms = _dot((t * t).astype(BF), bd) * (1.0 / HEAD_DIM)
        return t * lax.rsqrt(ms + NORM_EPS) * gain

    q = head_norm(_dot(xn, w_ref[:, 0:D]), qg_ref[...])
    k = head_norm(_dot(xn, w_ref[:, D:2 * D]), kg_ref[...])
    v = _dot(xn, w_ref[:, 2 * D:3 * D])
    og = jax.nn.sigmoid(_dot(xn, w_ref[:, 3 * D:4 * D])).astype(BF)
    lf = jax.nn.log_sigmoid(_dot(xn, w_ref[:, 4 * D:4 * D + LANES]) + bf_ref[...])
    if seg is not None:
        q_ref, kf_ref, vf_ref, og_ref, lf_ref = rest
        q_ref[...] = (q * (HEAD_DIM ** -0.5)).astype(BF)
    else:
        place_ref, qa_ref, ka_ref, va_ref, kf_ref, vf_ref, og_ref, lf_ref, carry_ref = rest
    kf_ref[...] = k
    vf_ref[...] = v
    og_ref[...] = og
    lf_ref[...] = lf[:, :H]
    if seg is not None:
        return

    @pl.when(pl.program_id(1) == 0)
    def _():
        carry_ref[...] = jnp.zeros_like(carry_ref)

    tri = (_iota((tm, tm), 1) <= _iota((tm, tm), 0)).astype(BF)
    cs = _dot_exact_lhs01(tri, lf) + carry_ref[...]
    carry_ref[...] = cs[tm - 1:tm, :]
    hi, mid, lo = (p.astype(F32) for p in _split3(cs * LOG2E))
    lane = _iota((tm, LANES), 1)
    pieces = jnp.where(lane < H, hi,
                       jnp.where(lane < 2 * H, pltpu.roll(mid, H, 1),
                                 jnp.where(lane < 3 * H, pltpu.roll(lo, 2 * H, 1),
                                           jnp.where(lane == 3 * H, 1.0, 0.0))))
    extras = _dot(pieces.astype(BF), place_ref[...])
    qs = q * (HEAD_DIM ** -0.5 * LOG2E)
    first = lane < HEAD_DIM
    one_lane = jnp.where(lane == HEAD_DIM, 1.0, 0.0)
    for h in range(H):
        pair = slice((h // PAIR) * LANES, (h // PAIR + 1) * LANES)

        def head_tile(t):
            tile = t[:, pair]
            return pltpu.roll(tile, HEAD_DIM, 1) if h % PAIR else tile

        qa_ref[h] = jnp.where(first, head_tile(qs), extras[:, h * LANES:(h + 1) * LANES]).astype(BF)
        ka_ref[h] = jnp.where(first, head_tile(k), extras[:, (H + h) * LANES:(H + h + 1) * LANES]).astype(BF)
        va_ref[h] = jnp.where(first, head_tile(v), one_lane).astype(BF)


def _fox_proj(x, g, w_all, q_g, k_g, b_f, bd, *, seg=None):
    B, T, D = x.shape
    H = D // HEAD_DIM
    tm = _tiles(T)["tm"]
    assert T % tm == 0 and (seg is None or (B == 1 and tm == T))
    tok = lambda n: pl.BlockSpec((None, tm, n), lambda b, t: (b, t, 0))
    consts = [w_all, q_g, k_g, b_f, bd]
    common_specs = [tok(D), tok(D), tok(D), tok(H)]
    common_shape = [jax.ShapeDtypeStruct((B, T, D), F32), jax.ShapeDtypeStruct((B, T, D), F32),
                    jax.ShapeDtypeStruct((B, T, D), BF), jax.ShapeDtypeStruct((B, T, H), F32)]
    if seg is None:
        consts.append(_aug_placement(H))
        heads = pl.BlockSpec((None, H, tm, LANES), lambda b, t: (b, 0, t, 0))
        out_specs = [heads] * 3 + common_specs
        out_shape = [jax.ShapeDtypeStruct((B, H, T, LANES), BF)] * 3 + common_shape
        scratch = [pltpu.VMEM((1, LANES), F32)]
    else:
        out_specs = [tok(D)] + common_specs
        out_shape = [jax.ShapeDtypeStruct((B, T, D), BF)] + common_shape
        scratch = []
    return pl.pallas_call(
        functools.partial(_fox_proj_kernel, seg=seg),
        grid=(B, T // tm),
        in_specs=[tok(D), _const_spec((None, 1, D), (g[1], 0, 0))] + [_const_spec(c.shape) for c in consts],
        out_specs=out_specs,
        out_shape=out_shape,
        scratch_shapes=scratch,
        compiler_params=_params(("arbitrary", "arbitrary")),
        name="fox_proj",
    )(x, g[0], *consts)


def _fox_attn_kernel(qi_ref, ki_ref, q_ref, k_ref, v_ref, og_ref, o_ref, m_ref, acc_ref, *, tq):
    step = pl.program_id(2)
    qi = qi_ref[step]
    ki = ki_ref[step]

    @pl.when(ki == 0)
    def _():
        m_ref[...] = jnp.full_like(m_ref, NEG_BIG)
        acc_ref[...] = jnp.zeros_like(acc_ref)

    def update(masked):
        s = [_dot_nt(q_ref[h], k_ref[h]) for h in range(PAIR)]
        if masked:
            visible = _iota((tq, tq), 0) >= _iota((tq, tq), 1)
            s = [jnp.where(visible, x, NEG_BIG) for x in s]
        m_prev = [m_ref[h] for h in range(PAIR)]
        m_new = [jnp.maximum(mp, jnp.max(x, axis=-1, keepdims=True)) for mp, x in zip(m_prev, s)]
        p = [jnp.exp2(x - jnp.concatenate([mn] * (tq // LANES), axis=1)).astype(BF) for x, mn in zip(s, m_new)]
        for h in range(PAIR):
            acc_ref[h] = jnp.exp2(m_prev[h] - m_new[h]) * acc_ref[h] + _dot(p[h], v_ref[h])
            m_ref[h] = m_new[h]

    @pl.when(ki < qi)
    def _():
        update(False)

    @pl.when(ki == qi)
    def _():
        update(True)
        o = [acc_ref[h] for h in range(PAIR)]
        o = [x / x[:, HEAD_DIM:HEAD_DIM + 1] for x in o]
        assert PAIR == 2
        pair = jnp.where(_iota((tq, LANES), 1) < HEAD_DIM, o[0], pltpu.roll(o[1], HEAD_DIM, 1))
        o_ref[...] = (pair * og_ref[...].astype(F32)).astype(BF)


def _fox_attn(qa, ka, va, og):
    B, H, T, _ = qa.shape
    D = og.shape[-1]
    tq = _tiles(T)["tq"]
    assert T % tq == 0 and tq % LANES == 0
    nq = T // tq
    qi_tab = np.asarray([q for q in range(nq) for _ in range(q + 1)], np.int32)
    ki_tab = np.asarray([k for q in range(nq) for k in range(q + 1)], np.int32)
    qspec = pl.BlockSpec((None, PAIR, tq, LANES), lambda b, p, s, qt, kt: (b, p, qt[s], 0))
    kspec = pl.BlockSpec((None, PAIR, tq, LANES), lambda b, p, s, qt, kt: (b, p, kt[s], 0))
    ospec = pl.BlockSpec((None, tq, LANES), lambda b, p, s, qt, kt: (b, qt[s], p))
    return pl.pallas_call(
        functools.partial(_fox_attn_kernel, tq=tq),
        grid_spec=pltpu.PrefetchScalarGridSpec(
            num_scalar_prefetch=2,
            grid=(B, H // PAIR, len(qi_tab)),
            in_specs=[qspec, kspec, kspec, ospec],
            out_specs=ospec,
            scratch_shapes=[pltpu.VMEM((PAIR, tq, LANES), F32), pltpu.VMEM((PAIR, tq, LANES), F32)],
        ),
        out_shape=jax.ShapeDtypeStruct((B, T, D), BF),
        compiler_params=_params(("arbitrary",) * 3),
        name="fox_attn",
    )(jnp.asarray(qi_tab), jnp.asarray(ki_tab), qa, ka, va, og)


def _fox_decode_kernel(pt_ref, q_ref, kn_ref, vn_ref, lfn_ref, *rest, n_new, group):
    kp_refs, vp_refs, lfp_refs = rest[:group], rest[group:2 * group], rest[2 * group:3 * group]
    og_ref, o_ref, qbd_ref, m_ref, l_ref, acc_ref, carry_ref = rest[3 * group:]
    j = pl.program_id(1)
    D = q_ref.shape[-1]
    n_heads = D // HEAD_DIM
    rows = n_heads * n_new
    page = lfp_refs[0].shape[-1]
    assert rows == LANES and page == LANES
    rr = _iota((rows, LANES), 0)
    ll = _iota((rows, LANES), 1)
    expand = (_iota((rows, n_heads), 0) // n_new == _iota((rows, n_heads), 1)).astype(BF)

    def online_update(s, pv):
        m_prev = m_ref[...]
        m_new = jnp.maximum(m_prev, jnp.max(s, axis=-1, keepdims=True))
        alpha = jnp.exp(m_prev - m_new)
        p = jnp.exp(s - m_new)
        l_ref[...] = alpha * l_ref[...] + jnp.sum(p, axis=-1, keepdims=True)
        acc_ref[...] = alpha * acc_ref[...] + pv(p.astype(BF))
        m_ref[...] = m_new

    @pl.when(j == 0)
    def _():
        q = q_ref[...].astype(F32)
        q_rows = jnp.concatenate([q] * n_heads, axis=0)
        own = _iota((rows, D), 1) // HEAD_DIM == _iota((rows, D), 0) // n_new
        qbd_ref[...] = jnp.where(own, q_rows, 0.0).astype(BF)
        m_ref[...] = jnp.full_like(m_ref, NEG_BIG)
        l_ref[...] = jnp.zeros_like(l_ref)
        acc_ref[...] = jnp.zeros_like(acc_ref)
        pad = page - n_new
        kn = jnp.concatenate([kn_ref[...], jnp.zeros((pad, D), F32)], axis=0).astype(BF)
        vn = jnp.concatenate([vn_ref[...], jnp.zeros((pad, D), F32)], axis=0).astype(BF)
        lfn = jnp.concatenate([lfn_ref[...], jnp.zeros((pad, n_heads), F32)], axis=0)
        incl = (_iota((page, page), 0) <= _iota((page, page), 1)).astype(BF)
        pieces = [_dot_nt(expand, piece).astype(BF) for piece in _split3(lfn)]
        cn = sum(_dot(piece, incl) for piece in pieces)
        t_of_row = rr % n_new
        cn_q = jnp.sum(jnp.where(ll == t_of_row, cn, 0.0), axis=-1, keepdims=True)
        s = _dot_nt(qbd_ref[...], kn) + (cn_q - cn)
        s = jnp.where(ll <= t_of_row, s, NEG_BIG)
        online_update(s, lambda p: _dot(p, vn))
        carry_ref[...] = cn_q

    @pl.when(j > 0)
    def _():
        kt = jnp.concatenate([r[...].reshape(D, page).astype(BF) for r in kp_refs], axis=1)
        vt = jnp.concatenate([r[...].reshape(D, page).astype(BF) for r in vp_refs], axis=1)
        lft = jnp.concatenate([r[...] for r in lfp_refs], axis=1)
        pieces = [_dot(expand, piece).astype(BF) for piece in _split3(lft)]
        n_keys = group * page
        later = (_iota((n_keys, n_keys), 0) > _iota((n_keys, n_keys), 1)).astype(BF)
        suffix = sum(_dot(piece, later) for piece in pieces)
        total = sum(jnp.sum(piece.astype(F32), axis=-1, keepdims=True) for piece in pieces)
        s = _dot(qbd_ref[...], kt) + (suffix + carry_ref[...])
        online_update(s, lambda p: _dot_nt(p, vt))
        carry_ref[...] = carry_ref[...] + total

    @pl.when(j == pl.num_programs(1) - 1)
    def _():
        o = acc_ref[...] / l_ref[...]
        tiles = []
        for p in range(D // LANES):
            lo = o[(PAIR * p) * n_new:(PAIR * p + 1) * n_new, p * LANES:(p + 1) * LANES]
            hi = o[(PAIR * p + 1) * n_new:(PAIR * p + 2) * n_new, p * LANES:(p + 1) * LANES]
            tiles.append(jnp.where(_iota((n_new, LANES), 1) < HEAD_DIM, lo, hi))
        o_ref[...] = (jnp.concatenate(tiles, axis=1) * og_ref[...].astype(F32)).astype(BF)


def _fox_decode(q, k_new, v_new, lf_new, og, cache_k, cache_v, cache_lf, page_table):
    DB, n_new, D = q.shape
    H = D // HEAD_DIM
    n_pool, page = cache_k.shape[:2]
    n_pages = page_table.shape[1]
    ck = cache_k.transpose(0, 2, 3, 1)
    cv = cache_v.transpose(0, 2, 3, 1)
    clf = cache_lf.transpose(0, 2, 1)
    group = DECODE_PAGES
    assert n_pages % group == 0
    new = lambda n: pl.BlockSpec((None, n_new, n), lambda b, j, pt: (b, 0, 0))

    def paged(*shape):
        zeros = (0,) * len(shape)
        return [pl.BlockSpec((None,) + shape,
                             lambda b, j, pt, g=g: (pt[b, n_pages - group * jnp.maximum(j, 1) + g],) + zeros)
                for g in range(group)]

    rows = H * n_new
    kv_page = (H, HEAD_DIM, page)
    return pl.pallas_call(
        functools.partial(_fox_decode_kernel, n_new=n_new, group=group),
        grid_spec=pltpu.PrefetchScalarGridSpec(
            num_scalar_prefetch=1,
            grid=(DB, n_pages // group + 1),
            in_specs=[new(D), new(D), new(D), new(H), *paged(*kv_page), *paged(*kv_page), *paged(H, page), new(D)],
            out_specs=new(D),
            scratch_shapes=[pltpu.VMEM((rows, D), BF), pltpu.VMEM((rows, 1), F32), pltpu.VMEM((rows, 1), F32),
                            pltpu.VMEM((rows, D), F32), pltpu.VMEM((rows, 1), F32)],
        ),
        out_shape=jax.ShapeDtypeStruct((DB, n_new, D), BF),
        compiler_params=_params(("arbitrary", "arbitrary")),
        name="fox_decode",
    )(page_table, q, k_new, v_new, lf_new, *[ck] * group, *[cv] * group, *[clf] * group, og)


def _resid_mm_kernel(x_ref, a_ref, w_ref, y_ref):
    y_ref[...] = x_ref[...] + _dot(a_ref[...], w_ref[...])


def _resid_mm(x, a, w):
    B, T, D = x.shape
    K = a.shape[-1]
    tm = _tiles(T)["tm"]
    return pl.pallas_call(
        _resid_mm_kernel,
        grid=(B, T // tm),
        in_specs=[pl.BlockSpec((None, tm, D), lambda b, t: (b, t, 0)),
                  pl.BlockSpec((None, tm, K), lambda b, t: (b, t, 0)),
                  _const_spec((K, D))],
        out_specs=pl.BlockSpec((None, tm, D), lambda b, t: (b, t, 0)),
        out_shape=jax.ShapeDtypeStruct((B, T, D), F32),
        compiler_params=_params(("arbitrary", "arbitrary")),
        name="resid_mm",
    )(x, a, w)


def _rwkv_proj_kernel(x_ref, sh_ref, g_ref, mu_ref, wr_ref, wk_ref, wv_ref, w0_ref, w1_ref, w2_ref,
                      a0_ref, a1_ref, a2_ref, g1_ref, g2_ref, kk_ref, ka_ref, rk_ref, bd_ref,
                      r_o, lw_o, k_o, v_o, kk_o, a_o, g_o, bonus_o, shift_o, *scratch, seg):
    x = x_ref[...]
    tm, D = x.shape
    xn = _rms(x, g_ref[...])
    row = _iota((tm, D), 0)
    if seg is None:
        carry_ref, = scratch

        @pl.when(pl.program_id(1) == 0)
        def _():
            carry_ref[...] = sh_ref[...]

        prev = jnp.where(row == 0, carry_ref[...], pltpu.roll(xn, 1, 0))
        carry_ref[...] = xn[tm - 1:tm, :]
        shift_o[...] = xn[tm - 1:tm, :]
    else:
        prev = jnp.where(row % seg == 0, sh_ref[...], pltpu.roll(xn, 1, 0))
        shift_o[...] = xn.reshape(tm // seg, seg, D)[:, seg - 1, :]
    xx = prev - xn
    xr, xw, xk, xv, xa, xg = ((xn + xx * mu_ref[i:i + 1, :]).astype(BF) for i in range(6))
    r = _dot(xr, wr_ref[...])
    k = _dot(xk, wk_ref[...])
    v = _dot(xv, wv_ref[...])
    lora_w = _dot(jnp.tanh(_dot(xw, w1_ref[...])).astype(BF), w2_ref[...])
    w_log = -jax.nn.softplus(-(w0_ref[...] + lora_w)) - 0.5
    lw_o[...] = -jnp.exp(w_log)
    a = jax.nn.sigmoid(a0_ref[...] + _dot(_dot(xa, a1_ref[...]).astype(BF), a2_ref[...]))
    g_o[...] = _dot(jax.nn.sigmoid(_dot(xg, g1_ref[...])).astype(BF), g2_ref[...]).astype(BF)
    bd = bd_ref[...]
    kk = k * kk_ref[...]
    kk = kk / jnp.maximum(jnp.sqrt(_dot((kk * kk).astype(BF), bd)), 1e-12)
    k = k * (1.0 + (a - 1.0) * ka_ref[...])
    r_o[...] = r
    k_o[...] = k
    v_o[...] = v.astype(BF)
    kk_o[...] = kk
    a_o[...] = a
    bonus_o[...] = _dot((r * k * rk_ref[...]).astype(BF), bd) * v


def _rwkv_proj(x, shift, g, w, bd, *, seg=None):
    B, T, D = x.shape
    tm = _tiles(T)["tm"]
    assert T % tm == 0
    tok = pl.BlockSpec((None, tm, D), lambda b, t: (b, t, 0))
    if seg is None:
        sh_spec = pl.BlockSpec((None, 1, D), lambda b, t: (b, 0, 0))
        shift_shape = jax.ShapeDtypeStruct((B, 1, D), F32)
        shift_spec = pl.BlockSpec((None, 1, D), lambda b, t: (b, 0, 0))
        scratch = [pltpu.VMEM((1, D), F32)]
    else:
        assert B == 1 and tm == T
        sh_spec = pl.BlockSpec((None, T, D), lambda b, t: (0, 0, 0))
        shift_shape = jax.ShapeDtypeStruct((T // seg, D), F32)
        shift_spec = pl.BlockSpec((T // seg, D), lambda b, t: (0, 0))
        scratch = []
    vec = _const_spec((1, D))
    mats = [w[n] for n in ("w_r", "w_k", "w_v")]
    consts = [w["mu"], *mats, w["w0"], w["w1"], w["w2"], w["a0"], w["a1"], w["a2"], w["g1"], w["g2"],
              w["k_k"], w["k_a"], w["r_k"], bd]
    return pl.pallas_call(
        functools.partial(_rwkv_proj_kernel, seg=seg),
        grid=(B, T // tm),
        in_specs=[tok, sh_spec, _const_spec((None, 1, D), (g[1], 0, 0))] + [_const_spec(c.shape) for c in consts],
        out_specs=[tok] * 8 + [shift_spec],
        out_shape=[jax.ShapeDtypeStruct((B, T, D), dt) for dt in (F32, F32, F32, BF, F32, F32, BF, F32)]
        + [shift_shape],
        scratch_shapes=scratch,
        compiler_params=_params(("arbitrary", "arbitrary")),
        name="rwkv_proj",
    )(x, shift, g[0], *consts)


def _rwkv_scan_kernel(r_ref, lw_ref, k_ref, v_ref, kk_ref, a_ref, s0_ref, y_ref, sT_ref, S_ref, *, C):
    D = r_ref.shape[-1]

    @pl.when(pl.program_id(1) == 0)
    def _():
        S_ref[...] = s0_ref[...]

    lw = lw_ref[...]
    tri = (_iota((C, C), 1) <= _iota((C, C), 0)).astype(BF)
    cum = _dot_exact_lhs01(tri, lw)
    cend = cum[C - 1:C, :]
    kk = kk_ref[...]
    kka = kk * a_ref[...]
    k = k_ref[...]
    inv_p = jnp.exp(-cum)
    to_end = jnp.exp(cend - cum)
    at = (-(kk * jnp.exp(cum - lw))).astype(BF)
    bt = (kka * inv_p).astype(BF)
    kt = (k * inv_p).astype(BF)
    rt = (r_ref[...] * jnp.exp(cum)).astype(BF)
    kh = (k * to_end).astype(BF)
    bh = (kka * to_end).astype(BF)
    vb = v_ref[...]
    p_end = jnp.exp(cend)

    first_half = _iota((C, LANES), 1) < HEAD_DIM
    n2 = PAIR * C
    ri = _iota((n2, n2), 0)
    ci = _iota((n2, n2), 1)
    same = ri // C == ci // C
    strict = same & (ci < ri)
    incl = same & (ci <= ri)
    eye = (ri == ci).astype(F32)
    same_head = _iota((LANES, LANES), 0) // HEAD_DIM == _iota((LANES, LANES), 1) // HEAD_DIM

    def stack(t):
        z = jnp.zeros_like(t)
        return jnp.concatenate([jnp.where(first_half, t, z), jnp.where(first_half, z, t)], axis=0)

    pairs = range(D // LANES)
    lanes = [slice(pr * LANES, (pr + 1) * LANES) for pr in pairs]
    at_s, rt_s, kt_s, bt_s, v_s = ([stack(t[:, ln]) for ln in lanes] for t in (at, rt, kt, bt, vb))
    aa = [_dot_nt(jnp.concatenate([at_s[pr], rt_s[pr]], axis=0), jnp.concatenate([kt_s[pr], bt_s[pr]], axis=0))
          for pr in pairs]
    a_ak = [jnp.where(strict, x[:n2, :n2], 0.0).astype(BF) for x in aa]
    a_ab = [jnp.where(strict, x[:n2, n2:], 0.0) for x in aa]
    a_rk = [jnp.where(incl, x[n2:, :n2], 0.0).astype(BF) for x in aa]
    a_rb = [jnp.where(incl, x[n2:, n2:], 0.0).astype(BF) for x in aa]
    inv = [eye + x for x in a_ab]
    npow = a_ab
    for _ in range(int(math.log2(C)) - 1):
        nb = [x.astype(BF) for x in npow]
        npow = [_dot(x, x) for x in nb]
        inv = [i + _dot(i.astype(BF), n.astype(BF)) for i, n in zip(inv, npow)]
    S = [S_ref[pr] for pr in pairs]
    Sb = [x.astype(BF) for x in S]
    rhs = [(_dot_nt(at_s[pr], Sb[pr]) + _dot(a_ak[pr], v_s[pr])).astype(BF) for pr in pairs]
    u = [_dot(inv[pr].astype(BF), rhs[pr]) for pr in pairs]
    y = [_dot_nt(rt_s[pr], Sb[pr]) + _dot(a_rk[pr], v_s[pr]) + _dot(a_rb[pr], u[pr].astype(BF)) for pr in pairs]
    for pr in pairs:
        y_ref[:, lanes[pr]] = y[pr][:C] + y[pr][C:]
        u_pair = (u[pr][:C] + u[pr][C:]).astype(BF)
        dS = _dot_tn(vb[:, lanes[pr]], kh[:, lanes[pr]]) + _dot_tn(u_pair, bh[:, lanes[pr]])
        S_ref[pr] = S[pr] * p_end[:, lanes[pr]] + jnp.where(same_head, dS, 0.0)

    @pl.when(pl.program_id(1) == pl.num_programs(1) - 1)
    def _():
        sT_ref[...] = S_ref[...]


def _rwkv_scan(r, lw, k, v, kk, a, s0_bd):
    B, T, D = r.shape
    C = RWKV_CHUNK
    assert T % C == 0
    n_pairs = D // LANES
    tok = pl.BlockSpec((None, C, D), lambda b, t: (b, t, 0))
    st = pl.BlockSpec((None, n_pairs, LANES, LANES), lambda b, t: (b, 0, 0, 0))
    return pl.pallas_call(
        functools.partial(_rwkv_scan_kernel, C=C),
        grid=(B, T // C),
        in_specs=[tok] * 6 + [st],
        out_specs=[tok, st],
        out_shape=[jax.ShapeDtypeStruct((B, T, D), F32), jax.ShapeDtypeStruct(s0_bd.shape, F32)],
        scratch_shapes=[pltpu.VMEM((n_pairs, LANES, LANES), F32)],
        compiler_params=_params(("arbitrary", "arbitrary")),
        name="rwkv_scan",
    )(r, lw, k, v, kk, a, s0_bd)


def _state_to_bd(s):
    B, H = s.shape[:2]
    sp = s.reshape(B, H // PAIR, PAIR, HEAD_DIM, HEAD_DIM)
    eye = jnp.eye(PAIR, dtype=s.dtype)
    bd = sp[:, :, :, :, None, :] * eye[None, None, :, None, :, None]
    return bd.reshape(B, H // PAIR, LANES, LANES)


def _bd_to_state(bd):
    B, P = bd.shape[:2]
    t = bd.reshape(B, P, PAIR, HEAD_DIM, PAIR, HEAD_DIM)
    return jnp.stack([t[:, :, h, :, h, :] for h in range(PAIR)], axis=2).reshape(B, P * PAIR, HEAD_DIM, HEAD_DIM)


def _rwkv_post_kernel(x_ref, y_ref, bonus_ref, g_ref, lng_ref, lnb_ref, bd_ref, wo_ref, o_ref):
    y = y_ref[...]
    bd = bd_ref[...]
    d = y - _dot(y.astype(BF), bd) * (1.0 / HEAD_DIM)
    var = _dot((d * d).astype(BF), bd) * (1.0 / HEAD_DIM)
    yn = d * lax.rsqrt(var + LNX_EPS) * lng_ref[...] + lnb_ref[...] + bonus_ref[...]
    o_ref[...] = x_ref[...] + _dot((yn * g_ref[...].astype(F32)).astype(BF), wo_ref[...])


def _rwkv_post(x, y, bonus, g, ln_g, ln_b, bd, w_o):
    B, T, D = x.shape
    tm = _tiles(T)["tm"]
    tok = pl.BlockSpec((None, tm, D), lambda b, t: (b, t, 0))
    return pl.pallas_call(
        _rwkv_post_kernel,
        grid=(B, T // tm),
        in_specs=[tok, tok, tok, tok, _const_spec((1, D)), _const_spec((1, D)), _const_spec((D, D)),
                  _const_spec((D, D))],
        out_specs=tok,
        out_shape=jax.ShapeDtypeStruct((B, T, D), F32),
        compiler_params=_params(("arbitrary", "arbitrary")),
        name="rwkv_post",
    )(x, y, bonus, g, ln_g, ln_b, bd, w_o)


def _pad_cols(w, n):
    return jnp.pad(w, ((0, 0), (0, n - w.shape[1])))


def _pad_rows(w, n):
    return jnp.pad(w, ((0, n - w.shape[0]), (0, 0)))


def kernel(x_prompt, x_sample, cache_k, cache_v, cache_logf, page_table, state_wkv, state_shift, state_conv,
           mix_norm, a_w_in, a_v_norm, a_w_s, a_b_s, a_w_out,
           f_w_qkv, f_q_norm, f_k_norm, f_w_fgate, f_b_fgate, f_w_ogate, f_w_out,
           r_mu, r_w_r, r_w_k, r_w_v, r_w0, r_w1, r_w2, r_a0, r_a1, r_a2, r_g1, r_g2,
           r_k_k, r_k_a, r_r_k, r_lnx_g, r_lnx_b, r_w_o,
           ffn_norm, ffn_w_up, ffn_conv_w, ffn_conv_b, ffn_w_down):
    B, T, D = x_prompt.shape
    DB, n_new, _ = x_sample.shape
    H = D // HEAD_DIM
    depth = ffn_w_up.shape[0]
    F2 = ffn_w_up.shape[-1]
    M = DB * n_new
    row = lambda v: v.reshape(1, -1).astype(F32)

    mix_g = mix_norm.reshape(depth, 1, D)
    ffn_g = ffn_norm.reshape(depth, 1, D)
    ffn_up, ffn_down = ffn_w_up.astype(BF), ffn_w_down.astype(BF)
    ffn_cb = ffn_conv_b.reshape(depth, 1, F2)
    a_in, a_out = a_w_in.astype(BF), a_w_out.astype(BF)
    a_vg = a_v_norm.reshape(a_v_norm.shape[0], 1, -1)
    head_ones = (jnp.arange(D)[:, None] // HEAD_DIM == jnp.arange(D)[None, :] // HEAD_DIM).astype(BF)
    f_all = jnp.concatenate([f_w_qkv, f_w_ogate, _pad_cols(f_w_fgate, LANES)], axis=1).astype(BF)
    f_qg, f_kg = row(jnp.tile(f_q_norm, H)), row(jnp.tile(f_k_norm, H))
    f_bf = _pad_cols(row(f_b_fgate), LANES)
    f_out = f_w_out.astype(BF)
    lora = lambda w1, w2, n: (_pad_cols(w1, n).astype(BF), _pad_rows(w2, n).astype(BF))
    rw = dict(mu=r_mu, w_r=r_w_r.astype(BF), w_k=r_w_k.astype(BF), w_v=r_w_v.astype(BF),
              w0=row(r_w0), a0=row(r_a0), k_k=row(r_k_k), k_a=row(r_k_a), r_k=row(r_r_k))
    rw["w1"], rw["w2"] = lora(r_w1, r_w2, LANES)
    rw["a1"], rw["a2"] = lora(r_a1, r_a2, LANES)
    rw["g1"], rw["g2"] = lora(r_g1, r_g2, 2 * LANES)
    r_out = r_w_o.astype(BF)

    xp = x_prompt
    xs = x_sample.reshape(1, M, D)
    conv_p, conv_s, chunk_v_s = [], [], []
    outs = {}
    for i in range(depth):
        kind = i % 3
        g = (mix_g, i)
        if kind == 0:
            j = i // 3
            ws_p = a_w_s[j]
            bs_p = a_b_s[j][:, :, None]
            xp, = _chunk_mlp(xp, j, g, a_in, a_vg, ws_p, bs_p, a_out, seq=CHUNK, emit_v=False)
            reps = M // n_new
            ws_s = jnp.tile(a_w_s[j][:, :n_new, :n_new], (1, reps, reps))
            bs_s = jnp.tile(a_b_s[j][:, :n_new], (1, reps))[:, :, None]
            xs, v_rows = _chunk_mlp(xs, j, g, a_in, a_vg, ws_s, bs_s, a_out, seq=n_new, emit_v=True)
            chunk_v_s.append(v_rows.reshape(DB, n_new, -1))
        elif kind == 1:
            qa, ka, va, kf, vf, og, lf = _fox_proj(xp, g, f_all, f_qg, f_kg, f_bf, head_ones)
            outs["k_p"], outs["v_p"], outs["logf_p"] = (kf.reshape(B, T, H, HEAD_DIM), vf.reshape(B, T, H, HEAD_DIM), lf)
            xp = _resid_mm(xp, _fox_attn(qa, ka, va, og), f_out)
            q, kf, vf, og, lf = _fox_proj(xs, g, f_all, f_qg, f_kg, f_bf, head_ones, seg=n_new)
            shp = (DB, n_new, D)
            outs["k_s"], outs["v_s"] = kf.reshape(DB, n_new, H, HEAD_DIM), vf.reshape(DB, n_new, H, HEAD_DIM)
            outs["logf_s"] = lf.reshape(DB, n_new, H)
            att = _fox_decode(q.reshape(shp), kf.reshape(shp), vf.reshape(shp), outs["logf_s"], og.reshape(shp),
                              cache_k, cache_v, cache_logf, page_table)
            xs = _resid_mm(xs, att.reshape(1, M, D), f_out)
        else:
            r, lw, k, v, kk, a, gate, bonus, shift = _rwkv_proj(xp, jnp.zeros((B, 1, D), F32), g, rw, head_ones)
            y, s_bd = _rwkv_scan(r, lw, k, v, kk, a, jnp.zeros((B, D // LANES, LANES, LANES), F32))
            outs["wkv_p"], outs["shift_p"] = _bd_to_state(s_bd), shift.reshape(B, D)
            xp = _rwkv_post(xp, y, bonus, gate, row(r_lnx_g), row(r_lnx_b), head_ones, r_out)
            sh = jnp.pad(state_shift[:, None, :], ((0, 0), (0, n_new - 1), (0, 0))).reshape(1, M, D)
            r, lw, k, v, kk, a, gate, bonus, shift = _rwkv_proj(xs, sh, g, rw, head_ones, seg=n_new)
            padded = [jnp.pad(t.reshape(DB, n_new, D), ((0, 0), (0, RWKV_CHUNK - n_new), (0, 0)))
                      for t in (r, lw, k, v, kk, a)]
            y, s_bd = _rwkv_scan(*padded, _state_to_bd(state_wkv.astype(F32)))
            outs["wkv_s"], outs["shift_s"] = _bd_to_state(s_bd), shift
            xs = _rwkv_post(xs, y[:, :n_new].reshape(1, M, D), bonus, gate, row(r_lnx_g), row(r_lnx_b), head_ones,
                            r_out)
        ffn_w = (ffn_g, ffn_up, ffn_conv_w, ffn_cb, ffn_down)
        xp, cp = _conv_ffn(xp, jnp.zeros((B, 2, F2), F32), i, *ffn_w)
        st = state_conv[i]
        e2 = jnp.pad(st, ((0, 0), (0, n_new - 2), (0, 0))).reshape(M, F2)
        e1 = jnp.pad(st[:, 1:2], ((0, 0), (0, n_new - 1), (0, 0))).reshape(M, F2)
        xs, cs = _conv_ffn(xs, jnp.stack([e2, e1]), i, *ffn_w, seg=n_new)
        conv_p.append(cp)
        conv_s.append(cs)
    return (xp, xs.reshape(DB, n_new, D), outs["k_p"], outs["v_p"], outs["logf_p"], outs["wkv_p"], outs["shift_p"],
            jnp.stack(conv_p), outs["k_s"], outs["v_s"], outs["logf_s"], outs["wkv_s"], outs["shift_s"],
            jnp.stack(conv_s), jnp.stack(chunk_v_s))
```

```python
import functools
import math

import jax
import jax.numpy as jnp
import numpy as np
from jax import lax
from jax.experimental import pallas as pl
from jax.experimental.pallas import tpu as pltpu

F32 = jnp.float32
BF = jnp.bfloat16

HEAD_DIM = 64
LANES = 128
SUBLANES = 8
PAIR = LANES // HEAD_DIM
A_GROUPS = 8
CHUNK = 128
RWKV_CHUNK = 64
RWKV_SEQS = 2
NORM_EPS = 1e-6
LNX_EPS = 64e-5
NEG_BIG = -1e30
LOG2E = math.log2(math.e)
N_PIECES = 3
ATTN_HEADS = 4
DECODE_PAGES = 4
DECODE_SEQS = 2
VMEM_LIMIT = 56 * 1024 * 1024


def _tiles(n_tokens):
    tm = min(n_tokens, 512)
    return dict(tm=tm, tq=min(n_tokens, 1024))


def _dot(a, b):
    return jnp.dot(a, b, preferred_element_type=F32)


def _dot_nt(a, b):
    return lax.dot_general(a, b, (((1,), (1,)), ((), ())), preferred_element_type=F32)


def _dot_tn(a, b):
    return lax.dot_general(a, b, (((0,), (0,)), ((), ())), preferred_element_type=F32)


def _split3(x):
    hi = x.astype(BF)
    r1 = x - hi.astype(F32)
    mid = r1.astype(BF)
    lo = (r1 - mid.astype(F32)).astype(BF)
    return hi, mid, lo


def _dot_exact_lhs01(m01, x):
    hi, mid, lo = _split3(x)
    return _dot(m01, hi) + _dot(m01, mid) + _dot(m01, lo)


def _rms(x, g):
    return x * lax.rsqrt(jnp.mean(x * x, axis=-1, keepdims=True) + NORM_EPS) * g


def _iota(shape, axis):
    return lax.broadcasted_iota(jnp.int32, shape, axis)


def _const_spec(shape, index=None):
    nd = len(shape)
    idx = tuple(index) if index is not None else (0,) * nd
    return pl.BlockSpec(shape, lambda *_: idx, pipeline_mode=pl.Buffered(1))


def _params(sem):
    return pltpu.CompilerParams(dimension_semantics=sem, vmem_limit_bytes=VMEM_LIMIT)


def _ffn_kernel(x_ref, st_ref, g_ref, wu_ref, cw_ref, cb_ref, wd_ref, y_ref, ns_ref, hbuf_ref, act_ref, *, seg, fc,
                down_group):
    x = x_ref[...]
    tm = x.shape[0]
    d_ff = wd_ref.shape[0]
    xn = _rms(x, g_ref[...]).astype(BF)
    top = SUBLANES
    prev = slice(top - 2, top)
    if seg is None:
        @pl.when(pl.program_id(1) == 0)
        def _():
            hbuf_ref[prev, :] = st_ref[...]
    else:
        hbuf_ref[prev, :] = jnp.zeros((2, hbuf_ref.shape[1]), F32)
        pos = _iota((tm, fc), 0) % seg
    acc = jnp.zeros(y_ref.shape, F32)
    n_chunks = d_ff // fc
    offsets = lambda c: (c * fc, d_ff + c * fc)
    up = lambda c: [_dot(xn, wu_ref[:, off:off + fc]) for off in offsets(c)]
    h_next = up(0)
    for c in range(n_chunks):
        h_cur, h_next = h_next, (up(c + 1) if c + 1 < n_chunks else None)
        halves = []
        for off, h in zip(offsets(c), h_cur):
            cols = slice(off, off + fc)
            hbuf_ref[top:top + tm, cols] = h
            hm1 = hbuf_ref[top - 1:top - 1 + tm, cols]
            hm2 = hbuf_ref[top - 2:top - 2 + tm, cols]
            if seg is not None:
                hm1 = jnp.where(pos == 0, st_ref[1, :, cols], hm1)
                hm2 = jnp.where(pos < 2, st_ref[0, :, cols], hm2)
                ns_ref[:, :, cols] = h.reshape(tm // seg, seg, fc)[:, seg - 2:, :]
            halves.append(cb_ref[:, cols] + cw_ref[0:1, cols] * hm2 + cw_ref[1:2, cols] * hm1
                          + cw_ref[2:3, cols] * h)
        gate, val = halves
        act_ref[:, c * fc:(c + 1) * fc] = (gate * jax.nn.sigmoid(gate) * val).astype(BF)
        if (c + 1) % down_group == 0 or c + 1 == n_chunks:
            rows = slice((c // down_group) * down_group * fc, (c + 1) * fc)
            acc = acc + _dot(act_ref[:, rows], wd_ref[rows, :])
    y_ref[...] = x + acc
    if seg is None:
        last = hbuf_ref[top + tm - 2:top + tm, :]
        ns_ref[...] = last
        hbuf_ref[prev, :] = last


def _conv_ffn(x, state, layer, g, w_up, conv_w, conv_b, w_down, *, seg=None):
    B, T, D = x.shape
    F2 = w_up.shape[-1]
    tm = _tiles(T)["tm"]
    fc = 256
    assert T % tm == 0 and (F2 // 2) % fc == 0
    if seg is None:
        st_spec = pl.BlockSpec((None, 2, F2), lambda b, t: (b, 0, 0))
        ns_shape = jax.ShapeDtypeStruct((B, 2, F2), F32)
        ns_spec = pl.BlockSpec((None, 2, F2), lambda b, t: (b, 0, 0))
    else:
        assert B == 1 and tm == T and T % seg == 0
        st_spec = pl.BlockSpec((2, T, F2), lambda b, t: (0, 0, 0))
        ns_shape = jax.ShapeDtypeStruct((T // seg, 2, F2), F32)
        ns_spec = pl.BlockSpec((T // seg, 2, F2), lambda b, t: (0, 0, 0))
    return pl.pallas_call(
        functools.partial(_ffn_kernel, seg=seg, fc=fc, down_group=6),
        grid=(B, T // tm),
        in_specs=[
            pl.BlockSpec((None, tm, D), lambda b, t: (b, t, 0)),
            st_spec,
            _const_spec((None, 1, D), (layer, 0, 0)),
            _const_spec((None, D, F2), (layer, 0, 0)),
            _const_spec((None, 3, F2), (layer, 0, 0)),
            _const_spec((None, 1, F2), (layer, 0, 0)),
            _const_spec((None, F2 // 2, D), (layer, 0, 0)),
        ],
        out_specs=[pl.BlockSpec((None, tm, D), lambda b, t: (b, t, 0)), ns_spec],
        out_shape=[jax.ShapeDtypeStruct((B, T, D), F32), ns_shape],
        scratch_shapes=[pltpu.VMEM((SUBLANES + tm, F2), F32), pltpu.VMEM((tm, F2 // 2), BF)],
        compiler_params=_params(("arbitrary", "arbitrary")),
        name="conv_ffn",
    )(x, state, g, w_up, conv_w, conv_b, w_down)


def _cmlp_kernel(x_ref, g_ref, win_ref, vg_ref, ws_ref, bs_ref, wout_ref, y_ref, *rest, seq, chunk):
    *v_out, z_ref = rest
    out_group = A_GROUPS // 2
    x = x_ref[...]
    tm = x.shape[0]
    d_a = wout_ref.shape[0]
    gd = d_a // A_GROUPS
    xn = _rms(x, g_ref[...]).astype(BF)
    span = ws_ref.shape[-1]
    r = _iota((span, span), 0)
    c = _iota((span, span), 1)
    if seq == chunk:
        assert chunk == span
        mix_of = lambda g: jnp.where(c <= r, ws_ref[g], 0.0).astype(BF)
        bias_of = lambda g: bs_ref[g]
    else:
        pick = (_iota((chunk, span), 0) % seq == _iota((chunk, span), 1)).astype(BF)
        same_seq = _iota((chunk, chunk), 0) // seq == _iota((chunk, chunk), 1) // seq

        def mix_of(g):
            corner = jnp.where((c <= r) & (r < seq), ws_ref[g], 0.0).astype(BF)
            return jnp.where(same_seq, _dot_nt(_dot(pick, corner).astype(BF), pick), 0.0).astype(BF)

        bias_of = lambda g: _dot_exact_lhs01(pick, jnp.broadcast_to(bs_ref[g], (span, gd)))
    acc = jnp.zeros(y_ref.shape, F32)
    proj = lambda g: [_dot(xn, win_ref[:, off + g * gd:off + (g + 1) * gd]) for off in (0, d_a)]
    uv_next = proj(0)
    for g in range(A_GROUPS):
        (u, v), uv_next = uv_next, (proj(g + 1) if g + 1 < A_GROUPS else None)
        cols = slice(g * gd, (g + 1) * gd)
        u = jax.nn.gelu(u)
        v = jax.nn.gelu(v)
        v = v * lax.rsqrt(jnp.mean(v * v, axis=-1, keepdims=True) + NORM_EPS) * vg_ref[:, cols]
        if v_out:
            v_out[0][:, cols] = v
        wm = mix_of(g)
        bias = bias_of(g)
        vb = v.astype(BF)
        parts = [_dot(wm, vb[j * chunk:(j + 1) * chunk, :]) + bias for j in range(tm // chunk)]
        mixed = parts[0] if len(parts) == 1 else jnp.concatenate(parts, axis=0)
        z_ref[:, cols] = (u * mixed).astype(BF)
        if (g + 1) % out_group == 0:
            rows = slice((g + 1 - out_group) * gd, (g + 1) * gd)
            acc = acc + _dot(z_ref[:, rows], wout_ref[rows, :])
    y_ref[...] = x + acc


def _chunk_mlp(x, j, g, w_in, v_g, w_s, b_s, w_out, *, seq, emit_v):
    B, T, D = x.shape
    span = w_s.shape[-1]
    d_a = w_out.shape[1]
    tm = _tiles(T)["tm"]
    chunk = span if seq == span else tm
    assert T % tm == 0 and tm % chunk == 0 and chunk % seq == 0
    tok = pl.BlockSpec((None, tm, D), lambda b, t: (b, t, 0))
    out_specs = [tok]
    out_shape = [jax.ShapeDtypeStruct((B, T, D), F32)]
    if emit_v:
        out_specs.append(pl.BlockSpec((None, tm, d_a), lambda b, t: (b, t, 0)))
        out_shape.append(jax.ShapeDtypeStruct((B, T, d_a), F32))
    return pl.pallas_call(
        functools.partial(_cmlp_kernel, seq=seq, chunk=chunk),
        grid=(B, T // tm),
        in_specs=[
            tok,
            _const_spec((None, 1, D), (g[1], 0, 0)),
            _const_spec((None, D, 2 * d_a), (j, 0, 0)),
            _const_spec((None, 1, d_a), (j, 0, 0)),
            _const_spec((A_GROUPS, span, span)),
            _const_spec((A_GROUPS, span, 1)),
            _const_spec((None, d_a, D), (j, 0, 0)),
        ],
        out_specs=out_specs,
        out_shape=out_shape,
        scratch_shapes=[pltpu.VMEM((tm, d_a), BF)],
        compiler_params=_params(("arbitrary", "arbitrary")),
        name="chunk_mlp",
    )(x, g[0], w_in, v_g, w_s, b_s, w_out)


def _aug_placement(n_heads):
    assert N_PIECES * n_heads < LANES
    w = np.zeros((LANES, 2, n_heads, LANES), np.float32)
    one = N_PIECES * n_heads
    for h in range(n_heads):
        for p in range(N_PIECES):
            w[p * n_heads + h, 0, h, HEAD_DIM + p] = 1.0
            w[one, 0, h, HEAD_DIM + N_PIECES + p] = 1.0
            w[one, 1, h, HEAD_DIM + p] = 1.0
            w[p * n_heads + h, 1, h, HEAD_DIM + N_PIECES + p] = -1.0
    return jnp.asarray(w.reshape(LANES, 2 * n_heads * LANES), BF)


def _fox_proj_kernel(x_ref, g_ref, w_ref, qg_ref, kg_ref, bf_ref, bd_ref, *rest, decode):
    x = x_ref[...]
    tm, D = x.shape
    H = D // HEAD_DIM
    xn = _rms(x, g_ref[...]).astype(BF)
    bd = bd_ref[...]

    def head_norm(t, gain):
        ms = _dot((t * t).astype(BF), bd) * (1.0 / HEAD_DIM)
        return t * lax.rsqrt(ms + NORM_EPS) * gain

    q = head_norm(_dot(xn, w_ref[:, 0:D]), qg_ref[...])
    k = head_norm(_dot(xn, w_ref[:, D:2 * D]), kg_ref[...])
    v = _dot(xn, w_ref[:, 2 * D:3 * D])
    og = jax.nn.sigmoid(_dot(xn, w_ref[:, 3 * D:4 * D])).astype(BF)
    lf = jax.nn.log_sigmoid(_dot(xn, w_ref[:, 4 * D:4 * D + LANES]) + bf_ref[...])
    if decode:
        q_ref, kf_ref, vf_ref, og_ref, lf_ref = rest
        q_ref[...] = (q * (HEAD_DIM ** -0.5)).astype(BF)
        kf_ref[...] = k
        vf_ref[...] = v
        og_ref[...] = og
        lf_ref[...] = lf[:, :H]
        return
    place_ref, qa_ref, ka_ref, va_ref, kf_ref, vf_ref, og_ref, lf_ref, carry_ref = rest
    og_ref[...] = og
    kf_ref[...] = k.T.reshape(H, HEAD_DIM, tm)
    vf_ref[...] = v.T.reshape(H, HEAD_DIM, tm)
    lf_ref[...] = lf.T[:H, :]

    @pl.when(pl.program_id(1) == 0)
    def _():
        carry_ref[...] = jnp.zeros_like(carry_ref)

    tri = (_iota((tm, tm), 1) <= _iota((tm, tm), 0)).astype(BF)
    cs = _dot_exact_lhs01(tri, lf) + carry_ref[...]
    carry_ref[...] = cs[tm - 1:tm, :]
    hi, mid, lo = (p.astype(F32) for p in _split3(cs * LOG2E))
    lane = _iota((tm, LANES), 1)
    pieces = jnp.where(lane < H, hi,
                       jnp.where(lane < 2 * H, pltpu.roll(mid, H, 1),
                                 jnp.where(lane < 3 * H, pltpu.roll(lo, 2 * H, 1),
                                           jnp.where(lane == 3 * H, 1.0, 0.0))))
    extras = _dot(pieces.astype(BF), place_ref[...])
    qs = q * (HEAD_DIM ** -0.5 * LOG2E)
    first = lane < HEAD_DIM
    one_lane = jnp.where(lane == HEAD_DIM, 1.0, 0.0)
    for h in range(H):
        pair = slice((h // PAIR) * LANES, (h // PAIR + 1) * LANES)

        def head_tile(t):
            tile = t[:, pair]
            return pltpu.roll(tile, HEAD_DIM, 1) if h % PAIR else tile

        qa_ref[h] = jnp.where(first, head_tile(qs), extras[:, h * LANES:(h + 1) * LANES]).astype(BF)
        ka_ref[h] = jnp.where(first, head_tile(k), extras[:, (H + h) * LANES:(H + h + 1) * LANES]).astype(BF)
        va_ref[h] = jnp.where(first, head_tile(v), one_lane).astype(BF)


def _fox_proj(x, g, w_all, q_g, k_g, b_f, bd, *, decode):
    B, T, D = x.shape
    H = D // HEAD_DIM
    tm = _tiles(T)["tm"]
    assert T % tm == 0
    tok = lambda n: pl.BlockSpec((None, tm, n), lambda b, t: (b, t, 0))
    consts = [w_all, q_g, k_g, b_f, bd]
    if decode:
        out_specs = [tok(D), tok(D), tok(D), tok(D), tok(H)]
        out_shape = [jax.ShapeDtypeStruct((B, T, D), BF), jax.ShapeDtypeStruct((B, T, D), F32),
                     jax.ShapeDtypeStruct((B, T, D), F32), jax.ShapeDtypeStruct((B, T, D), BF),
                     jax.ShapeDtypeStruct((B, T, H), F32)]
        scratch = []
    else:
        consts.append(_aug_placement(H))
        heads = pl.BlockSpec((None, H, tm, LANES), lambda b, t: (b, 0, t, 0))
        kv_t = pl.BlockSpec((None, H, HEAD_DIM, tm), lambda b, t: (b, 0, 0, t))
        out_specs = [heads] * 3 + [kv_t, kv_t, tok(D), pl.BlockSpec((None, H, tm), lambda b, t: (b, 0, t))]
        out_shape = ([jax.ShapeDtypeStruct((B, H, T, LANES), BF)] * 3
                     + [jax.ShapeDtypeStruct((B, H, HEAD_DIM, T), F32)] * 2
                     + [jax.ShapeDtypeStruct((B, T, D), BF), jax.ShapeDtypeStruct((B, H, T), F32)])
        scratch = [pltpu.VMEM((1, LANES), F32)]
    return pl.pallas_call(
        functools.partial(_fox_proj_kernel, decode=decode),
        grid=(B, T // tm),
        in_specs=[tok(D), _const_spec((None, 1, D), (g[1], 0, 0))] + [_const_spec(c.shape) for c in consts],
        out_specs=out_specs,
        out_shape=out_shape,
        scratch_shapes=scratch,
        compiler_params=_params(("arbitrary", "arbitrary")),
        name="fox_proj",
    )(x, g[0], *consts)


def _fox_attn_kernel(qi_ref, ki_ref, q_ref, k_ref, v_ref, og_ref, o_ref, m_ref, acc_ref, *, tq):
    step = pl.program_id(2)
    qi = qi_ref[step]
    ki = ki_ref[step]
    heads = range(q_ref.shape[0])

    @pl.when(ki == 0)
    def _():
        m_ref[...] = jnp.full_like(m_ref, NEG_BIG)
        acc_ref[...] = jnp.zeros_like(acc_ref)

    def update(masked):
        s = [_dot_nt(q_ref[h], k_ref[h]) for h in heads]
        if masked:
            visible = _iota((tq, tq), 0) >= _iota((tq, tq), 1)
            s = [jnp.where(visible, x, NEG_BIG) for x in s]
        m_prev = [m_ref[h] for h in heads]
        m_new = [jnp.maximum(mp, jnp.max(x, axis=-1, keepdims=True)) for mp, x in zip(m_prev, s)]
        p = [jnp.exp2(x - jnp.concatenate([mn] * (tq // LANES), axis=1)).astype(BF) for x, mn in zip(s, m_new)]
        for h in heads:
            acc_ref[h] = jnp.exp2(m_prev[h] - m_new[h]) * acc_ref[h] + _dot(p[h], v_ref[h])
            m_ref[h] = m_new[h]

    @pl.when(ki < qi)
    def _():
        update(False)

    @pl.when(ki == qi)
    def _():
        update(True)
        o = [acc_ref[h] for h in heads]
        o = [x / x[:, HEAD_DIM:HEAD_DIM + 1] for x in o]
        assert PAIR == 2
        first = _iota((tq, LANES), 1) < HEAD_DIM
        pairs = [jnp.where(first, o[h], pltpu.roll(o[h + 1], HEAD_DIM, 1)) for h in heads[::PAIR]]
        o_ref[...] = (jnp.concatenate(pairs, axis=1) * og_ref[...].astype(F32)).astype(BF)


def _fox_attn(qa, ka, va, og):
    B, H, T, _ = qa.shape
    D = og.shape[-1]
    tq = _tiles(T)["tq"]
    assert T % tq == 0 and tq % LANES == 0
    nq = T // tq
    qi_tab = np.asarray([q for q in range(nq) for _ in range(q + 1)], np.int32)
    ki_tab = np.asarray([k for q in range(nq) for k in range(q + 1)], np.int32)
    hs = ATTN_HEADS
    assert H % hs == 0 and hs % PAIR == 0
    qspec = pl.BlockSpec((None, hs, tq, LANES), lambda b, p, s, qt, kt: (b, p, qt[s], 0))
    kspec = pl.BlockSpec((None, hs, tq, LANES), lambda b, p, s, qt, kt: (b, p, kt[s], 0))
    ospec = pl.BlockSpec((None, tq, hs * HEAD_DIM), lambda b, p, s, qt, kt: (b, qt[s], p))
    return pl.pallas_call(
        functools.partial(_fox_attn_kernel, tq=tq),
        grid_spec=pltpu.PrefetchScalarGridSpec(
            num_scalar_prefetch=2,
            grid=(B, H // hs, len(qi_tab)),
            in_specs=[qspec, kspec, kspec, ospec],
            out_specs=ospec,
            scratch_shapes=[pltpu.VMEM((hs, tq, LANES), F32), pltpu.VMEM((hs, tq, LANES), F32)],
        ),
        out_shape=jax.ShapeDtypeStruct((B, T, D), BF),
        compiler_params=_params(("arbitrary",) * 3),
        name="fox_attn",
    )(jnp.asarray(qi_tab), jnp.asarray(ki_tab), qa, ka, va, og)


def _fox_decode_kernel(pt_ref, q_ref, kn_ref, vn_ref, lfn_ref, *rest, n_new, group, nb):
    n_in = nb * group
    kp_refs, vp_refs, lfp_refs = rest[:n_in], rest[n_in:2 * n_in], rest[2 * n_in:3 * n_in]
    og_ref, o_ref, qbd_ref, m_ref, l_ref, acc_ref, carry_ref = rest[3 * n_in:]
    j = pl.program_id(1)
    D = q_ref.shape[-1]
    n_heads = D // HEAD_DIM
    rows = n_heads * n_new
    page = lfp_refs[0].shape[-1]
    assert rows == LANES and page == LANES
    seqs = range(nb)
    rr = _iota((rows, LANES), 0)
    ll = _iota((rows, LANES), 1)
    expand = (_iota((rows, n_heads), 0) // n_new == _iota((rows, n_heads), 1)).astype(BF)

    def online_update(s, pv):
        m_prev = [m_ref[i] for i in seqs]
        m_new = [jnp.maximum(m_prev[i], jnp.max(s[i], axis=-1, keepdims=True)) for i in seqs]
        p = [jnp.exp(s[i] - m_new[i]) for i in seqs]
        for i in seqs:
            alpha = jnp.exp(m_prev[i] - m_new[i])
            l_ref[i] = alpha * l_ref[i] + jnp.sum(p[i], axis=-1, keepdims=True)
            acc_ref[i] = alpha * acc_ref[i] + pv(i, p[i].astype(BF))
            m_ref[i] = m_new[i]

    @pl.when(j == 0)
    def _():
        m_ref[...] = jnp.full_like(m_ref, NEG_BIG)
        l_ref[...] = jnp.zeros_like(l_ref)
        acc_ref[...] = jnp.zeros_like(acc_ref)
        own = _iota((rows, D), 1) // HEAD_DIM == _iota((rows, D), 0) // n_new
        incl = (_iota((page, page), 0) <= _iota((page, page), 1)).astype(BF)
        t_of_row = rr % n_new
        pad = page - n_new
        s, vn = [], []
        for i in seqs:
            q = q_ref[i].astype(F32)
            qbd_ref[i] = jnp.where(own, jnp.concatenate([q] * n_heads, axis=0), 0.0).astype(BF)
            kn = jnp.concatenate([kn_ref[i], jnp.zeros((pad, D), F32)], axis=0).astype(BF)
            vn.append(jnp.concatenate([vn_ref[i], jnp.zeros((pad, D), F32)], axis=0).astype(BF))
            lfn = jnp.concatenate([lfn_ref[i], jnp.zeros((pad, n_heads), F32)], axis=0)
            pieces = [_dot_nt(expand, piece).astype(BF) for piece in _split3(lfn)]
            cn = sum(_dot(piece, incl) for piece in pieces)
            cn_q = jnp.sum(jnp.where(ll == t_of_row, cn, 0.0), axis=-1, keepdims=True)
            carry_ref[i] = cn_q
            s.append(jnp.where(ll <= t_of_row, _dot_nt(qbd_ref[i], kn) + (cn_q - cn), NEG_BIG))
        online_update(s, lambda i, p: _dot(p, vn[i]))

    @pl.when(j > 0)
    def _():
        n_keys = group * page
        later = (_iota((n_keys, n_keys), 0) > _iota((n_keys, n_keys), 1)).astype(BF)
        of = lambda refs, i: refs[i * group:(i + 1) * group]
        kt = [jnp.concatenate([r[...].reshape(D, page).astype(BF) for r in of(kp_refs, i)], axis=1) for i in seqs]
        vt = [jnp.concatenate([r[...].reshape(D, page).astype(BF) for r in of(vp_refs, i)], axis=1) for i in seqs]
        lft = [jnp.concatenate([r[...] for r in of(lfp_refs, i)], axis=1) for i in seqs]
        pieces = [[_dot(expand, piece).astype(BF) for piece in _split3(x)] for x in lft]
        suffix = [sum(_dot(piece, later) for piece in ps) for ps in pieces]
        total = [sum(jnp.sum(piece.astype(F32), axis=-1, keepdims=True) for piece in ps) for ps in pieces]
        s = [_dot(qbd_ref[i], kt[i]) + (suffix[i] + carry_ref[i]) for i in seqs]
        online_update(s, lambda i, p: _dot_nt(p, vt[i]))
        for i in seqs:
            carry_ref[i] = carry_ref[i] + total[i]

    @pl.when(j == pl.num_programs(1) - 1)
    def _():
        for i in seqs:
            o = acc_ref[i] / l_ref[i]
            tiles = []
            for p in range(D // LANES):
                lo = o[(PAIR * p) * n_new:(PAIR * p + 1) * n_new, p * LANES:(p + 1) * LANES]
                hi = o[(PAIR * p + 1) * n_new:(PAIR * p + 2) * n_new, p * LANES:(p + 1) * LANES]
                tiles.append(jnp.where(_iota((n_new, LANES), 1) < HEAD_DIM, lo, hi))
            o_ref[i] = (jnp.concatenate(tiles, axis=1) * og_ref[i].astype(F32)).astype(BF)


def _fox_decode(q, k_new, v_new, lf_new, og, cache_k, cache_v, cache_lf, page_table):
    DB, n_new, D = q.shape
    H = D // HEAD_DIM
    n_pool, page = cache_k.shape[:2]
    n_pages = page_table.shape[1]
    ck = cache_k.transpose(0, 2, 3, 1)
    cv = cache_v.transpose(0, 2, 3, 1)
    clf = cache_lf.transpose(0, 2, 1)
    group, nb = DECODE_PAGES, DECODE_SEQS
    assert n_pages % group == 0 and DB % nb == 0
    new = lambda n: pl.BlockSpec((nb, n_new, n), lambda b, j, pt: (b, 0, 0))

    def paged(*shape):
        zeros = (0,) * len(shape)
        return [pl.BlockSpec((None,) + shape,
                             lambda b, j, pt, i=i, g=g: (pt[nb * b + i, n_pages - group * jnp.maximum(j, 1) + g],) + zeros)
                for i in range(nb) for g in range(group)]

    rows = H * n_new
    kv_page = (H, HEAD_DIM, page)
    n_in = nb * group
    return pl.pallas_call(
        functools.partial(_fox_decode_kernel, n_new=n_new, group=group, nb=nb),
        grid_spec=pltpu.PrefetchScalarGridSpec(
            num_scalar_prefetch=1,
            grid=(DB // nb, n_pages // group + 1),
            in_specs=[new(D), new(D), new(D), new(H), *paged(*kv_page), *paged(*kv_page), *paged(H, page), new(D)],
            out_specs=new(D),
            scratch_shapes=[pltpu.VMEM((nb, rows, D), BF), pltpu.VMEM((nb, rows, 1), F32),
                            pltpu.VMEM((nb, rows, 1), F32), pltpu.VMEM((nb, rows, D), F32),
                            pltpu.VMEM((nb, rows, 1), F32)],
        ),
        out_shape=jax.ShapeDtypeStruct((DB, n_new, D), BF),
        compiler_params=_params(("arbitrary", "arbitrary")),
        name="fox_decode",
    )(page_table, q, k_new, v_new, lf_new, *[ck] * n_in, *[cv] * n_in, *[clf] * n_in, og)


def _resid_mm_kernel(x_ref, a_ref, w_ref, y_ref):
    y_ref[...] = x_ref[...] + _dot(a_ref[...], w_ref[...])


def _resid_mm(x, a, w):
    B, T, D = x.shape
    K = a.shape[-1]
    tm = _tiles(T)["tm"]
    return pl.pallas_call(
        _resid_mm_kernel,
        grid=(B, T // tm),
        in_specs=[pl.BlockSpec((None, tm, D), lambda b, t: (b, t, 0)),
                  pl.BlockSpec((None, tm, K), lambda b, t: (b, t, 0)),
                  _const_spec((K, D))],
        out_specs=pl.BlockSpec((None, tm, D), lambda b, t: (b, t, 0)),
        out_shape=jax.ShapeDtypeStruct((B, T, D), F32),
        compiler_params=_params(("arbitrary", "arbitrary")),
        name="resid_mm",
    )(x, a, w)


def _rwkv_proj_kernel(x_ref, sh_ref, g_ref, mu_ref, wr_ref, wk_ref, wv_ref, w0_ref, w1_ref, w2_ref,
                      a0_ref, a1_ref, a2_ref, g1_ref, g2_ref, kk_ref, ka_ref, rk_ref, bd_ref,
                      r_o, lw_o, k_o, v_o, kk_o, a_o, g_o, bonus_o, shift_o, *scratch, seg):
    x = x_ref[...]
    tm, D = x.shape
    xn = _rms(x, g_ref[...])
    row = _iota((tm, D), 0)
    if seg is None:
        carry_ref, = scratch

        @pl.when(pl.program_id(1) == 0)
        def _():
            carry_ref[...] = sh_ref[...]

        prev = jnp.where(row == 0, carry_ref[...], pltpu.roll(xn, 1, 0))
        carry_ref[...] = xn[tm - 1:tm, :]
        shift_o[...] = xn[tm - 1:tm, :]
    else:
        prev = jnp.where(row % seg == 0, sh_ref[...], pltpu.roll(xn, 1, 0))
        shift_o[...] = xn.reshape(tm // seg, seg, D)[:, seg - 1, :]
    xx = prev - xn
    xr, xw, xk, xv, xa, xg = ((xn + xx * mu_ref[i:i + 1, :]).astype(BF) for i in range(6))
    r = _dot(xr, wr_ref[...])
    k = _dot(xk, wk_ref[...])
    v = _dot(xv, wv_ref[...])
    lora_w = _dot(jnp.tanh(_dot(xw, w1_ref[...])).astype(BF), w2_ref[...])
    w_log = -jax.nn.softplus(-(w0_ref[...] + lora_w)) - 0.5
    lw_o[...] = -jnp.exp(w_log)
    a = jax.nn.sigmoid(a0_ref[...] + _dot(_dot(xa, a1_ref[...]).astype(BF), a2_ref[...]))
    g_o[...] = _dot(jax.nn.sigmoid(_dot(xg, g1_ref[...])).astype(BF), g2_ref[...]).astype(BF)
    bd = bd_ref[...]
    kk = k * kk_ref[...]
    kk = kk / jnp.maximum(jnp.sqrt(_dot((kk * kk).astype(BF), bd)), 1e-12)
    k = k * (1.0 + (a - 1.0) * ka_ref[...])
    r_o[...] = r
    k_o[...] = k
    v_o[...] = v.astype(BF)
    kk_o[...] = kk
    a_o[...] = a
    bonus_o[...] = _dot((r * k * rk_ref[...]).astype(BF), bd) * v


def _rwkv_proj(x, shift, g, w, bd, *, seg=None):
    B, T, D = x.shape
    tm = _tiles(T)["tm"]
    assert T % tm == 0
    tok = pl.BlockSpec((None, tm, D), lambda b, t: (b, t, 0))
    if seg is None:
        sh_spec = pl.BlockSpec((None, 1, D), lambda b, t: (b, 0, 0))
        shift_shape = jax.ShapeDtypeStruct((B, 1, D), F32)
        shift_spec = pl.BlockSpec((None, 1, D), lambda b, t: (b, 0, 0))
        scratch = [pltpu.VMEM((1, D), F32)]
    else:
        assert B == 1 and tm == T
        sh_spec = pl.BlockSpec((None, T, D), lambda b, t: (0, 0, 0))
        shift_shape = jax.ShapeDtypeStruct((T // seg, D), F32)
        shift_spec = pl.BlockSpec((T // seg, D), lambda b, t: (0, 0))
        scratch = []
    mats = [w[n] for n in ("w_r", "w_k", "w_v")]
    consts = [w["mu"], *mats, w["w0"], w["w1"], w["w2"], w["a0"], w["a1"], w["a2"], w["g1"], w["g2"],
              w["k_k"], w["k_a"], w["r_k"], bd]
    return pl.pallas_call(
        functools.partial(_rwkv_proj_kernel, seg=seg),
        grid=(B, T // tm),
        in_specs=[tok, sh_spec, _const_spec((None, 1, D), (g[1], 0, 0))] + [_const_spec(c.shape) for c in consts],
        out_specs=[tok] * 8 + [shift_spec],
        out_shape=[jax.ShapeDtypeStruct((B, T, D), dt) for dt in (F32, F32, F32, BF, F32, F32, BF, F32)]
        + [shift_shape],
        scratch_shapes=scratch,
        compiler_params=_params(("arbitrary", "arbitrary")),
        name="rwkv_proj",
    )(x, shift, g[0], *consts)


def _rwkv_scan_kernel(r_ref, lw_ref, k_ref, v_ref, kk_ref, a_ref, s0_ref, y_ref, sT_ref, S_ref, *, C):
    nb, _, d_model = r_ref.shape
    n_pairs = d_model // LANES
    D = nb * d_model
    cat = lambda ref: jnp.concatenate([ref[i] for i in range(nb)], axis=1)

    @pl.when(pl.program_id(1) == 0)
    def _():
        S_ref[...] = s0_ref[...].reshape(S_ref.shape)

    lw = cat(lw_ref)
    tri = (_iota((C, C), 1) <= _iota((C, C), 0)).astype(BF)
    cum = _dot_exact_lhs01(tri, lw)
    cend = cum[C - 1:C, :]
    kk = cat(kk_ref)
    kka = kk * cat(a_ref)
    k = cat(k_ref)
    inv_p = jnp.exp(-cum)
    to_end = jnp.exp(cend - cum)
    at = (-(kk * jnp.exp(cum - lw))).astype(BF)
    bt = (kka * inv_p).astype(BF)
    kt = (k * inv_p).astype(BF)
    rt = (cat(r_ref) * jnp.exp(cum)).astype(BF)
    kh = (k * to_end).astype(BF)
    bh = (kka * to_end).astype(BF)
    vb = cat(v_ref)
    p_end = jnp.exp(cend)

    first_half = _iota((C, LANES), 1) < HEAD_DIM
    n2 = PAIR * C
    ri = _iota((n2, n2), 0)
    ci = _iota((n2, n2), 1)
    same = ri // C == ci // C
    strict = same & (ci < ri)
    incl = same & (ci <= ri)
    eye = (ri == ci).astype(F32)
    same_head = _iota((LANES, LANES), 0) // HEAD_DIM == _iota((LANES, LANES), 1) // HEAD_DIM

    def stack(t):
        z = jnp.zeros_like(t)
        return jnp.concatenate([jnp.where(first_half, t, z), jnp.where(first_half, z, t)], axis=0)

    pairs = range(D // LANES)
    lanes = [slice(pr * LANES, (pr + 1) * LANES) for pr in pairs]
    at_s, rt_s, kt_s, bt_s, v_s = ([stack(t[:, ln]) for ln in lanes] for t in (at, rt, kt, bt, vb))
    aa = [_dot_nt(jnp.concatenate([at_s[pr], rt_s[pr]], axis=0), jnp.concatenate([kt_s[pr], bt_s[pr]], axis=0))
          for pr in pairs]
    a_ak = [jnp.where(strict, x[:n2, :n2], 0.0).astype(BF) for x in aa]
    a_ab = [jnp.where(strict, x[:n2, n2:], 0.0) for x in aa]
    a_rk = [jnp.where(incl, x[n2:, :n2], 0.0).astype(BF) for x in aa]
    a_rb = [jnp.where(incl, x[n2:, n2:], 0.0).astype(BF) for x in aa]
    inv = [eye + x for x in a_ab]
    npow = a_ab
    for _ in range(int(math.log2(C)) - 1):
        nbf = [x.astype(BF) for x in npow]
        npow = [_dot(x, x) for x in nbf]
        inv = [i + _dot(i.astype(BF), n.astype(BF)) for i, n in zip(inv, npow)]
    S = [S_ref[pr] for pr in pairs]
    Sb = [x.astype(BF) for x in S]
    rhs = [(_dot_nt(at_s[pr], Sb[pr]) + _dot(a_ak[pr], v_s[pr])).astype(BF) for pr in pairs]
    u = [_dot(inv[pr].astype(BF), rhs[pr]) for pr in pairs]
    y = [_dot_nt(rt_s[pr], Sb[pr]) + _dot(a_rk[pr], v_s[pr]) + _dot(a_rb[pr], u[pr].astype(BF)) for pr in pairs]
    for pr in pairs:
        y_ref[pr // n_pairs, :, lanes[pr % n_pairs]] = y[pr][:C] + y[pr][C:]
        u_pair = (u[pr][:C] + u[pr][C:]).astype(BF)
        dS = _dot_tn(vb[:, lanes[pr]], kh[:, lanes[pr]]) + _dot_tn(u_pair, bh[:, lanes[pr]])
        S_ref[pr] = S[pr] * p_end[:, lanes[pr]] + jnp.where(same_head, dS, 0.0)

    @pl.when(pl.program_id(1) == pl.num_programs(1) - 1)
    def _():
        sT_ref[...] = S_ref[...].reshape(sT_ref.shape)


def _rwkv_scan(r, lw, k, v, kk, a, s0_bd):
    B, T, D = r.shape
    C, nb = RWKV_CHUNK, RWKV_SEQS
    assert T % C == 0 and B % nb == 0
    n_pairs = D // LANES
    tok = pl.BlockSpec((nb, C, D), lambda b, t: (b, t, 0))
    st = pl.BlockSpec((nb, n_pairs, LANES, LANES), lambda b, t: (b, 0, 0, 0))
    return pl.pallas_call(
        functools.partial(_rwkv_scan_kernel, C=C),
        grid=(B // nb, T // C),
        in_specs=[tok] * 6 + [st],
        out_specs=[tok, st],
        out_shape=[jax.ShapeDtypeStruct((B, T, D), F32), jax.ShapeDtypeStruct(s0_bd.shape, F32)],
        scratch_shapes=[pltpu.VMEM((nb * n_pairs, LANES, LANES), F32)],
        compiler_params=_params(("arbitrary", "arbitrary")),
        name="rwkv_scan",
    )(r, lw, k, v, kk, a, s0_bd)


def _state_to_bd(s):
    B, H = s.shape[:2]
    sp = s.reshape(B, H // PAIR, PAIR, HEAD_DIM, HEAD_DIM)
    eye = jnp.eye(PAIR, dtype=s.dtype)
    bd = sp[:, :, :, :, None, :] * eye[None, None, :, None, :, None]
    return bd.reshape(B, H // PAIR, LANES, LANES)


def _bd_to_state(bd):
    B, P = bd.shape[:2]
    t = bd.reshape(B, P, PAIR, HEAD_DIM, PAIR, HEAD_DIM)
    return jnp.stack([t[:, :, h, :, h, :] for h in range(PAIR)], axis=2).reshape(B, P * PAIR, HEAD_DIM, HEAD_DIM)


def _rwkv_post_kernel(x_ref, y_ref, bonus_ref, g_ref, lng_ref, lnb_ref, bd_ref, wo_ref, o_ref):
    y = y_ref[...]
    bd = bd_ref[...]
    d = y - _dot(y.astype(BF), bd) * (1.0 / HEAD_DIM)
    var = _dot((d * d).astype(BF), bd) * (1.0 / HEAD_DIM)
    yn = d * lax.rsqrt(var + LNX_EPS) * lng_ref[...] + lnb_ref[...] + bonus_ref[...]
    o_ref[...] = x_ref[...] + _dot((yn * g_ref[...].astype(F32)).astype(BF), wo_ref[...])


def _rwkv_post(x, y, bonus, g, ln_g, ln_b, bd, w_o):
    B, T, D = x.shape
    tm = _tiles(T)["tm"]
    tok = pl.BlockSpec((None, tm, D), lambda b, t: (b, t, 0))
    return pl.pallas_call(
        _rwkv_post_kernel,
        grid=(B, T // tm),
        in_specs=[tok, tok, tok, tok, _const_spec((1, D)), _const_spec((1, D)), _const_spec((D, D)),
                  _const_spec((D, D))],
        out_specs=tok,
        out_shape=jax.ShapeDtypeStruct((B, T, D), F32),
        compiler_params=_params(("arbitrary", "arbitrary")),
        name="rwkv_post",
    )(x, y, bonus, g, ln_g, ln_b, bd, w_o)


def _pad_cols(w, n):
    return jnp.pad(w, ((0, 0), (0, n - w.shape[1])))


def _pad_rows(w, n):
    return jnp.pad(w, ((0, n - w.shape[0]), (0, 0)))


def kernel(x_prompt, x_sample, cache_k, cache_v, cache_logf, page_table, state_wkv, state_shift, state_conv,
           mix_norm, a_w_in, a_v_norm, a_w_s, a_b_s, a_w_out,
           f_w_qkv, f_q_norm, f_k_norm, f_w_fgate, f_b_fgate, f_w_ogate, f_w_out,
           r_mu, r_w_r, r_w_k, r_w_v, r_w0, r_w1, r_w2, r_a0, r_a1, r_a2, r_g1, r_g2,
           r_k_k, r_k_a, r_r_k, r_lnx_g, r_lnx_b, r_w_o,
           ffn_norm, ffn_w_up, ffn_conv_w, ffn_conv_b, ffn_w_down):
    B, T, D = x_prompt.shape
    DB, n_new, _ = x_sample.shape
    H = D // HEAD_DIM
    depth = ffn_w_up.shape[0]
    F2 = ffn_w_up.shape[-1]
    M = DB * n_new
    row = lambda v: v.reshape(1, -1).astype(F32)

    mix_g = mix_norm.reshape(depth, 1, D)
    ffn_g = ffn_norm.reshape(depth, 1, D)
    ffn_up, ffn_down = ffn_w_up.astype(BF), ffn_w_down.astype(BF)
    ffn_cb = ffn_conv_b.reshape(depth, 1, F2)
    a_in, a_out = a_w_in.astype(BF), a_w_out.astype(BF)
    a_vg = a_v_norm.reshape(a_v_norm.shape[0], 1, -1)
    head_ones = (jnp.arange(D)[:, None] // HEAD_DIM == jnp.arange(D)[None, :] // HEAD_DIM).astype(BF)
    f_all = jnp.concatenate([f_w_qkv, f_w_ogate, _pad_cols(f_w_fgate, LANES)], axis=1).astype(BF)
    f_qg, f_kg = row(jnp.tile(f_q_norm, H)), row(jnp.tile(f_k_norm, H))
    f_bf = _pad_cols(row(f_b_fgate), LANES)
    f_out = f_w_out.astype(BF)
    lora = lambda w1, w2, n: (_pad_cols(w1, n).astype(BF), _pad_rows(w2, n).astype(BF))
    rw = dict(mu=r_mu, w_r=r_w_r.astype(BF), w_k=r_w_k.astype(BF), w_v=r_w_v.astype(BF),
              w0=row(r_w0), a0=row(r_a0), k_k=row(r_k_k), k_a=row(r_k_a), r_k=row(r_r_k))
    rw["w1"], rw["w2"] = lora(r_w1, r_w2, LANES)
    rw["a1"], rw["a2"] = lora(r_a1, r_a2, LANES)
    rw["g1"], rw["g2"] = lora(r_g1, r_g2, 2 * LANES)
    r_out = r_w_o.astype(BF)

    xp = x_prompt
    xs = x_sample.reshape(1, M, D)
    conv_p, conv_s, chunk_v_s = [], [], []
    outs = {}
    for i in range(depth):
        kind = i % 3
        g = (mix_g, i)
        if kind == 0:
            j = i // 3
            a_w = (a_in, a_vg, a_w_s[j], a_b_s[j][:, :, None], a_out)
            xp, = _chunk_mlp(xp, j, g, *a_w, seq=CHUNK, emit_v=False)
            xs, v_rows = _chunk_mlp(xs, j, g, *a_w, seq=n_new, emit_v=True)
            chunk_v_s.append(v_rows.reshape(DB, n_new, -1))
        elif kind == 1:
            f_w = (f_all, f_qg, f_kg, f_bf, head_ones)
            qa, ka, va, kf, vf, og, lf = _fox_proj(xp, g, *f_w, decode=False)
            outs["k_p"], outs["v_p"] = kf.transpose(0, 3, 1, 2), vf.transpose(0, 3, 1, 2)
            outs["logf_p"] = lf.transpose(0, 2, 1)
            xp = _resid_mm(xp, _fox_attn(qa, ka, va, og), f_out)
            q, kf, vf, og, lf = _fox_proj(xs, g, *f_w, decode=True)
            shp = (DB, n_new, D)
            outs["k_s"], outs["v_s"] = kf.reshape(DB, n_new, H, HEAD_DIM), vf.reshape(DB, n_new, H, HEAD_DIM)
            outs["logf_s"] = lf.reshape(DB, n_new, H)
            att = _fox_decode(q.reshape(shp), kf.reshape(shp), vf.reshape(shp), outs["logf_s"], og.reshape(shp),
                              cache_k, cache_v, cache_logf, page_table)
            xs = _resid_mm(xs, att.reshape(1, M, D), f_out)
        else:
            r, lw, k, v, kk, a, gate, bonus, shift = _rwkv_proj(xp, jnp.zeros((B, 1, D), F32), g, rw, head_ones)
            y, s_bd = _rwkv_scan(r, lw, k, v, kk, a, jnp.zeros((B, D // LANES, LANES, LANES), F32))
            outs["wkv_p"], outs["shift_p"] = _bd_to_state(s_bd), shift.reshape(B, D)
            xp = _rwkv_post(xp, y, bonus, gate, row(r_lnx_g), row(r_lnx_b), head_ones, r_out)
            sh = jnp.pad(state_shift[:, None, :], ((0, 0), (0, n_new - 1), (0, 0))).reshape(1, M, D)
            r, lw, k, v, kk, a, gate, bonus, shift = _rwkv_proj(xs, sh, g, rw, head_ones, seg=n_new)
            padded = [jnp.pad(t.reshape(DB, n_new, D), ((0, 0), (0, RWKV_CHUNK - n_new), (0, 0)))
                      for t in (r, lw, k, v, kk, a)]
            y, s_bd = _rwkv_scan(*padded, _state_to_bd(state_wkv.astype(F32)))
            outs["wkv_s"], outs["shift_s"] = _bd_to_state(s_bd), shift
            xs = _rwkv_post(xs, y[:, :n_new].reshape(1, M, D), bonus, gate, row(r_lnx_g), row(r_lnx_b), head_ones,
                            r_out)
        ffn_w = (ffn_g, ffn_up, ffn_conv_w, ffn_cb, ffn_down)
        xp, cp = _conv_ffn(xp, jnp.zeros((B, 2, F2), F32), i, *ffn_w)
        st = state_conv[i]
        e2 = jnp.pad(st, ((0, 0), (0, n_new - 2), (0, 0))).reshape(M, F2)
        e1 = jnp.pad(st[:, 1:2], ((0, 0), (0, n_new - 1), (0, 0))).reshape(M, F2)
        xs, cs = _conv_ffn(xs, jnp.stack([e2, e1]), i, *ffn_w, seg=n_new)
        conv_p.append(cp)
        conv_s.append(cs)
    return (xp, xs.reshape(DB, n_new, D), outs["k_p"], outs["v_p"], outs["logf_p"], outs["wkv_p"], outs["shift_p"],
            jnp.stack(conv_p), outs["k_s"], outs["v_s"], outs["logf_s"], outs["wkv_s"], outs["shift_s"],
            jnp.stack(conv_s), jnp.stack(chunk_v_s))
```

```python
import functools
import math

import jax
import jax.numpy as jnp
import numpy as np
from jax import lax
from jax.experimental import pallas as pl
from jax.experimental.pallas import tpu as pltpu

F32 = jnp.float32
BF = jnp.bfloat16

HEAD_DIM = 64
LANES = 128
SUBLANES = 8
PAIR = LANES // HEAD_DIM
A_GROUPS = 8
CHUNK = 128
RWKV_CHUNK = 64
RWKV_SEQS = 2
NORM_EPS = 1e-6
LNX_EPS = 64e-5
NEG_BIG = -1e30
LOG2E = math.log2(math.e)
N_PIECES = 3
ATTN_HEADS = 4
DECODE_PAGES = 4
DECODE_SEQS = 2
VMEM_LIMIT = 56 * 1024 * 1024


def _tiles(n_tokens):
    tm = min(n_tokens, 512)
    return dict(tm=tm, tq=min(n_tokens, 1024))


def _dot(a, b):
    return jnp.dot(a, b, preferred_element_type=F32)


def _dot_nt(a, b):
    return lax.dot_general(a, b, (((1,), (1,)), ((), ())), preferred_element_type=F32)


def _dot_tn(a, b):
    return lax.dot_general(a, b, (((0,), (0,)), ((), ())), preferred_element_type=F32)


def _split3(x):
    hi = x.astype(BF)
    r1 = x - hi.astype(F32)
    mid = r1.astype(BF)
    lo = (r1 - mid.astype(F32)).astype(BF)
    return hi, mid, lo


def _dot_exact_lhs01(m01, x):
    hi, mid, lo = _split3(x)
    return _dot(m01, hi) + _dot(m01, mid) + _dot(m01, lo)


def _rms(x, g):
    return x * lax.rsqrt(jnp.mean(x * x, axis=-1, keepdims=True) + NORM_EPS) * g


def _iota(shape, axis):
    return lax.broadcasted_iota(jnp.int32, shape, axis)


def _const_spec(shape, index=None):
    nd = len(shape)
    idx = tuple(index) if index is not None else (0,) * nd
    return pl.BlockSpec(shape, lambda *_: idx, pipeline_mode=pl.Buffered(1))


def _params(sem):
    return pltpu.CompilerParams(dimension_semantics=sem, vmem_limit_bytes=VMEM_LIMIT)


def _ffn_kernel(x_ref, st_ref, g_ref, wu_ref, cw_ref, cb_ref, wd_ref, y_ref, ns_ref, hbuf_ref, act_ref, *, seg, fc,
                down_group):
    x = x_ref[...]
    tm = x.shape[0]
    d_ff = wd_ref.shape[0]
    xn = _rms(x, g_ref[...]).astype(BF)
    top = SUBLANES
    prev = slice(top - 2, top)
    if seg is None:
        @pl.when(pl.program_id(1) == 0)
        def _():
            hbuf_ref[prev, :] = st_ref[...]
    else:
        hbuf_ref[prev, :] = jnp.zeros((2, hbuf_ref.shape[1]), F32)
        pos = _iota((tm, fc), 0) % seg
    acc = jnp.zeros(y_ref.shape, F32)
    n_chunks = d_ff // fc
    offsets = lambda c: (c * fc, d_ff + c * fc)
    up = lambda c: [_dot(xn, wu_ref[:, off:off + fc]) for off in offsets(c)]
    h_next = up(0)
    for c in range(n_chunks):
        h_cur, h_next = h_next, (up(c + 1) if c + 1 < n_chunks else None)
        halves = []
        for off, h in zip(offsets(c), h_cur):
            cols = slice(off, off + fc)
            hbuf_ref[top:top + tm, cols] = h
            hm1 = hbuf_ref[top - 1:top - 1 + tm, cols]
            hm2 = hbuf_ref[top - 2:top - 2 + tm, cols]
            if seg is not None:
                hm1 = jnp.where(pos == 0, st_ref[1, :, cols], hm1)
                hm2 = jnp.where(pos < 2, st_ref[0, :, cols], hm2)
                ns_ref[:, :, cols] = h.reshape(tm // seg, seg, fc)[:, seg - 2:, :]
            halves.append(cb_ref[:, cols] + cw_ref[0:1, cols] * hm2 + cw_ref[1:2, cols] * hm1
                          + cw_ref[2:3, cols] * h)
        gate, val = halves
        act_ref[:, c * fc:(c + 1) * fc] = (gate * jax.nn.sigmoid(gate) * val).astype(BF)
        if (c + 1) % down_group == 0 or c + 1 == n_chunks:
            rows = slice((c // down_group) * down_group * fc, (c + 1) * fc)
            acc = acc + _dot(act_ref[:, rows], wd_ref[rows, :])
    y_ref[...] = x + acc
    if seg is None:
        last = hbuf_ref[top + tm - 2:top + tm, :]
        ns_ref[...] = last
        hbuf_ref[prev, :] = last


def _conv_ffn(x, state, layer, g, w_up, conv_w, conv_b, w_down, *, seg=None):
    B, T, D = x.shape
    F2 = w_up.shape[-1]
    tm = _tiles(T)["tm"]
    fc = 256
    assert T % tm == 0 and (F2 // 2) % fc == 0
    if seg is None:
        st_spec = pl.BlockSpec((None, 2, F2), lambda b, t: (b, 0, 0))
        ns_shape = jax.ShapeDtypeStruct((B, 2, F2), F32)
        ns_spec = pl.BlockSpec((None, 2, F2), lambda b, t: (b, 0, 0))
    else:
        assert B == 1 and tm == T and T % seg == 0
        st_spec = pl.BlockSpec((2, T, F2), lambda b, t: (0, 0, 0))
        ns_shape = jax.ShapeDtypeStruct((T // seg, 2, F2), F32)
        ns_spec = pl.BlockSpec((T // seg, 2, F2), lambda b, t: (0, 0, 0))
    return pl.pallas_call(
        functools.partial(_ffn_kernel, seg=seg, fc=fc, down_group=6),
        grid=(B, T // tm),
        in_specs=[
            pl.BlockSpec((None, tm, D), lambda b, t: (b, t, 0)),
            st_spec,
            _const_spec((None, 1, D), (layer, 0, 0)),
            _const_spec((None, D, F2), (layer, 0, 0)),
            _const_spec((None, 3, F2), (layer, 0, 0)),
            _const_spec((None, 1, F2), (layer, 0, 0)),
            _const_spec((None, F2 // 2, D), (layer, 0, 0)),
        ],
        out_specs=[pl.BlockSpec((None, tm, D), lambda b, t: (b, t, 0)), ns_spec],
        out_shape=[jax.ShapeDtypeStruct((B, T, D), F32), ns_shape],
        scratch_shapes=[pltpu.VMEM((SUBLANES + tm, F2), F32), pltpu.VMEM((tm, F2 // 2), BF)],
        compiler_params=_params(("arbitrary", "arbitrary")),
        name="conv_ffn",
    )(x, state, g, w_up, conv_w, conv_b, w_down)


def _cmlp_kernel(x_ref, g_ref, win_ref, vg_ref, ws_ref, bs_ref, wout_ref, y_ref, *rest, seq, chunk):
    *v_out, z_ref = rest
    out_group = A_GROUPS // 2
    x = x_ref[...]
    tm = x.shape[0]
    d_a = wout_ref.shape[0]
    gd = d_a // A_GROUPS
    xn = _rms(x, g_ref[...]).astype(BF)
    span = ws_ref.shape[-1]
    r = _iota((span, span), 0)
    c = _iota((span, span), 1)
    if seq == chunk:
        assert chunk == span
        mix_of = lambda g: jnp.where(c <= r, ws_ref[g], 0.0).astype(BF)
        bias_of = lambda g: bs_ref[g]
    else:
        pick = (_iota((chunk, span), 0) % seq == _iota((chunk, span), 1)).astype(BF)
        same_seq = _iota((chunk, chunk), 0) // seq == _iota((chunk, chunk), 1) // seq

        def mix_of(g):
            corner = jnp.where((c <= r) & (r < seq), ws_ref[g], 0.0).astype(BF)
            return jnp.where(same_seq, _dot_nt(_dot(pick, corner).astype(BF), pick), 0.0).astype(BF)

        bias_of = lambda g: _dot_exact_lhs01(pick, jnp.broadcast_to(bs_ref[g], (span, gd)))
    acc = jnp.zeros(y_ref.shape, F32)
    proj = lambda g: [_dot(xn, win_ref[:, off + g * gd:off + (g + 1) * gd]) for off in (0, d_a)]
    uv_next = proj(0)
    for g in range(A_GROUPS):
        (u, v), uv_next = uv_next, (proj(g + 1) if g + 1 < A_GROUPS else None)
        cols = slice(g * gd, (g + 1) * gd)
        u = jax.nn.gelu(u)
        v = jax.nn.gelu(v)
        v = v * lax.rsqrt(jnp.mean(v * v, axis=-1, keepdims=True) + NORM_EPS) * vg_ref[:, cols]
        if v_out:
            v_out[0][:, cols] = v
        wm = mix_of(g)
        bias = bias_of(g)
        vb = v.astype(BF)
        parts = [_dot(wm, vb[j * chunk:(j + 1) * chunk, :]) + bias for j in range(tm // chunk)]
        mixed = parts[0] if len(parts) == 1 else jnp.concatenate(parts, axis=0)
        z_ref[:, cols] = (u * mixed).astype(BF)
        if (g + 1) % out_group == 0:
            rows = slice((g + 1 - out_group) * gd, (g + 1) * gd)
            acc = acc + _dot(z_ref[:, rows], wout_ref[rows, :])
    y_ref[...] = x + acc


def _chunk_mlp(x, j, g, w_in, v_g, w_s, b_s, w_out, *, seq, emit_v):
    B, T, D = x.shape
    span = w_s.shape[-1]
    d_a = w_out.shape[1]
    tm = _tiles(T)["tm"]
    chunk = span if seq == span else tm
    assert T % tm == 0 and tm % chunk == 0 and chunk % seq == 0
    tok = pl.BlockSpec((None, tm, D), lambda b, t: (b, t, 0))
    out_specs = [tok]
    out_shape = [jax.ShapeDtypeStruct((B, T, D), F32)]
    if emit_v:
        out_specs.append(pl.BlockSpec((None, tm, d_a), lambda b, t: (b, t, 0)))
        out_shape.append(jax.ShapeDtypeStruct((B, T, d_a), F32))
    return pl.pallas_call(
        functools.partial(_cmlp_kernel, seq=seq, chunk=chunk),
        grid=(B, T // tm),
        in_specs=[
            tok,
            _const_spec((None, 1, D), (g[1], 0, 0)),
            _const_spec((None, D, 2 * d_a), (j, 0, 0)),
            _const_spec((None, 1, d_a), (j, 0, 0)),
            _const_spec((A_GROUPS, span, span)),
            _const_spec((A_GROUPS, span, 1)),
            _const_spec((None, d_a, D), (j, 0, 0)),
        ],
        out_specs=out_specs,
        out_shape=out_shape,
        scratch_shapes=[pltpu.VMEM((tm, d_a), BF)],
        compiler_params=_params(("arbitrary", "arbitrary")),
        name="chunk_mlp",
    )(x, g[0], w_in, v_g, w_s, b_s, w_out)


def _aug_placement(n_heads):
    assert N_PIECES * n_heads < LANES
    w = np.zeros((LANES, 2, n_heads, LANES), np.float32)
    one = N_PIECES * n_heads
    for h in range(n_heads):
        for p in range(N_PIECES):
            w[p * n_heads + h, 0, h, HEAD_DIM + p] = 1.0
            w[one, 0, h, HEAD_DIM + N_PIECES + p] = 1.0
            w[one, 1, h, HEAD_DIM + p] = 1.0
            w[p * n_heads + h, 1, h, HEAD_DIM + N_PIECES + p] = -1.0
    return jnp.asarray(w.reshape(LANES, 2 * n_heads * LANES), BF)


def _fox_proj_kernel(x_ref, g_ref, w_ref, qg_ref, kg_ref, bf_ref, bd_ref, *rest, decode):
    x = x_ref[...]
    tm, D = x.shape
    H = D // HEAD_DIM
    xn = _rms(x, g_ref[...]).astype(BF)
    bd = bd_ref[...]

    def head_norm(t, gain):
        ms = _dot((t * t).astype(BF), bd) * (1.0 / HEAD_DIM)
        return t * lax.rsqrt(ms + NORM_EPS) * gain

    q = head_norm(_dot(xn, w_ref[:, 0:D]), qg_ref[...])
    k = head_norm(_dot(xn, w_ref[:, D:2 * D]), kg_ref[...])
    v = _dot(xn, w_ref[:, 2 * D:3 * D])
    og = jax.nn.sigmoid(_dot(xn, w_ref[:, 3 * D:4 * D])).astype(BF)
    lf = jax.nn.log_sigmoid(_dot(xn, w_ref[:, 4 * D:4 * D + LANES]) + bf_ref[...])
    if decode:
        q_ref, kf_ref, vf_ref, og_ref, lf_ref = rest
        q_ref[...] = (q * (HEAD_DIM ** -0.5)).astype(BF)
        kf_ref[...] = k
        vf_ref[...] = v
        og_ref[...] = og
        lf_ref[...] = lf[:, :H]
        return
    place_ref, qa_ref, ka_ref, va_ref, kf_ref, vf_ref, og_ref, lf_ref, carry_ref = rest
    og_ref[...] = og
    kf_ref[...] = k.T.reshape(H, HEAD_DIM, tm)
    vf_ref[...] = v.T.reshape(H, HEAD_DIM, tm)
    lf_ref[...] = lf.T[:H, :]

    @pl.when(pl.program_id(1) == 0)
    def _():
        carry_ref[...] = jnp.zeros_like(carry_ref)

    tri = (_iota((tm, tm), 1) <= _iota((tm, tm), 0)).astype(BF)
    cs = _dot_exact_lhs01(tri, lf) + carry_ref[...]
    carry_ref[...] = cs[tm - 1:tm, :]
    hi, mid, lo = (p.astype(F32) for p in _split3(cs * LOG2E))
    lane = _iota((tm, LANES), 1)
    pieces = jnp.where(lane < H, hi,
                       jnp.where(lane < 2 * H, pltpu.roll(mid, H, 1),
                                 jnp.where(lane < 3 * H, pltpu.roll(lo, 2 * H, 1),
                                           jnp.where(lane == 3 * H, 1.0, 0.0))))
    extras = _dot(pieces.astype(BF), place_ref[...])
    qs = q * (HEAD_DIM ** -0.5 * LOG2E)
    first = lane < HEAD_DIM
    one_lane = jnp.where(lane == HEAD_DIM, 1.0, 0.0)
    for h in range(H):
        pair = slice((h // PAIR) * LANES, (h // PAIR + 1) * LANES)

        def head_tile(t):
            tile = t[:, pair]
            return pltpu.roll(tile, HEAD_DIM, 1) if h % PAIR else tile

        qa_ref[h] = jnp.where(first, head_tile(qs), extras[:, h * LANES:(h + 1) * LANES]).astype(BF)
        ka_ref[h] = jnp.where(first, head_tile(k), extras[:, (H + h) * LANES:(H + h + 1) * LANES]).astype(BF)
        va_ref[h] = jnp.where(first, head_tile(v), one_lane).astype(BF)


def _fox_proj(x, g, w_all, q_g, k_g, b_f, bd, *, decode):
    B, T, D = x.shape
    H = D // HEAD_DIM
    tm = _tiles(T)["tm"]
    assert T % tm == 0
    tok = lambda n: pl.BlockSpec((None, tm, n), lambda b, t: (b, t, 0))
    consts = [w_all, q_g, k_g, b_f, bd]
    if decode:
        out_specs = [tok(D), tok(D), tok(D), tok(D), tok(H)]
        out_shape = [jax.ShapeDtypeStruct((B, T, D), BF), jax.ShapeDtypeStruct((B, T, D), F32),
                     jax.ShapeDtypeStruct((B, T, D), F32), jax.ShapeDtypeStruct((B, T, D), BF),
                     jax.ShapeDtypeStruct((B, T, H), F32)]
        scratch = []
    else:
        consts.append(_aug_placement(H))
        heads = pl.BlockSpec((None, H, tm, LANES), lambda b, t: (b, 0, t, 0))
        kv_t = pl.BlockSpec((None, H, HEAD_DIM, tm), lambda b, t: (b, 0, 0, t))
        out_specs = [heads] * 3 + [kv_t, kv_t, tok(D), pl.BlockSpec((None, H, tm), lambda b, t: (b, 0, t))]
        out_shape = ([jax.ShapeDtypeStruct((B, H, T, LANES), BF)] * 3
                     + [jax.ShapeDtypeStruct((B, H, HEAD_DIM, T), F32)] * 2
                     + [jax.ShapeDtypeStruct((B, T, D), BF), jax.ShapeDtypeStruct((B, H, T), F32)])
        scratch = [pltpu.VMEM((1, LANES), F32)]
    return pl.pallas_call(
        functools.partial(_fox_proj_kernel, decode=decode),
        grid=(B, T // tm),
        in_specs=[tok(D), _const_spec((None, 1, D), (g[1], 0, 0))] + [_const_spec(c.shape) for c in consts],
        out_specs=out_specs,
        out_shape=out_shape,
        scratch_shapes=scratch,
        compiler_params=_params(("arbitrary", "arbitrary")),
        name="fox_proj",
    )(x, g[0], *consts)


def _fox_attn_kernel(qi_ref, ki_ref, q_ref, k_ref, v_ref, og_ref, o_ref, m_ref, acc_ref, *, tq):
    step = pl.program_id(2)
    qi = qi_ref[step]
    ki = ki_ref[step]
    heads = range(q_ref.shape[0])

    @pl.when(ki == 0)
    def _():
        m_ref[...] = jnp.full_like(m_ref, NEG_BIG)
        acc_ref[...] = jnp.zeros_like(acc_ref)

    def update(masked):
        s = [_dot_nt(q_ref[h], k_ref[h]) for h in heads]
        if masked:
            visible = _iota((tq, tq), 0) >= _iota((tq, tq), 1)
            s = [jnp.where(visible, x, NEG_BIG) for x in s]
        m_prev = [m_ref[h] for h in heads]
        m_new = [jnp.maximum(mp, jnp.max(x, axis=-1, keepdims=True)) for mp, x in zip(m_prev, s)]
        p = [jnp.exp2(x - jnp.concatenate([mn] * (tq // LANES), axis=1)).astype(BF) for x, mn in zip(s, m_new)]
        for h in heads:
            acc_ref[h] = jnp.exp2(m_prev[h] - m_new[h]) * acc_ref[h] + _dot(p[h], v_ref[h])
            m_ref[h] = m_new[h]

    @pl.when(ki < qi)
    def _():
        update(False)

    @pl.when(ki == qi)
    def _():
        update(True)
        o = [acc_ref[h] for h in heads]
        o = [x / x[:, HEAD_DIM:HEAD_DIM + 1] for x in o]
        assert PAIR == 2
        first = _iota((tq, LANES), 1) < HEAD_DIM
        pairs = [jnp.where(first, o[h], pltpu.roll(o[h + 1], HEAD_DIM, 1)) for h in heads[::PAIR]]
        o_ref[...] = (jnp.concatenate(pairs, axis=1) * og_ref[...].astype(F32)).astype(BF)


def _fox_attn(qa, ka, va, og):
    B, H, T, _ = qa.shape
    D = og.shape[-1]
    tq = _tiles(T)["tq"]
    assert T % tq == 0 and tq % LANES == 0
    nq = T // tq
    qi_tab = np.asarray([q for q in range(nq) for _ in range(q + 1)], np.int32)
    ki_tab = np.asarray([k for q in range(nq) for k in range(q + 1)], np.int32)
    hs = ATTN_HEADS
    assert H % hs == 0 and hs % PAIR == 0
    qspec = pl.BlockSpec((None, hs, tq, LANES), lambda b, p, s, qt, kt: (b, p, qt[s], 0))
    kspec = pl.BlockSpec((None, hs, tq, LANES), lambda b, p, s, qt, kt: (b, p, kt[s], 0))
    ospec = pl.BlockSpec((None, tq, hs * HEAD_DIM), lambda b, p, s, qt, kt: (b, qt[s], p))
    return pl.pallas_call(
        functools.partial(_fox_attn_kernel, tq=tq),
        grid_spec=pltpu.PrefetchScalarGridSpec(
            num_scalar_prefetch=2,
            grid=(B, H // hs, len(qi_tab)),
            in_specs=[qspec, kspec, kspec, ospec],
            out_specs=ospec,
            scratch_shapes=[pltpu.VMEM((hs, tq, LANES), F32), pltpu.VMEM((hs, tq, LANES), F32)],
        ),
        out_shape=jax.ShapeDtypeStruct((B, T, D), BF),
        compiler_params=_params(("arbitrary",) * 3),
        name="fox_attn",
    )(jnp.asarray(qi_tab), jnp.asarray(ki_tab), qa, ka, va, og)


def _fox_decode_kernel(pt_ref, q_ref, kn_ref, vn_ref, lfn_ref, *rest, n_new, group, nb):
    n_in = nb * group
    kp_refs, vp_refs, lfp_refs = rest[:n_in], rest[n_in:2 * n_in], rest[2 * n_in:3 * n_in]
    og_ref, o_ref, qbd_ref, m_ref, l_ref, acc_ref, carry_ref = rest[3 * n_in:]
    j = pl.program_id(1)
    D = q_ref.shape[-1]
    n_heads = D // HEAD_DIM
    rows = n_heads * n_new
    page = lfp_refs[0].shape[-1]
    assert rows == LANES and page == LANES
    seqs = range(nb)
    rr = _iota((rows, LANES), 0)
    ll = _iota((rows, LANES), 1)
    expand = (_iota((rows, n_heads), 0) // n_new == _iota((rows, n_heads), 1)).astype(BF)

    def online_update(s, pv):
        m_prev = [m_ref[i] for i in seqs]
        m_new = [jnp.maximum(m_prev[i], jnp.max(s[i], axis=-1, keepdims=True)) for i in seqs]
        p = [jnp.exp(s[i] - m_new[i]) for i in seqs]
        for i in seqs:
            alpha = jnp.exp(m_prev[i] - m_new[i])
            l_ref[i] = alpha * l_ref[i] + jnp.sum(p[i], axis=-1, keepdims=True)
            acc_ref[i] = alpha * acc_ref[i] + pv(i, p[i].astype(BF))
            m_ref[i] = m_new[i]

    @pl.when(j == 0)
    def _():
        m_ref[...] = jnp.full_like(m_ref, NEG_BIG)
        l_ref[...] = jnp.zeros_like(l_ref)
        acc_ref[...] = jnp.zeros_like(acc_ref)
        own = _iota((rows, D), 1) // HEAD_DIM == _iota((rows, D), 0) // n_new
        incl = (_iota((page, page), 0) <= _iota((page, page), 1)).astype(BF)
        t_of_row = rr % n_new
        pad = page - n_new
        s, vn = [], []
        for i in seqs:
            q = q_ref[i].astype(F32)
            qbd_ref[i] = jnp.where(own, jnp.concatenate([q] * n_heads, axis=0), 0.0).astype(BF)
            kn = jnp.concatenate([kn_ref[i], jnp.zeros((pad, D), F32)], axis=0).astype(BF)
            vn.append(jnp.concatenate([vn_ref[i], jnp.zeros((pad, D), F32)], axis=0).astype(BF))
            lfn = jnp.concatenate([lfn_ref[i], jnp.zeros((pad, n_heads), F32)], axis=0)
            pieces = [_dot_nt(expand, piece).astype(BF) for piece in _split3(lfn)]
            cn = sum(_dot(piece, incl) for piece in pieces)
            cn_q = jnp.sum(jnp.where(ll == t_of_row, cn, 0.0), axis=-1, keepdims=True)
            carry_ref[i] = cn_q
            s.append(jnp.where(ll <= t_of_row, _dot_nt(qbd_ref[i], kn) + (cn_q - cn), NEG_BIG))
        online_update(s, lambda i, p: _dot(p, vn[i]))

    @pl.when(j > 0)
    def _():
        n_keys = group * page
        later = (_iota((n_keys, n_keys), 0) > _iota((n_keys, n_keys), 1)).astype(BF)
        of = lambda refs, i: refs[i * group:(i + 1) * group]
        kt = [jnp.concatenate([r[...].reshape(D, page).astype(BF) for r in of(kp_refs, i)], axis=1) for i in seqs]
        vt = [jnp.concatenate([r[...].reshape(D, page).astype(BF) for r in of(vp_refs, i)], axis=1) for i in seqs]
        lft = [jnp.concatenate([r[...] for r in of(lfp_refs, i)], axis=1) for i in seqs]
        pieces = [[_dot(expand, piece).astype(BF) for piece in _split3(x)] for x in lft]
        suffix = [sum(_dot(piece, later) for piece in ps) for ps in pieces]
        total = [sum(jnp.sum(piece.astype(F32), axis=-1, keepdims=True) for piece in ps) for ps in pieces]
        s = [_dot(qbd_ref[i], kt[i]) + (suffix[i] + carry_ref[i]) for i in seqs]
        online_update(s, lambda i, p: _dot_nt(p, vt[i]))
        for i in seqs:
            carry_ref[i] = carry_ref[i] + total[i]

    @pl.when(j == pl.num_programs(1) - 1)
    def _():
        for i in seqs:
            o = acc_ref[i] / l_ref[i]
            tiles = []
            for p in range(D // LANES):
                lo = o[(PAIR * p) * n_new:(PAIR * p + 1) * n_new, p * LANES:(p + 1) * LANES]
                hi = o[(PAIR * p + 1) * n_new:(PAIR * p + 2) * n_new, p * LANES:(p + 1) * LANES]
                tiles.append(jnp.where(_iota((n_new, LANES), 1) < HEAD_DIM, lo, hi))
            o_ref[i] = (jnp.concatenate(tiles, axis=1) * og_ref[i].astype(F32)).astype(BF)


def _fox_decode(q, k_new, v_new, lf_new, og, cache_k, cache_v, cache_lf, page_table):
    DB, n_new, D = q.shape
    H = D // HEAD_DIM
    n_pool, page = cache_k.shape[:2]
    n_pages = page_table.shape[1]
    ck = cache_k.transpose(0, 2, 3, 1)
    cv = cache_v.transpose(0, 2, 3, 1)
    clf = cache_lf.transpose(0, 2, 1)
    group, nb = DECODE_PAGES, DECODE_SEQS
    assert n_pages % group == 0 and DB % nb == 0
    new = lambda n: pl.BlockSpec((nb, n_new, n), lambda b, j, pt: (b, 0, 0))

    def paged(*shape):
        zeros = (0,) * len(shape)
        return [pl.BlockSpec((None,) + shape,
                             lambda b, j, pt, i=i, g=g: (pt[nb * b + i, n_pages - group * jnp.maximum(j, 1) + g],) + zeros)
                for i in range(nb) for g in range(group)]

    rows = H * n_new
    kv_page = (H, HEAD_DIM, page)
    n_in = nb * group
    return pl.pallas_call(
        functools.partial(_fox_decode_kernel, n_new=n_new, group=group, nb=nb),
        grid_spec=pltpu.PrefetchScalarGridSpec(
            num_scalar_prefetch=1,
            grid=(DB // nb, n_pages // group + 1),
            in_specs=[new(D), new(D), new(D), new(H), *paged(*kv_page), *paged(*kv_page), *paged(H, page), new(D)],
            out_specs=new(D),
            scratch_shapes=[pltpu.VMEM((nb, rows, D), BF), pltpu.VMEM((nb, rows, 1), F32),
                            pltpu.VMEM((nb, rows, 1), F32), pltpu.VMEM((nb, rows, D), F32),
                            pltpu.VMEM((nb, rows, 1), F32)],
        ),
        out_shape=jax.ShapeDtypeStruct((DB, n_new, D), BF),
        compiler_params=_params(("arbitrary", "arbitrary")),
        name="fox_decode",
    )(page_table, q, k_new, v_new, lf_new, *[ck] * n_in, *[cv] * n_in, *[clf] * n_in, og)


def _resid_mm_kernel(x_ref, a_ref, w_ref, y_ref):
    y_ref[...] = x_ref[...] + _dot(a_ref[...], w_ref[...])


def _resid_mm(x, a, w):
    B, T, D = x.shape
    K = a.shape[-1]
    tm = _tiles(T)["tm"]
    return pl.pallas_call(
        _resid_mm_kernel,
        grid=(B, T // tm),
        in_specs=[pl.BlockSpec((None, tm, D), lambda b, t: (b, t, 0)),
                  pl.BlockSpec((None, tm, K), lambda b, t: (b, t, 0)),
                  _const_spec((K, D))],
        out_specs=pl.BlockSpec((None, tm, D), lambda b, t: (b, t, 0)),
        out_shape=jax.ShapeDtypeStruct((B, T, D), F32),
        compiler_params=_params(("arbitrary", "arbitrary")),
        name="resid_mm",
    )(x, a, w)


def _rwkv_proj_kernel(x_ref, sh_ref, g_ref, mu_ref, wr_ref, wk_ref, wv_ref, w0_ref, w1_ref, w2_ref,
                      a0_ref, a1_ref, a2_ref, g1_ref, g2_ref, kk_ref, ka_ref, rk_ref, bd_ref,
                      r_o, lw_o, k_o, v_o, kk_o, a_o, g_o, bonus_o, shift_o, *scratch, seg):
    x = x_ref[...]
    tm, D = x.shape
    xn = _rms(x, g_ref[...])
    row = _iota((tm, D), 0)
    if seg is None:
        carry_ref, = scratch

        @pl.when(pl.program_id(1) == 0)
        def _():
            carry_ref[...] = sh_ref[...]

        prev = jnp.where(row == 0, carry_ref[...], pltpu.roll(xn, 1, 0))
        carry_ref[...] = xn[tm - 1:tm, :]
        shift_o[...] = xn[tm - 1:tm, :]
    else:
        prev = jnp.where(row % seg == 0, sh_ref[...], pltpu.roll(xn, 1, 0))
        shift_o[...] = xn.reshape(tm // seg, seg, D)[:, seg - 1, :]
    xx = prev - xn
    xr, xw, xk, xv, xa, xg = ((xn + xx * mu_ref[i:i + 1, :]).astype(BF) for i in range(6))
    r = _dot(xr, wr_ref[...])
    k = _dot(xk, wk_ref[...])
    v = _dot(xv, wv_ref[...])
    lora_w = _dot(jnp.tanh(_dot(xw, w1_ref[...])).astype(BF), w2_ref[...])
    w_log = -jax.nn.softplus(-(w0_ref[...] + lora_w)) - 0.5
    lw_o[...] = -jnp.exp(w_log)
    a = jax.nn.sigmoid(a0_ref[...] + _dot(_dot(xa, a1_ref[...]).astype(BF), a2_ref[...]))
    g_o[...] = _dot(jax.nn.sigmoid(_dot(xg, g1_ref[...])).astype(BF), g2_ref[...]).astype(BF)
    bd = bd_ref[...]
    kk = k * kk_ref[...]
    kk = kk / jnp.maximum(jnp.sqrt(_dot((kk * kk).astype(BF), bd)), 1e-12)
    k = k * (1.0 + (a - 1.0) * ka_ref[...])
    r_o[...] = r
    k_o[...] = k
    v_o[...] = v.astype(BF)
    kk_o[...] = kk
    a_o[...] = a
    bonus_o[...] = _dot((r * k * rk_ref[...]).astype(BF), bd) * v


def _rwkv_proj(x, shift, g, w, bd, *, seg=None):
    B, T, D = x.shape
    tm = _tiles(T)["tm"]
    assert T % tm == 0
    tok = pl.BlockSpec((None, tm, D), lambda b, t: (b, t, 0))
    if seg is None:
        sh_spec = pl.BlockSpec((None, 1, D), lambda b, t: (b, 0, 0))
        shift_shape = jax.ShapeDtypeStruct((B, 1, D), F32)
        shift_spec = pl.BlockSpec((None, 1, D), lambda b, t: (b, 0, 0))
        scratch = [pltpu.VMEM((1, D), F32)]
    else:
        assert B == 1 and tm == T
        sh_spec = pl.BlockSpec((None, T, D), lambda b, t: (0, 0, 0))
        shift_shape = jax.ShapeDtypeStruct((T // seg, D), F32)
        shift_spec = pl.BlockSpec((T // seg, D), lambda b, t: (0, 0))
        scratch = []
    mats = [w[n] for n in ("w_r", "w_k", "w_v")]
    consts = [w["mu"], *mats, w["w0"], w["w1"], w["w2"], w["a0"], w["a1"], w["a2"], w["g1"], w["g2"],
              w["k_k"], w["k_a"], w["r_k"], bd]
    return pl.pallas_call(
        functools.partial(_rwkv_proj_kernel, seg=seg),
        grid=(B, T // tm),
        in_specs=[tok, sh_spec, _const_spec((None, 1, D), (g[1], 0, 0))] + [_const_spec(c.shape) for c in consts],
        out_specs=[tok] * 8 + [shift_spec],
        out_shape=[jax.ShapeDtypeStruct((B, T, D), dt) for dt in (F32, F32, F32, BF, F32, F32, BF, F32)]
        + [shift_shape],
        scratch_shapes=scratch,
        compiler_params=_params(("arbitrary", "arbitrary")),
        name="rwkv_proj",
    )(x, shift, g[0], *consts)


def _rwkv_scan_kernel(r_ref, lw_ref, k_ref, v_ref, kk_ref, a_ref, s0_ref, y_ref, sT_ref, S_ref, *, C):
    nb, n_rows, d_model = r_ref.shape
    n_pairs = d_model // LANES
    D = nb * d_model

    def cat(ref):
        x = jnp.concatenate([ref[i] for i in range(nb)], axis=1)
        if n_rows < C:
            x = jnp.concatenate([x.astype(F32), jnp.zeros((C - n_rows, D), F32)], axis=0).astype(x.dtype)
        return x

    @pl.when(pl.program_id(1) == 0)
    def _():
        zero = jnp.zeros((HEAD_DIM, HEAD_DIM), F32)
        for i in range(nb):
            for pr in range(n_pairs):
                top = jnp.concatenate([s0_ref[i, PAIR * pr], zero], axis=1)
                bottom = jnp.concatenate([zero, s0_ref[i, PAIR * pr + 1]], axis=1)
                S_ref[i * n_pairs + pr] = jnp.concatenate([top, bottom], axis=0)

    lw = cat(lw_ref)
    tri = (_iota((C, C), 1) <= _iota((C, C), 0)).astype(BF)
    cum = _dot_exact_lhs01(tri, lw)
    cend = cum[C - 1:C, :]
    kk = cat(kk_ref)
    kka = kk * cat(a_ref)
    k = cat(k_ref)
    inv_p = jnp.exp(-cum)
    to_end = jnp.exp(cend - cum)
    at = (-(kk * jnp.exp(cum - lw))).astype(BF)
    bt = (kka * inv_p).astype(BF)
    kt = (k * inv_p).astype(BF)
    rt = (cat(r_ref) * jnp.exp(cum)).astype(BF)
    kh = (k * to_end).astype(BF)
    bh = (kka * to_end).astype(BF)
    vb = cat(v_ref)
    p_end = jnp.exp(cend)

    first_half = _iota((C, LANES), 1) < HEAD_DIM
    n2 = PAIR * C
    ri = _iota((n2, n2), 0)
    ci = _iota((n2, n2), 1)
    same = ri // C == ci // C
    strict = same & (ci < ri)
    incl = same & (ci <= ri)
    eye = (ri == ci).astype(F32)
    same_head = _iota((LANES, LANES), 0) // HEAD_DIM == _iota((LANES, LANES), 1) // HEAD_DIM

    def stack(t):
        z = jnp.zeros_like(t)
        return jnp.concatenate([jnp.where(first_half, t, z), jnp.where(first_half, z, t)], axis=0)

    pairs = range(D // LANES)
    lanes = [slice(pr * LANES, (pr + 1) * LANES) for pr in pairs]
    at_s, rt_s, kt_s, bt_s, v_s = ([stack(t[:, ln]) for ln in lanes] for t in (at, rt, kt, bt, vb))
    aa = [_dot_nt(jnp.concatenate([at_s[pr], rt_s[pr]], axis=0), jnp.concatenate([kt_s[pr], bt_s[pr]], axis=0))
          for pr in pairs]
    a_ak = [jnp.where(strict, x[:n2, :n2], 0.0).astype(BF) for x in aa]
    a_ab = [jnp.where(strict, x[:n2, n2:], 0.0) for x in aa]
    a_rk = [jnp.where(incl, x[n2:, :n2], 0.0).astype(BF) for x in aa]
    a_rb = [jnp.where(incl, x[n2:, n2:], 0.0).astype(BF) for x in aa]
    inv = [eye + x for x in a_ab]
    npow = a_ab
    for _ in range(int(math.log2(C)) - 1):
        nbf = [x.astype(BF) for x in npow]
        npow = [_dot(x, x) for x in nbf]
        inv = [i + _dot(i.astype(BF), n.astype(BF)) for i, n in zip(inv, npow)]
    S = [S_ref[pr] for pr in pairs]
    Sb = [x.astype(BF) for x in S]
    rhs = [(_dot_nt(at_s[pr], Sb[pr]) + _dot(a_ak[pr], v_s[pr])).astype(BF) for pr in pairs]
    u = [_dot(inv[pr].astype(BF), rhs[pr]) for pr in pairs]
    y = [_dot_nt(rt_s[pr], Sb[pr]) + _dot(a_rk[pr], v_s[pr]) + _dot(a_rb[pr], u[pr].astype(BF)) for pr in pairs]
    for pr in pairs:
        y_ref[pr // n_pairs, :, lanes[pr % n_pairs]] = (y[pr][:C] + y[pr][C:])[:n_rows]
        u_pair = (u[pr][:C] + u[pr][C:]).astype(BF)
        dS = _dot_tn(vb[:, lanes[pr]], kh[:, lanes[pr]]) + _dot_tn(u_pair, bh[:, lanes[pr]])
        S_ref[pr] = S[pr] * p_end[:, lanes[pr]] + jnp.where(same_head, dS, 0.0)

    @pl.when(pl.program_id(1) == pl.num_programs(1) - 1)
    def _():
        for pr in pairs:
            i, h = pr // n_pairs, PAIR * (pr % n_pairs)
            sT_ref[i, h] = S_ref[pr][:HEAD_DIM, :HEAD_DIM]
            sT_ref[i, h + 1] = S_ref[pr][HEAD_DIM:, HEAD_DIM:]


def _rwkv_scan(r, lw, k, v, kk, a, s0):
    B, T, D = r.shape
    C, nb = RWKV_CHUNK, RWKV_SEQS
    rows = min(T, C)
    assert T % rows == 0 and B % nb == 0
    n_pairs = D // LANES
    tok = pl.BlockSpec((nb, rows, D), lambda b, t: (b, t, 0))
    st = pl.BlockSpec((nb, D // HEAD_DIM, HEAD_DIM, HEAD_DIM), lambda b, t: (b, 0, 0, 0))
    return pl.pallas_call(
        functools.partial(_rwkv_scan_kernel, C=C),
        grid=(B // nb, T // rows),
        in_specs=[tok] * 6 + [st],
        out_specs=[tok, st],
        out_shape=[jax.ShapeDtypeStruct((B, T, D), F32), jax.ShapeDtypeStruct(s0.shape, F32)],
        scratch_shapes=[pltpu.VMEM((nb * n_pairs, LANES, LANES), F32)],
        compiler_params=_params(("arbitrary", "arbitrary")),
        name="rwkv_scan",
    )(r, lw, k, v, kk, a, s0)


def _rwkv_post_kernel(x_ref, y_ref, bonus_ref, g_ref, lng_ref, lnb_ref, bd_ref, wo_ref, o_ref):
    y = y_ref[...]
    bd = bd_ref[...]
    d = y - _dot(y.astype(BF), bd) * (1.0 / HEAD_DIM)
    var = _dot((d * d).astype(BF), bd) * (1.0 / HEAD_DIM)
    yn = d * lax.rsqrt(var + LNX_EPS) * lng_ref[...] + lnb_ref[...] + bonus_ref[...]
    o_ref[...] = x_ref[...] + _dot((yn * g_ref[...].astype(F32)).astype(BF), wo_ref[...])


def _rwkv_post(x, y, bonus, g, ln_g, ln_b, bd, w_o):
    B, T, D = x.shape
    tm = _tiles(T)["tm"]
    tok = pl.BlockSpec((None, tm, D), lambda b, t: (b, t, 0))
    return pl.pallas_call(
        _rwkv_post_kernel,
        grid=(B, T // tm),
        in_specs=[tok, tok, tok, tok, _const_spec((1, D)), _const_spec((1, D)), _const_spec((D, D)),
                  _const_spec((D, D))],
        out_specs=tok,
        out_shape=jax.ShapeDtypeStruct((B, T, D), F32),
        compiler_params=_params(("arbitrary", "arbitrary")),
        name="rwkv_post",
    )(x, y, bonus, g, ln_g, ln_b, bd, w_o)


def _pad_cols(w, n):
    return jnp.pad(w, ((0, 0), (0, n - w.shape[1])))


def _pad_rows(w, n):
    return jnp.pad(w, ((0, n - w.shape[0]), (0, 0)))


def kernel(x_prompt, x_sample, cache_k, cache_v, cache_logf, page_table, state_wkv, state_shift, state_conv,
           mix_norm, a_w_in, a_v_norm, a_w_s, a_b_s, a_w_out,
           f_w_qkv, f_q_norm, f_k_norm, f_w_fgate, f_b_fgate, f_w_ogate, f_w_out,
           r_mu, r_w_r, r_w_k, r_w_v, r_w0, r_w1, r_w2, r_a0, r_a1, r_a2, r_g1, r_g2,
           r_k_k, r_k_a, r_r_k, r_lnx_g, r_lnx_b, r_w_o,
           ffn_norm, ffn_w_up, ffn_conv_w, ffn_conv_b, ffn_w_down):
    B, T, D = x_prompt.shape
    DB, n_new, _ = x_sample.shape
    H = D // HEAD_DIM
    depth = ffn_w_up.shape[0]
    F2 = ffn_w_up.shape[-1]
    M = DB * n_new
    row = lambda v: v.reshape(1, -1).astype(F32)

    mix_g = mix_norm.reshape(depth, 1, D)
    ffn_g = ffn_norm.reshape(depth, 1, D)
    ffn_up, ffn_down = ffn_w_up.astype(BF), ffn_w_down.astype(BF)
    ffn_cb = ffn_conv_b.reshape(depth, 1, F2)
    a_in, a_out = a_w_in.astype(BF), a_w_out.astype(BF)
    a_vg = a_v_norm.reshape(a_v_norm.shape[0], 1, -1)
    head_ones = (jnp.arange(D)[:, None] // HEAD_DIM == jnp.arange(D)[None, :] // HEAD_DIM).astype(BF)
    f_all = jnp.concatenate([f_w_qkv, f_w_ogate, _pad_cols(f_w_fgate, LANES)], axis=1).astype(BF)
    f_qg, f_kg = row(jnp.tile(f_q_norm, H)), row(jnp.tile(f_k_norm, H))
    f_bf = _pad_cols(row(f_b_fgate), LANES)
    f_out = f_w_out.astype(BF)
    lora = lambda w1, w2, n: (_pad_cols(w1, n).astype(BF), _pad_rows(w2, n).astype(BF))
    rw = dict(mu=r_mu, w_r=r_w_r.astype(BF), w_k=r_w_k.astype(BF), w_v=r_w_v.astype(BF),
              w0=row(r_w0), a0=row(r_a0), k_k=row(r_k_k), k_a=row(r_k_a), r_k=row(r_r_k))
    rw["w1"], rw["w2"] = lora(r_w1, r_w2, LANES)
    rw["a1"], rw["a2"] = lora(r_a1, r_a2, LANES)
    rw["g1"], rw["g2"] = lora(r_g1, r_g2, 2 * LANES)
    r_out = r_w_o.astype(BF)

    xp = x_prompt
    xs = x_sample.reshape(1, M, D)
    conv_p, conv_s, chunk_v_s = [], [], []
    outs = {}
    for i in range(depth):
        kind = i % 3
        g = (mix_g, i)
        if kind == 0:
            j = i // 3
            a_w = (a_in, a_vg, a_w_s[j], a_b_s[j][:, :, None], a_out)
            xp, = _chunk_mlp(xp, j, g, *a_w, seq=CHUNK, emit_v=False)
            xs, v_rows = _chunk_mlp(xs, j, g, *a_w, seq=n_new, emit_v=True)
            chunk_v_s.append(v_rows.reshape(DB, n_new, -1))
        elif kind == 1:
            f_w = (f_all, f_qg, f_kg, f_bf, head_ones)
            qa, ka, va, kf, vf, og, lf = _fox_proj(xp, g, *f_w, decode=False)
            outs["k_p"], outs["v_p"] = kf.transpose(0, 3, 1, 2), vf.transpose(0, 3, 1, 2)
            outs["logf_p"] = lf.transpose(0, 2, 1)
            xp = _resid_mm(xp, _fox_attn(qa, ka, va, og), f_out)
            q, kf, vf, og, lf = _fox_proj(xs, g, *f_w, decode=True)
            shp = (DB, n_new, D)
            outs["k_s"], outs["v_s"] = kf.reshape(DB, n_new, H, HEAD_DIM), vf.reshape(DB, n_new, H, HEAD_DIM)
            outs["logf_s"] = lf.reshape(DB, n_new, H)
            att = _fox_decode(q.reshape(shp), kf.reshape(shp), vf.reshape(shp), outs["logf_s"], og.reshape(shp),
                              cache_k, cache_v, cache_logf, page_table)
            xs = _resid_mm(xs, att.reshape(1, M, D), f_out)
        else:
            r, lw, k, v, kk, a, gate, bonus, shift = _rwkv_proj(xp, jnp.zeros((B, 1, D), F32), g, rw, head_ones)
            y, outs["wkv_p"] = _rwkv_scan(r, lw, k, v, kk, a, jnp.zeros((B, H, HEAD_DIM, HEAD_DIM), F32))
            outs["shift_p"] = shift.reshape(B, D)
            xp = _rwkv_post(xp, y, bonus, gate, row(r_lnx_g), row(r_lnx_b), head_ones, r_out)
            sh = jnp.pad(state_shift[:, None, :], ((0, 0), (0, n_new - 1), (0, 0))).reshape(1, M, D)
            r, lw, k, v, kk, a, gate, bonus, shift = _rwkv_proj(xs, sh, g, rw, head_ones, seg=n_new)
            seqs = [t.reshape(DB, n_new, D) for t in (r, lw, k, v, kk, a)]
            y, outs["wkv_s"] = _rwkv_scan(*seqs, state_wkv)
            outs["shift_s"] = shift
            xs = _rwkv_post(xs, y.reshape(1, M, D), bonus, gate, row(r_lnx_g), row(r_lnx_b), head_ones,
                            r_out)
        ffn_w = (ffn_g, ffn_up, ffn_conv_w, ffn_cb, ffn_down)
        xp, cp = _conv_ffn(xp, jnp.zeros((B, 2, F2), F32), i, *ffn_w)
        st = state_conv[i]
        e2 = jnp.pad(st, ((0, 0), (0, n_new - 2), (0, 0))).reshape(M, F2)
        e1 = jnp.pad(st[:, 1:2], ((0, 0), (0, n_new - 1), (0, 0))).reshape(M, F2)
        xs, cs = _conv_ffn(xs, jnp.stack([e2, e1]), i, *ffn_w, seg=n_new)
        conv_p.append(cp)
        conv_s.append(cs)
    return (xp, xs.reshape(DB, n_new, D), outs["k_p"], outs["v_p"], outs["logf_p"], outs["wkv_p"], outs["shift_p"],
            jnp.stack(conv_p), outs["k_s"], outs["v_s"], outs["logf_s"], outs["wkv_s"], outs["shift_s"],
            jnp.stack(conv_s), jnp.stack(chunk_v_s))
```

```python
import functools
import math

import jax
import jax.numpy as jnp
import numpy as np
from jax import lax
from jax.experimental import pallas as pl
from jax.experimental.pallas import tpu as pltpu

F32 = jnp.float32
BF = jnp.bfloat16

HEAD_DIM = 64
LANES = 128
SUBLANES = 8
PAIR = LANES // HEAD_DIM
A_GROUPS = 8
CHUNK = 128
RWKV_CHUNK = 64
RWKV_SEQS = 2
NORM_EPS = 1e-6
LNX_EPS = 64e-5
NEG_BIG = -1e30
LOG2E = math.log2(math.e)
N_PIECES = 3
ATTN_HEADS = 4
DECODE_PAGES = 4
DECODE_SEQS = 2
VMEM_LIMIT = 56 * 1024 * 1024


def _tiles(n_tokens):
    tm = min(n_tokens, 512)
    return dict(tm=tm, tq=min(n_tokens, 1024))


def _dot(a, b):
    return jnp.dot(a, b, preferred_element_type=F32)


def _dot_nt(a, b):
    return lax.dot_general(a, b, (((1,), (1,)), ((), ())), preferred_element_type=F32)


def _dot_tn(a, b):
    return lax.dot_general(a, b, (((0,), (0,)), ((), ())), preferred_element_type=F32)


def _split3(x):
    hi = x.astype(BF)
    r1 = x - hi.astype(F32)
    mid = r1.astype(BF)
    lo = (r1 - mid.astype(F32)).astype(BF)
    return hi, mid, lo


def _dot_exact_lhs01(m01, x):
    hi, mid, lo = _split3(x)
    return _dot(m01, hi) + _dot(m01, mid) + _dot(m01, lo)


def _rms(x, g):
    return x * lax.rsqrt(jnp.mean(x * x, axis=-1, keepdims=True) + NORM_EPS) * g


def _iota(shape, axis):
    return lax.broadcasted_iota(jnp.int32, shape, axis)


def _const_spec(shape, index=None):
    nd = len(shape)
    idx = tuple(index) if index is not None else (0,) * nd
    return pl.BlockSpec(shape, lambda *_: idx, pipeline_mode=pl.Buffered(1))


def _params(sem):
    return pltpu.CompilerParams(dimension_semantics=sem, vmem_limit_bytes=VMEM_LIMIT)


def _ffn_kernel(x_ref, st_ref, g_ref, wu_ref, cw_ref, cb_ref, wd_ref, y_ref, ns_ref, hbuf_ref, act_ref, *, seg, fc,
                down_group):
    x = x_ref[...]
    tm = x.shape[0]
    d_ff = wd_ref.shape[0]
    xn = _rms(x, g_ref[...]).astype(BF)
    top = SUBLANES
    prev = slice(top - 2, top)
    if seg is None:
        @pl.when(pl.program_id(1) == 0)
        def _():
            hbuf_ref[prev, :] = st_ref[...]
    else:
        hbuf_ref[prev, :] = jnp.zeros((2, hbuf_ref.shape[1]), F32)
        pos = _iota((tm, fc), 0) % seg
    acc = jnp.zeros(y_ref.shape, F32)
    n_chunks = d_ff // fc
    offsets = lambda c: (c * fc, d_ff + c * fc)
    up = lambda c: [_dot(xn, wu_ref[:, off:off + fc]) for off in offsets(c)]
    h_next = up(0)
    for c in range(n_chunks):
        h_cur, h_next = h_next, (up(c + 1) if c + 1 < n_chunks else None)
        halves = []
        for off, h in zip(offsets(c), h_cur):
            cols = slice(off, off + fc)
            hbuf_ref[top:top + tm, cols] = h
            hm1 = hbuf_ref[top - 1:top - 1 + tm, cols]
            hm2 = hbuf_ref[top - 2:top - 2 + tm, cols]
            if seg is not None:
                hm1 = jnp.where(pos == 0, st_ref[1, :, cols], hm1)
                hm2 = jnp.where(pos < 2, st_ref[0, :, cols], hm2)
                ns_ref[:, :, cols] = h.reshape(tm // seg, seg, fc)[:, seg - 2:, :]
            halves.append(cb_ref[:, cols] + cw_ref[0:1, cols] * hm2 + cw_ref[1:2, cols] * hm1
                          + cw_ref[2:3, cols] * h)
        gate, val = halves
        act_ref[:, c * fc:(c + 1) * fc] = (gate * jax.nn.sigmoid(gate) * val).astype(BF)
        if (c + 1) % down_group == 0 or c + 1 == n_chunks:
            rows = slice((c // down_group) * down_group * fc, (c + 1) * fc)
            acc = acc + _dot(act_ref[:, rows], wd_ref[rows, :])
    y_ref[...] = x + acc
    if seg is None:
        last = hbuf_ref[top + tm - 2:top + tm, :]
        ns_ref[...] = last
        hbuf_ref[prev, :] = last


def _conv_ffn(x, state, layer, g, w_up, conv_w, conv_b, w_down, *, seg=None):
    B, T, D = x.shape
    F2 = w_up.shape[-1]
    tm = _tiles(T)["tm"]
    fc = 256
    assert T % tm == 0 and (F2 // 2) % fc == 0
    if seg is None:
        st_spec = pl.BlockSpec((None, 2, F2), lambda b, t: (b, 0, 0))
        ns_shape = jax.ShapeDtypeStruct((B, 2, F2), F32)
        ns_spec = pl.BlockSpec((None, 2, F2), lambda b, t: (b, 0, 0))
    else:
        assert B == 1 and tm == T and T % seg == 0
        st_spec = pl.BlockSpec((2, T, F2), lambda b, t: (0, 0, 0))
        ns_shape = jax.ShapeDtypeStruct((T // seg, 2, F2), F32)
        ns_spec = pl.BlockSpec((T // seg, 2, F2), lambda b, t: (0, 0, 0))
    return pl.pallas_call(
        functools.partial(_ffn_kernel, seg=seg, fc=fc, down_group=6),
        grid=(B, T // tm),
        in_specs=[
            pl.BlockSpec((None, tm, D), lambda b, t: (b, t, 0)),
            st_spec,
            _const_spec((None, 1, D), (layer, 0, 0)),
            _const_spec((None, D, F2), (layer, 0, 0)),
            _const_spec((None, 3, F2), (layer, 0, 0)),
            _const_spec((None, 1, F2), (layer, 0, 0)),
            _const_spec((None, F2 // 2, D), (layer, 0, 0)),
        ],
        out_specs=[pl.BlockSpec((None, tm, D), lambda b, t: (b, t, 0)), ns_spec],
        out_shape=[jax.ShapeDtypeStruct((B, T, D), F32), ns_shape],
        scratch_shapes=[pltpu.VMEM((SUBLANES + tm, F2), F32), pltpu.VMEM((tm, F2 // 2), BF)],
        compiler_params=_params(("arbitrary", "arbitrary")),
        name="conv_ffn",
    )(x, state, g, w_up, conv_w, conv_b, w_down)


def _cmlp_kernel(x_ref, g_ref, win_ref, vg_ref, ws_ref, bs_ref, wout_ref, y_ref, *rest, seq, chunk):
    *v_out, z_ref = rest
    out_group = A_GROUPS // 2
    x = x_ref[...]
    tm = x.shape[0]
    d_a = wout_ref.shape[0]
    gd = d_a // A_GROUPS
    xn = _rms(x, g_ref[...]).astype(BF)
    span = ws_ref.shape[-1]
    r = _iota((span, span), 0)
    c = _iota((span, span), 1)
    if seq == chunk:
        assert chunk == span
        mix_of = lambda g: jnp.where(c <= r, ws_ref[g], 0.0).astype(BF)
        bias_of = lambda g: bs_ref[g]
    else:
        pick = (_iota((chunk, span), 0) % seq == _iota((chunk, span), 1)).astype(BF)
        same_seq = _iota((chunk, chunk), 0) // seq == _iota((chunk, chunk), 1) // seq

        def mix_of(g):
            corner = jnp.where((c <= r) & (r < seq), ws_ref[g], 0.0).astype(BF)
            return jnp.where(same_seq, _dot_nt(_dot(pick, corner).astype(BF), pick), 0.0).astype(BF)

        bias_of = lambda g: _dot_exact_lhs01(pick, jnp.broadcast_to(bs_ref[g], (span, gd)))
    acc = jnp.zeros(y_ref.shape, F32)
    proj = lambda g: [_dot(xn, win_ref[:, off + g * gd:off + (g + 1) * gd]) for off in (0, d_a)]
    uv_next = proj(0)
    for g in range(A_GROUPS):
        (u, v), uv_next = uv_next, (proj(g + 1) if g + 1 < A_GROUPS else None)
        cols = slice(g * gd, (g + 1) * gd)
        u = jax.nn.gelu(u)
        v = jax.nn.gelu(v)
        v = v * lax.rsqrt(jnp.mean(v * v, axis=-1, keepdims=True) + NORM_EPS) * vg_ref[:, cols]
        if v_out:
            v_out[0][:, cols] = v
        wm = mix_of(g)
        bias = bias_of(g)
        vb = v.astype(BF)
        parts = [_dot(wm, vb[j * chunk:(j + 1) * chunk, :]) + bias for j in range(tm // chunk)]
        mixed = parts[0] if len(parts) == 1 else jnp.concatenate(parts, axis=0)
        z_ref[:, cols] = (u * mixed).astype(BF)
        if (g + 1) % out_group == 0:
            rows = slice((g + 1 - out_group) * gd, (g + 1) * gd)
            acc = acc + _dot(z_ref[:, rows], wout_ref[rows, :])
    y_ref[...] = x + acc


def _chunk_mlp(x, j, g, w_in, v_g, w_s, b_s, w_out, *, seq, emit_v):
    B, T, D = x.shape
    span = w_s.shape[-1]
    d_a = w_out.shape[1]
    tm = _tiles(T)["tm"]
    chunk = span if seq == span else tm
    assert T % tm == 0 and tm % chunk == 0 and chunk % seq == 0
    tok = pl.BlockSpec((None, tm, D), lambda b, t: (b, t, 0))
    out_specs = [tok]
    out_shape = [jax.ShapeDtypeStruct((B, T, D), F32)]
    if emit_v:
        out_specs.append(pl.BlockSpec((None, tm, d_a), lambda b, t: (b, t, 0)))
        out_shape.append(jax.ShapeDtypeStruct((B, T, d_a), F32))
    return pl.pallas_call(
        functools.partial(_cmlp_kernel, seq=seq, chunk=chunk),
        grid=(B, T // tm),
        in_specs=[
            tok,
            _const_spec((None, 1, D), (g[1], 0, 0)),
            _const_spec((None, D, 2 * d_a), (j, 0, 0)),
            _const_spec((None, 1, d_a), (j, 0, 0)),
            _const_spec((A_GROUPS, span, span)),
            _const_spec((A_GROUPS, span, 1)),
            _const_spec((None, d_a, D), (j, 0, 0)),
        ],
        out_specs=out_specs,
        out_shape=out_shape,
        scratch_shapes=[pltpu.VMEM((tm, d_a), BF)],
        compiler_params=_params(("arbitrary", "arbitrary")),
        name="chunk_mlp",
    )(x, g[0], w_in, v_g, w_s, b_s, w_out)


def _aug_placement(n_heads):
    assert N_PIECES * n_heads < LANES
    w = np.zeros((LANES, 2, n_heads, LANES), np.float32)
    one = N_PIECES * n_heads
    for h in range(n_heads):
        for p in range(N_PIECES):
            w[p * n_heads + h, 0, h, HEAD_DIM + p] = 1.0
            w[one, 0, h, HEAD_DIM + N_PIECES + p] = 1.0
            w[one, 1, h, HEAD_DIM + p] = 1.0
            w[p * n_heads + h, 1, h, HEAD_DIM + N_PIECES + p] = -1.0
    return jnp.asarray(w.reshape(LANES, 2 * n_heads * LANES), BF)


def _fox_proj_kernel(x_ref, g_ref, w_ref, qg_ref, kg_ref, bf_ref, bd_ref, *rest, decode):
    x = x_ref[...]
    tm, D = x.shape
    H = D // HEAD_DIM
    xn = _rms(x, g_ref[...]).astype(BF)
    bd = bd_ref[...]

    def head_norm(t, gain):
        ms = _dot((t * t).astype(BF), bd) * (1.0 / HEAD_DIM)
        return t * lax.rsqrt(ms + NORM_EPS) * gain

    q = head_norm(_dot(xn, w_ref[:, 0:D]), qg_ref[...])
    k = head_norm(_dot(xn, w_ref[:, D:2 * D]), kg_ref[...])
    v = _dot(xn, w_ref[:, 2 * D:3 * D])
    og = jax.nn.sigmoid(_dot(xn, w_ref[:, 3 * D:4 * D])).astype(BF)
    lf = jax.nn.log_sigmoid(_dot(xn, w_ref[:, 4 * D:4 * D + LANES]) + bf_ref[...])
    if decode:
        q_ref, kf_ref, vf_ref, og_ref, lf_ref = rest
        q_ref[...] = (q * (HEAD_DIM ** -0.5)).astype(BF)
        kf_ref[...] = k
        vf_ref[...] = v
        og_ref[...] = og
        lf_ref[...] = lf[:, :H]
        return
    place_ref, qa_ref, ka_ref, va_ref, kf_ref, vf_ref, og_ref, lf_ref, carry_ref = rest
    og_ref[...] = og
    kf_ref[...] = k.T.reshape(H, HEAD_DIM, tm)
    vf_ref[...] = v.T.reshape(H, HEAD_DIM, tm)
    lf_ref[...] = lf.T[:H, :]

    @pl.when(pl.program_id(1) == 0)
    def _():
        carry_ref[...] = jnp.zeros_like(carry_ref)

    tri = (_iota((tm, tm), 1) <= _iota((tm, tm), 0)).astype(BF)
    cs = _dot_exact_lhs01(tri, lf) + carry_ref[...]
    carry_ref[...] = cs[tm - 1:tm, :]
    hi, mid, lo = (p.astype(F32) for p in _split3(cs * LOG2E))
    lane = _iota((tm, LANES), 1)
    pieces = jnp.where(lane < H, hi,
                       jnp.where(lane < 2 * H, pltpu.roll(mid, H, 1),
                                 jnp.where(lane < 3 * H, pltpu.roll(lo, 2 * H, 1),
                                           jnp.where(lane == 3 * H, 1.0, 0.0))))
    extras = _dot(pieces.astype(BF), place_ref[...])
    qs = q * (HEAD_DIM ** -0.5 * LOG2E)
    first = lane < HEAD_DIM
    one_lane = jnp.where(lane == HEAD_DIM, 1.0, 0.0)
    for h in range(H):
        pair = slice((h // PAIR) * LANES, (h // PAIR + 1) * LANES)

        def head_tile(t):
            tile = t[:, pair]
            return pltpu.roll(tile, HEAD_DIM, 1) if h % PAIR else tile

        qa_ref[h] = jnp.where(first, head_tile(qs), extras[:, h * LANES:(h + 1) * LANES]).astype(BF)
        ka_ref[h] = jnp.where(first, head_tile(k), extras[:, (H + h) * LANES:(H + h + 1) * LANES]).astype(BF)
        va_ref[h] = jnp.where(first, head_tile(v), one_lane).astype(BF)


def _fox_proj(x, g, w_all, q_g, k_g, b_f, bd, *, decode):
    B, T, D = x.shape
    H = D // HEAD_DIM
    tm = _tiles(T)["tm"]
    assert T % tm == 0
    tok = lambda n: pl.BlockSpec((None, tm, n), lambda b, t: (b, t, 0))
    consts = [w_all, q_g, k_g, b_f, bd]
    if decode:
        out_specs = [tok(D), tok(D), tok(D), tok(D), tok(H)]
        out_shape = [jax.ShapeDtypeStruct((B, T, D), BF), jax.ShapeDtypeStruct((B, T, D), F32),
                     jax.ShapeDtypeStruct((B, T, D), F32), jax.ShapeDtypeStruct((B, T, D), BF),
                     jax.ShapeDtypeStruct((B, T, H), F32)]
        scratch = []
    else:
        consts.append(_aug_placement(H))
        heads = pl.BlockSpec((None, H, tm, LANES), lambda b, t: (b, 0, t, 0))
        kv_t = pl.BlockSpec((None, H, HEAD_DIM, tm), lambda b, t: (b, 0, 0, t))
        out_specs = [heads] * 3 + [kv_t, kv_t, tok(D), pl.BlockSpec((None, H, tm), lambda b, t: (b, 0, t))]
        out_shape = ([jax.ShapeDtypeStruct((B, H, T, LANES), BF)] * 3
                     + [jax.ShapeDtypeStruct((B, H, HEAD_DIM, T), F32)] * 2
                     + [jax.ShapeDtypeStruct((B, T, D), BF), jax.ShapeDtypeStruct((B, H, T), F32)])
        scratch = [pltpu.VMEM((1, LANES), F32)]
    return pl.pallas_call(
        functools.partial(_fox_proj_kernel, decode=decode),
        grid=(B, T // tm),
        in_specs=[tok(D), _const_spec((None, 1, D), (g[1], 0, 0))] + [_const_spec(c.shape) for c in consts],
        out_specs=out_specs,
        out_shape=out_shape,
        scratch_shapes=scratch,
        compiler_params=_params(("arbitrary", "arbitrary")),
        name="fox_proj",
    )(x, g[0], *consts)


def _fox_attn_kernel(qi_ref, ki_ref, q_ref, k_ref, v_ref, og_ref, o_ref, m_ref, acc_ref, *, tq):
    step = pl.program_id(2)
    qi = qi_ref[step]
    ki = ki_ref[step]
    heads = range(q_ref.shape[0])

    @pl.when(ki == 0)
    def _():
        m_ref[...] = jnp.full_like(m_ref, NEG_BIG)
        acc_ref[...] = jnp.zeros_like(acc_ref)

    def update(rows, n_keys, causal_tail):
        n_rows = rows.stop - rows.start
        s = [_dot_nt(q_ref[h, rows, :], k_ref[h, 0:n_keys, :]) for h in heads]
        if causal_tail:
            shape = (n_rows, n_keys)
            visible = _iota(shape, 1) - (n_keys - n_rows) <= _iota(shape, 0)
            s = [jnp.where(visible, x, NEG_BIG) for x in s]
        m_prev = [m_ref[h, rows, :] for h in heads]
        m_new = [jnp.maximum(mp, jnp.max(x, axis=-1, keepdims=True)) for mp, x in zip(m_prev, s)]
        p = [jnp.exp2(x - jnp.concatenate([mn] * (n_keys // LANES), axis=1)).astype(BF) for x, mn in zip(s, m_new)]
        for h in heads:
            acc_ref[h, rows, :] = (jnp.exp2(m_prev[h] - m_new[h]) * acc_ref[h, rows, :]
                                   + _dot(p[h], v_ref[h, 0:n_keys, :]))
            m_ref[h, rows, :] = m_new[h]

    @pl.when(ki < qi)
    def _():
        update(slice(0, tq), tq, False)

    @pl.when(ki == qi)
    def _():
        half = tq // 2
        update(slice(0, half), half, True)
        update(slice(half, tq), tq, True)
        o = [acc_ref[h] for h in heads]
        o = [x / x[:, HEAD_DIM:HEAD_DIM + 1] for x in o]
        assert PAIR == 2
        first = _iota((tq, LANES), 1) < HEAD_DIM
        pairs = [jnp.where(first, o[h], pltpu.roll(o[h + 1], HEAD_DIM, 1)) for h in heads[::PAIR]]
        o_ref[...] = (jnp.concatenate(pairs, axis=1) * og_ref[...].astype(F32)).astype(BF)


def _fox_attn(qa, ka, va, og):
    B, H, T, _ = qa.shape
    D = og.shape[-1]
    tq = _tiles(T)["tq"]
    assert T % tq == 0 and tq % (2 * LANES) == 0
    nq = T // tq
    qi_tab = np.asarray([q for q in range(nq) for _ in range(q + 1)], np.int32)
    ki_tab = np.asarray([k for q in range(nq) for k in range(q + 1)], np.int32)
    hs = ATTN_HEADS
    assert H % hs == 0 and hs % PAIR == 0
    qspec = pl.BlockSpec((None, hs, tq, LANES), lambda b, p, s, qt, kt: (b, p, qt[s], 0))
    kspec = pl.BlockSpec((None, hs, tq, LANES), lambda b, p, s, qt, kt: (b, p, kt[s], 0))
    ospec = pl.BlockSpec((None, tq, hs * HEAD_DIM), lambda b, p, s, qt, kt: (b, qt[s], p))
    return pl.pallas_call(
        functools.partial(_fox_attn_kernel, tq=tq),
        grid_spec=pltpu.PrefetchScalarGridSpec(
            num_scalar_prefetch=2,
            grid=(B, H // hs, len(qi_tab)),
            in_specs=[qspec, kspec, kspec, ospec],
            out_specs=ospec,
            scratch_shapes=[pltpu.VMEM((hs, tq, LANES), F32), pltpu.VMEM((hs, tq, LANES), F32)],
        ),
        out_shape=jax.ShapeDtypeStruct((B, T, D), BF),
        compiler_params=_params(("arbitrary",) * 3),
        name="fox_attn",
    )(jnp.asarray(qi_tab), jnp.asarray(ki_tab), qa, ka, va, og)


def _fox_decode_kernel(pt_ref, q_ref, kn_ref, vn_ref, lfn_ref, *rest, n_new, group, nb):
    n_in = nb * group
    kp_refs, vp_refs, lfp_refs = rest[:n_in], rest[n_in:2 * n_in], rest[2 * n_in:3 * n_in]
    og_ref, o_ref, qbd_ref, m_ref, l_ref, acc_ref, carry_ref = rest[3 * n_in:]
    j = pl.program_id(1)
    D = q_ref.shape[-1]
    n_heads = D // HEAD_DIM
    rows = n_heads * n_new
    page = lfp_refs[0].shape[-1]
    assert rows == LANES and page == LANES
    seqs = range(nb)
    rr = _iota((rows, LANES), 0)
    ll = _iota((rows, LANES), 1)
    expand = (_iota((rows, n_heads), 0) // n_new == _iota((rows, n_heads), 1)).astype(BF)

    def online_update(s, pv):
        m_prev = [m_ref[i] for i in seqs]
        m_new = [jnp.maximum(m_prev[i], jnp.max(s[i], axis=-1, keepdims=True)) for i in seqs]
        p = [jnp.exp(s[i] - m_new[i]) for i in seqs]
        for i in seqs:
            alpha = jnp.exp(m_prev[i] - m_new[i])
            l_ref[i] = alpha * l_ref[i] + jnp.sum(p[i], axis=-1, keepdims=True)
            acc_ref[i] = alpha * acc_ref[i] + pv(i, p[i].astype(BF))
            m_ref[i] = m_new[i]

    @pl.when(j == 0)
    def _():
        m_ref[...] = jnp.full_like(m_ref, NEG_BIG)
        l_ref[...] = jnp.zeros_like(l_ref)
        acc_ref[...] = jnp.zeros_like(acc_ref)
        own = _iota((rows, D), 1) // HEAD_DIM == _iota((rows, D), 0) // n_new
        incl = (_iota((page, page), 0) <= _iota((page, page), 1)).astype(BF)
        t_of_row = rr % n_new
        pad = page - n_new
        s, vn = [], []
        for i in seqs:
            q = q_ref[i].astype(F32)
            qbd_ref[i] = jnp.where(own, jnp.concatenate([q] * n_heads, axis=0), 0.0).astype(BF)
            kn = jnp.concatenate([kn_ref[i], jnp.zeros((pad, D), F32)], axis=0).astype(BF)
            vn.append(jnp.concatenate([vn_ref[i], jnp.zeros((pad, D), F32)], axis=0).astype(BF))
            lfn = jnp.concatenate([lfn_ref[i], jnp.zeros((pad, n_heads), F32)], axis=0)
            pieces = [_dot_nt(expand, piece).astype(BF) for piece in _split3(lfn)]
            cn = sum(_dot(piece, incl) for piece in pieces)
            cn_q = jnp.sum(jnp.where(ll == t_of_row, cn, 0.0), axis=-1, keepdims=True)
            carry_ref[i] = cn_q
            s.append(jnp.where(ll <= t_of_row, _dot_nt(qbd_ref[i], kn) + (cn_q - cn), NEG_BIG))
        online_update(s, lambda i, p: _dot(p, vn[i]))

    @pl.when(j > 0)
    def _():
        n_keys = group * page
        later = (_iota((n_keys, n_keys), 0) > _iota((n_keys, n_keys), 1)).astype(BF)
        of = lambda refs, i: refs[i * group:(i + 1) * group]
        kt = [jnp.concatenate([r[...].reshape(D, page).astype(BF) for r in of(kp_refs, i)], axis=1) for i in seqs]
        vt = [jnp.concatenate([r[...].reshape(D, page).astype(BF) for r in of(vp_refs, i)], axis=1) for i in seqs]
        lft = [jnp.concatenate([r[...] for r in of(lfp_refs, i)], axis=1) for i in seqs]
        pieces = [[_dot(expand, piece).astype(BF) for piece in _split3(x)] for x in lft]
        suffix = [sum(_dot(piece, later) for piece in ps) for ps in pieces]
        total = [sum(jnp.sum(piece.astype(F32), axis=-1, keepdims=True) for piece in ps) for ps in pieces]
        s = [_dot(qbd_ref[i], kt[i]) + (suffix[i] + carry_ref[i]) for i in seqs]
        online_update(s, lambda i, p: _dot_nt(p, vt[i]))
        for i in seqs:
            carry_ref[i] = carry_ref[i] + total[i]

    @pl.when(j == pl.num_programs(1) - 1)
    def _():
        for i in seqs:
            o = acc_ref[i] / l_ref[i]
            tiles = []
            for p in range(D // LANES):
                lo = o[(PAIR * p) * n_new:(PAIR * p + 1) * n_new, p * LANES:(p + 1) * LANES]
                hi = o[(PAIR * p + 1) * n_new:(PAIR * p + 2) * n_new, p * LANES:(p + 1) * LANES]
                tiles.append(jnp.where(_iota((n_new, LANES), 1) < HEAD_DIM, lo, hi))
            o_ref[i] = (jnp.concatenate(tiles, axis=1) * og_ref[i].astype(F32)).astype(BF)


def _fox_decode(q, k_new, v_new, lf_new, og, cache_k, cache_v, cache_lf, page_table):
    DB, n_new, D = q.shape
    H = D // HEAD_DIM
    n_pool, page = cache_k.shape[:2]
    n_pages = page_table.shape[1]
    ck = cache_k.transpose(0, 2, 3, 1)
    cv = cache_v.transpose(0, 2, 3, 1)
    clf = cache_lf.transpose(0, 2, 1)
    group, nb = DECODE_PAGES, DECODE_SEQS
    assert n_pages % group == 0 and DB % nb == 0
    new = lambda n: pl.BlockSpec((nb, n_new, n), lambda b, j, pt: (b, 0, 0))

    def paged(*shape):
        zeros = (0,) * len(shape)
        return [pl.BlockSpec((None,) + shape,
                             lambda b, j, pt, i=i, g=g: (pt[nb * b + i, n_pages - group * jnp.maximum(j, 1) + g],) + zeros)
                for i in range(nb) for g in range(group)]

    rows = H * n_new
    kv_page = (H, HEAD_DIM, page)
    n_in = nb * group
    return pl.pallas_call(
        functools.partial(_fox_decode_kernel, n_new=n_new, group=group, nb=nb),
        grid_spec=pltpu.PrefetchScalarGridSpec(
            num_scalar_prefetch=1,
            grid=(DB // nb, n_pages // group + 1),
            in_specs=[new(D), new(D), new(D), new(H), *paged(*kv_page), *paged(*kv_page), *paged(H, page), new(D)],
            out_specs=new(D),
            scratch_shapes=[pltpu.VMEM((nb, rows, D), BF), pltpu.VMEM((nb, rows, 1), F32),
                            pltpu.VMEM((nb, rows, 1), F32), pltpu.VMEM((nb, rows, D), F32),
                            pltpu.VMEM((nb, rows, 1), F32)],
        ),
        out_shape=jax.ShapeDtypeStruct((DB, n_new, D), BF),
        compiler_params=_params(("arbitrary", "arbitrary")),
        name="fox_decode",
    )(page_table, q, k_new, v_new, lf_new, *[ck] * n_in, *[cv] * n_in, *[clf] * n_in, og)


def _resid_mm_kernel(x_ref, a_ref, w_ref, y_ref):
    y_ref[...] = x_ref[...] + _dot(a_ref[...], w_ref[...])


def _resid_mm(x, a, w):
    B, T, D = x.shape
    K = a.shape[-1]
    tm = _tiles(T)["tm"]
    return pl.pallas_call(
        _resid_mm_kernel,
        grid=(B, T // tm),
        in_specs=[pl.BlockSpec((None, tm, D), lambda b, t: (b, t, 0)),
                  pl.BlockSpec((None, tm, K), lambda b, t: (b, t, 0)),
                  _const_spec((K, D))],
        out_specs=pl.BlockSpec((None, tm, D), lambda b, t: (b, t, 0)),
        out_shape=jax.ShapeDtypeStruct((B, T, D), F32),
        compiler_params=_params(("arbitrary", "arbitrary")),
        name="resid_mm",
    )(x, a, w)


def _rwkv_proj_kernel(x_ref, sh_ref, g_ref, mu_ref, wr_ref, wk_ref, wv_ref, w0_ref, w1_ref, w2_ref,
                      a0_ref, a1_ref, a2_ref, g1_ref, g2_ref, kk_ref, ka_ref, rk_ref, bd_ref,
                      r_o, lw_o, k_o, v_o, kk_o, a_o, g_o, bonus_o, shift_o, *scratch, seg):
    x = x_ref[...]
    tm, D = x.shape
    xn = _rms(x, g_ref[...])
    row = _iota((tm, D), 0)
    if seg is None:
        carry_ref, = scratch

        @pl.when(pl.program_id(1) == 0)
        def _():
            carry_ref[...] = sh_ref[...]

        prev = jnp.where(row == 0, carry_ref[...], pltpu.roll(xn, 1, 0))
        carry_ref[...] = xn[tm - 1:tm, :]
        shift_o[...] = xn[tm - 1:tm, :]
    else:
        prev = jnp.where(row % seg == 0, sh_ref[...], pltpu.roll(xn, 1, 0))
        shift_o[...] = xn.reshape(tm // seg, seg, D)[:, seg - 1, :]
    xx = prev - xn
    xr, xw, xk, xv, xa, xg = ((xn + xx * mu_ref[i:i + 1, :]).astype(BF) for i in range(6))
    r = _dot(xr, wr_ref[...])
    k = _dot(xk, wk_ref[...])
    v = _dot(xv, wv_ref[...])
    lora_w = _dot(jnp.tanh(_dot(xw, w1_ref[...])).astype(BF), w2_ref[...])
    w_log = -jax.nn.softplus(-(w0_ref[...] + lora_w)) - 0.5
    lw_o[...] = -jnp.exp(w_log)
    a = jax.nn.sigmoid(a0_ref[...] + _dot(_dot(xa, a1_ref[...]).astype(BF), a2_ref[...]))
    g_o[...] = _dot(jax.nn.sigmoid(_dot(xg, g1_ref[...])).astype(BF), g2_ref[...]).astype(BF)
    bd = bd_ref[...]
    kk = k * kk_ref[...]
    kk = kk / jnp.maximum(jnp.sqrt(_dot((kk * kk).astype(BF), bd)), 1e-12)
    k = k * (1.0 + (a - 1.0) * ka_ref[...])
    r_o[...] = r
    k_o[...] = k
    v_o[...] = v.astype(BF)
    kk_o[...] = kk
    a_o[...] = a
    bonus_o[...] = _dot((r * k * rk_ref[...]).astype(BF), bd) * v


def _rwkv_proj(x, shift, g, w, bd, *, seg=None):
    B, T, D = x.shape
    tm = _tiles(T)["tm"]
    assert T % tm == 0
    tok = pl.BlockSpec((None, tm, D), lambda b, t: (b, t, 0))
    if seg is None:
        sh_spec = pl.BlockSpec((None, 1, D), lambda b, t: (b, 0, 0))
        shift_shape = jax.ShapeDtypeStruct((B, 1, D), F32)
        shift_spec = pl.BlockSpec((None, 1, D), lambda b, t: (b, 0, 0))
        scratch = [pltpu.VMEM((1, D), F32)]
    else:
        assert B == 1 and tm == T
        sh_spec = pl.BlockSpec((None, T, D), lambda b, t: (0, 0, 0))
        shift_shape = jax.ShapeDtypeStruct((T // seg, D), F32)
        shift_spec = pl.BlockSpec((T // seg, D), lambda b, t: (0, 0))
        scratch = []
    mats = [w[n] for n in ("w_r", "w_k", "w_v")]
    consts = [w["mu"], *mats, w["w0"], w["w1"], w["w2"], w["a0"], w["a1"], w["a2"], w["g1"], w["g2"],
              w["k_k"], w["k_a"], w["r_k"], bd]
    return pl.pallas_call(
        functools.partial(_rwkv_proj_kernel, seg=seg),
        grid=(B, T // tm),
        in_specs=[tok, sh_spec, _const_spec((None, 1, D), (g[1], 0, 0))] + [_const_spec(c.shape) for c in consts],
        out_specs=[tok] * 8 + [shift_spec],
        out_shape=[jax.ShapeDtypeStruct((B, T, D), dt) for dt in (F32, F32, F32, BF, F32, F32, BF, F32)]
        + [shift_shape],
        scratch_shapes=scratch,
        compiler_params=_params(("arbitrary", "arbitrary")),
        name="rwkv_proj",
    )(x, shift, g[0], *consts)


def _rwkv_scan_kernel(r_ref, lw_ref, k_ref, v_ref, kk_ref, a_ref, s0_ref, y_ref, sT_ref, S_ref, *, C):
    nb, n_rows, d_model = r_ref.shape
    n_pairs = d_model // LANES
    D = nb * d_model

    def cat(ref):
        x = jnp.concatenate([ref[i] for i in range(nb)], axis=1)
        if n_rows < C:
            x = jnp.concatenate([x.astype(F32), jnp.zeros((C - n_rows, D), F32)], axis=0).astype(x.dtype)
        return x

    @pl.when(pl.program_id(1) == 0)
    def _():
        zero = jnp.zeros((HEAD_DIM, HEAD_DIM), F32)
        for i in range(nb):
            for pr in range(n_pairs):
                top = jnp.concatenate([s0_ref[i, PAIR * pr], zero], axis=1)
                bottom = jnp.concatenate([zero, s0_ref[i, PAIR * pr + 1]], axis=1)
                S_ref[i * n_pairs + pr] = jnp.concatenate([top, bottom], axis=0)

    lw = cat(lw_ref)
    tri = (_iota((C, C), 1) <= _iota((C, C), 0)).astype(BF)
    cum = _dot_exact_lhs01(tri, lw)
    cend = cum[C - 1:C, :]
    kk = cat(kk_ref)
    kka = kk * cat(a_ref)
    k = cat(k_ref)
    inv_p = jnp.exp(-cum)
    to_end = jnp.exp(cend - cum)
    at = (-(kk * jnp.exp(cum - lw))).astype(BF)
    bt = (kka * inv_p).astype(BF)
    kt = (k * inv_p).astype(BF)
    rt = (cat(r_ref) * jnp.exp(cum)).astype(BF)
    kh = (k * to_end).astype(BF)
    bh = (kka * to_end).astype(BF)
    vb = cat(v_ref)
    p_end = jnp.exp(cend)

    first_half = _iota((C, LANES), 1) < HEAD_DIM
    n2 = PAIR * C
    ri = _iota((n2, n2), 0)
    ci = _iota((n2, n2), 1)
    same = ri // C == ci // C
    strict = same & (ci < ri)
    incl = same & (ci <= ri)
    eye = (ri == ci).astype(F32)
    same_head = _iota((LANES, LANES), 0) // HEAD_DIM == _iota((LANES, LANES), 1) // HEAD_DIM

    def stack(t):
        z = jnp.zeros_like(t)
        return jnp.concatenate([jnp.where(first_half, t, z), jnp.where(first_half, z, t)], axis=0)

    pairs = range(D // LANES)
    lanes = [slice(pr * LANES, (pr + 1) * LANES) for pr in pairs]
    at_s, rt_s, kt_s, bt_s, v_s = ([stack(t[:, ln]) for ln in lanes] for t in (at, rt, kt, bt, vb))
    aa = [_dot_nt(jnp.concatenate([at_s[pr], rt_s[pr]], axis=0), jnp.concatenate([kt_s[pr], bt_s[pr]], axis=0))
          for pr in pairs]
    a_ak = [jnp.where(strict, x[:n2, :n2], 0.0).astype(BF) for x in aa]
    a_ab = [jnp.where(strict, x[:n2, n2:], 0.0) for x in aa]
    a_rk = [jnp.where(incl, x[n2:, :n2], 0.0).astype(BF) for x in aa]
    a_rb = [jnp.where(incl, x[n2:, n2:], 0.0).astype(BF) for x in aa]
    inv = [eye + x for x in a_ab]
    npow = a_ab
    for _ in range(int(math.log2(C)) - 1):
        nbf = [x.astype(BF) for x in npow]
        npow = [_dot(x, x) for x in nbf]
        inv = [i + _dot(i.astype(BF), n.astype(BF)) for i, n in zip(inv, npow)]
    S = [S_ref[pr] for pr in pairs]
    Sb = [x.astype(BF) for x in S]
    rhs = [(_dot_nt(at_s[pr], Sb[pr]) + _dot(a_ak[pr], v_s[pr])).astype(BF) for pr in pairs]
    u = [_dot(inv[pr].astype(BF), rhs[pr]) for pr in pairs]
    y = [_dot_nt(rt_s[pr], Sb[pr]) + _dot(a_rk[pr], v_s[pr]) + _dot(a_rb[pr], u[pr].astype(BF)) for pr in pairs]
    for pr in pairs:
        y_ref[pr // n_pairs, :, lanes[pr % n_pairs]] = (y[pr][:C] + y[pr][C:])[:n_rows]
        u_pair = (u[pr][:C] + u[pr][C:]).astype(BF)
        dS = _dot_tn(vb[:, lanes[pr]], kh[:, lanes[pr]]) + _dot_tn(u_pair, bh[:, lanes[pr]])
        S_ref[pr] = S[pr] * p_end[:, lanes[pr]] + jnp.where(same_head, dS, 0.0)

    @pl.when(pl.program_id(1) == pl.num_programs(1) - 1)
    def _():
        for pr in pairs:
            i, h = pr // n_pairs, PAIR * (pr % n_pairs)
            sT_ref[i, h] = S_ref[pr][:HEAD_DIM, :HEAD_DIM]
            sT_ref[i, h + 1] = S_ref[pr][HEAD_DIM:, HEAD_DIM:]


def _rwkv_scan(r, lw, k, v, kk, a, s0):
    B, T, D = r.shape
    C, nb = RWKV_CHUNK, RWKV_SEQS
    rows = min(T, C)
    assert T % rows == 0 and B % nb == 0
    n_pairs = D // LANES
    tok = pl.BlockSpec((nb, rows, D), lambda b, t: (b, t, 0))
    st = pl.BlockSpec((nb, D // HEAD_DIM, HEAD_DIM, HEAD_DIM), lambda b, t: (b, 0, 0, 0))
    return pl.pallas_call(
        functools.partial(_rwkv_scan_kernel, C=C),
        grid=(B // nb, T // rows),
        in_specs=[tok] * 6 + [st],
        out_specs=[tok, st],
        out_shape=[jax.ShapeDtypeStruct((B, T, D), F32), jax.ShapeDtypeStruct(s0.shape, F32)],
        scratch_shapes=[pltpu.VMEM((nb * n_pairs, LANES, LANES), F32)],
        compiler_params=_params(("arbitrary", "arbitrary")),
        name="rwkv_scan",
    )(r, lw, k, v, kk, a, s0)


def _rwkv_post_kernel(x_ref, y_ref, bonus_ref, g_ref, lng_ref, lnb_ref, bd_ref, wo_ref, o_ref):
    y = y_ref[...]
    bd = bd_ref[...]
    d = y - _dot(y.astype(BF), bd) * (1.0 / HEAD_DIM)
    var = _dot((d * d).astype(BF), bd) * (1.0 / HEAD_DIM)
    yn = d * lax.rsqrt(var + LNX_EPS) * lng_ref[...] + lnb_ref[...] + bonus_ref[...]
    o_ref[...] = x_ref[...] + _dot((yn * g_ref[...].astype(F32)).astype(BF), wo_ref[...])


def _rwkv_post(x, y, bonus, g, ln_g, ln_b, bd, w_o):
    B, T, D = x.shape
    tm = _tiles(T)["tm"]
    tok = pl.BlockSpec((None, tm, D), lambda b, t: (b, t, 0))
    return pl.pallas_call(
        _rwkv_post_kernel,
        grid=(B, T // tm),
        in_specs=[tok, tok, tok, tok, _const_spec((1, D)), _const_spec((1, D)), _const_spec((D, D)),
                  _const_spec((D, D))],
        out_specs=tok,
        out_shape=jax.ShapeDtypeStruct((B, T, D), F32),
        compiler_params=_params(("arbitrary", "arbitrary")),
        name="rwkv_post",
    )(x, y, bonus, g, ln_g, ln_b, bd, w_o)


def _pad_cols(w, n):
    return jnp.pad(w, ((0, 0), (0, n - w.shape[1])))


def _pad_rows(w, n):
    return jnp.pad(w, ((0, n - w.shape[0]), (0, 0)))


def kernel(x_prompt, x_sample, cache_k, cache_v, cache_logf, page_table, state_wkv, state_shift, state_conv,
           mix_norm, a_w_in, a_v_norm, a_w_s, a_b_s, a_w_out,
           f_w_qkv, f_q_norm, f_k_norm, f_w_fgate, f_b_fgate, f_w_ogate, f_w_out,
           r_mu, r_w_r, r_w_k, r_w_v, r_w0, r_w1, r_w2, r_a0, r_a1, r_a2, r_g1, r_g2,
           r_k_k, r_k_a, r_r_k, r_lnx_g, r_lnx_b, r_w_o,
           ffn_norm, ffn_w_up, ffn_conv_w, ffn_conv_b, ffn_w_down):
    B, T, D = x_prompt.shape
    DB, n_new, _ = x_sample.shape
    H = D // HEAD_DIM
    depth = ffn_w_up.shape[0]
    F2 = ffn_w_up.shape[-1]
    M = DB * n_new
    row = lambda v: v.reshape(1, -1).astype(F32)

    mix_g = mix_norm.reshape(depth, 1, D)
    ffn_g = ffn_norm.reshape(depth, 1, D)
    ffn_up, ffn_down = ffn_w_up.astype(BF), ffn_w_down.astype(BF)
    ffn_cb = ffn_conv_b.reshape(depth, 1, F2)
    a_in, a_out = a_w_in.astype(BF), a_w_out.astype(BF)
    a_vg = a_v_norm.reshape(a_v_norm.shape[0], 1, -1)
    head_ones = (jnp.arange(D)[:, None] // HEAD_DIM == jnp.arange(D)[None, :] // HEAD_DIM).astype(BF)
    f_all = jnp.concatenate([f_w_qkv, f_w_ogate, _pad_cols(f_w_fgate, LANES)], axis=1).astype(BF)
    f_qg, f_kg = row(jnp.tile(f_q_norm, H)), row(jnp.tile(f_k_norm, H))
    f_bf = _pad_cols(row(f_b_fgate), LANES)
    f_out = f_w_out.astype(BF)
    lora = lambda w1, w2, n: (_pad_cols(w1, n).astype(BF), _pad_rows(w2, n).astype(BF))
    rw = dict(mu=r_mu, w_r=r_w_r.astype(BF), w_k=r_w_k.astype(BF), w_v=r_w_v.astype(BF),
              w0=row(r_w0), a0=row(r_a0), k_k=row(r_k_k), k_a=row(r_k_a), r_k=row(r_r_k))
    rw["w1"], rw["w2"] = lora(r_w1, r_w2, LANES)
    rw["a1"], rw["a2"] = lora(r_a1, r_a2, LANES)
    rw["g1"], rw["g2"] = lora(r_g1, r_g2, 2 * LANES)
    r_out = r_w_o.astype(BF)

    xp = x_prompt
    xs = x_sample.reshape(1, M, D)
    conv_p, conv_s, chunk_v_s = [], [], []
    outs = {}
    for i in range(depth):
        kind = i % 3
        g = (mix_g, i)
        if kind == 0:
            j = i // 3
            a_w = (a_in, a_vg, a_w_s[j], a_b_s[j][:, :, None], a_out)
            xp, = _chunk_mlp(xp, j, g, *a_w, seq=CHUNK, emit_v=False)
            xs, v_rows = _chunk_mlp(xs, j, g, *a_w, seq=n_new, emit_v=True)
            chunk_v_s.append(v_rows.reshape(DB, n_new, -1))
        elif kind == 1:
            f_w = (f_all, f_qg, f_kg, f_bf, head_ones)
            qa, ka, va, kf, vf, og, lf = _fox_proj(xp, g, *f_w, decode=False)
            outs["k_p"], outs["v_p"] = kf.transpose(0, 3, 1, 2), vf.transpose(0, 3, 1, 2)
            outs["logf_p"] = lf.transpose(0, 2, 1)
            xp = _resid_mm(xp, _fox_attn(qa, ka, va, og), f_out)
            q, kf, vf, og, lf = _fox_proj(xs, g, *f_w, decode=True)
            shp = (DB, n_new, D)
            outs["k_s"], outs["v_s"] = kf.reshape(DB, n_new, H, HEAD_DIM), vf.reshape(DB, n_new, H, HEAD_DIM)
            outs["logf_s"] = lf.reshape(DB, n_new, H)
            att = _fox_decode(q.reshape(shp), kf.reshape(shp), vf.reshape(shp), outs["logf_s"], og.reshape(shp),
                              cache_k, cache_v, cache_logf, page_table)
            xs = _resid_mm(xs, att.reshape(1, M, D), f_out)
        else:
            r, lw, k, v, kk, a, gate, bonus, shift = _rwkv_proj(xp, jnp.zeros((B, 1, D), F32), g, rw, head_ones)
            y, outs["wkv_p"] = _rwkv_scan(r, lw, k, v, kk, a, jnp.zeros((B, H, HEAD_DIM, HEAD_DIM), F32))
            outs["shift_p"] = shift.reshape(B, D)
            xp = _rwkv_post(xp, y, bonus, gate, row(r_lnx_g), row(r_lnx_b), head_ones, r_out)
            sh = jnp.pad(state_shift[:, None, :], ((0, 0), (0, n_new - 1), (0, 0))).reshape(1, M, D)
            r, lw, k, v, kk, a, gate, bonus, shift = _rwkv_proj(xs, sh, g, rw, head_ones, seg=n_new)
            seqs = [t.reshape(DB, n_new, D) for t in (r, lw, k, v, kk, a)]
            y, outs["wkv_s"] = _rwkv_scan(*seqs, state_wkv)
            outs["shift_s"] = shift
            xs = _rwkv_post(xs, y.reshape(1, M, D), bonus, gate, row(r_lnx_g), row(r_lnx_b), head_ones,
                            r_out)
        ffn_w = (ffn_g, ffn_up, ffn_conv_w, ffn_cb, ffn_down)
        xp, cp = _conv_ffn(xp, jnp.zeros((B, 2, F2), F32), i, *ffn_w)
        st = state_conv[i]
        e2 = jnp.pad(st, ((0, 0), (0, n_new - 2), (0, 0))).reshape(M, F2)
        e1 = jnp.pad(st[:, 1:2], ((0, 0), (0, n_new - 1), (0, 0))).reshape(M, F2)
        xs, cs = _conv_ffn(xs, jnp.stack([e2, e1]), i, *ffn_w, seg=n_new)
        conv_p.append(cp)
        conv_s.append(cs)
    return (xp, xs.reshape(DB, n_new, D), outs["k_p"], outs["v_p"], outs["logf_p"], outs["wkv_p"], outs["shift_p"],
            jnp.stack(conv_p), outs["k_s"], outs["v_s"], outs["logf_s"], outs["wkv_s"], outs["shift_s"],
            jnp.stack(conv_s), jnp.stack(chunk_v_s))
```

```python
import functools
import math

import jax
import jax.numpy as jnp
import numpy as np
from jax import lax
from jax.experimental import pallas as pl
from jax.experimental.pallas import tpu as pltpu

F32 = jnp.float32
BF = jnp.bfloat16

HEAD_DIM = 64
LANES = 128
SUBLANES = 8
PAIR = LANES // HEAD_DIM
A_GROUPS = 8
CHUNK = 128
RWKV_CHUNK = 64
RWKV_SEQS = 2
NORM_EPS = 1e-6
LNX_EPS = 64e-5
NEG_BIG = -1e30
LOG2E = math.log2(math.e)
N_PIECES = 3
ATTN_HEADS = 4
DIAG_BANDS = 4
DECODE_PAGES = 4
DECODE_SEQS = 2
VMEM_LIMIT = 56 * 1024 * 1024


def _tiles(n_tokens):
    tm = min(n_tokens, 512)
    return dict(tm=tm, tq=min(n_tokens, 1024))


def _dot(a, b):
    return jnp.dot(a, b, preferred_element_type=F32)


def _dot_nt(a, b):
    return lax.dot_general(a, b, (((1,), (1,)), ((), ())), preferred_element_type=F32)


def _dot_tn(a, b):
    return lax.dot_general(a, b, (((0,), (0,)), ((), ())), preferred_element_type=F32)


def _split3(x):
    hi = x.astype(BF)
    r1 = x - hi.astype(F32)
    mid = r1.astype(BF)
    lo = (r1 - mid.astype(F32)).astype(BF)
    return hi, mid, lo


def _dot_exact_lhs01(m01, x):
    hi, mid, lo = _split3(x)
    return _dot(m01, hi) + _dot(m01, mid) + _dot(m01, lo)


def _rms(x, g):
    return x * lax.rsqrt(jnp.mean(x * x, axis=-1, keepdims=True) + NORM_EPS) * g


def _iota(shape, axis):
    return lax.broadcasted_iota(jnp.int32, shape, axis)


def _const_spec(shape, index=None):
    nd = len(shape)
    idx = tuple(index) if index is not None else (0,) * nd
    return pl.BlockSpec(shape, lambda *_: idx, pipeline_mode=pl.Buffered(1))


def _params(sem):
    return pltpu.CompilerParams(dimension_semantics=sem, vmem_limit_bytes=VMEM_LIMIT)


def _ffn_kernel(x_ref, st_ref, g_ref, wu_ref, cw_ref, cb_ref, wd_ref, y_ref, ns_ref, hbuf_ref, act_ref, *, seg, fc,
                down_group):
    x = x_ref[...]
    tm = x.shape[0]
    d_ff = wd_ref.shape[0]
    xn = _rms(x, g_ref[...]).astype(BF)
    top = SUBLANES
    prev = slice(top - 2, top)
    if seg is None:
        @pl.when(pl.program_id(1) == 0)
        def _():
            hbuf_ref[prev, :] = st_ref[...]
    else:
        hbuf_ref[prev, :] = jnp.zeros((2, hbuf_ref.shape[1]), F32)
        pos = _iota((tm, fc), 0) % seg
    acc = jnp.zeros(y_ref.shape, F32)
    n_chunks = d_ff // fc
    offsets = lambda c: (c * fc, d_ff + c * fc)
    up = lambda c: [_dot(xn, wu_ref[:, off:off + fc]) for off in offsets(c)]
    h_next = up(0)
    for c in range(n_chunks):
        h_cur, h_next = h_next, (up(c + 1) if c + 1 < n_chunks else None)
        halves = []
        for off, h in zip(offsets(c), h_cur):
            cols = slice(off, off + fc)
            hbuf_ref[top:top + tm, cols] = h
            hm1 = hbuf_ref[top - 1:top - 1 + tm, cols]
            hm2 = hbuf_ref[top - 2:top - 2 + tm, cols]
            if seg is not None:
                hm1 = jnp.where(pos == 0, st_ref[1, :, cols], hm1)
                hm2 = jnp.where(pos < 2, st_ref[0, :, cols], hm2)
                ns_ref[:, :, cols] = h.reshape(tm // seg, seg, fc)[:, seg - 2:, :]
            halves.append(cb_ref[:, cols] + cw_ref[0:1, cols] * hm2 + cw_ref[1:2, cols] * hm1
                          + cw_ref[2:3, cols] * h)
        gate, val = halves
        act_ref[:, c * fc:(c + 1) * fc] = (gate * jax.nn.sigmoid(gate) * val).astype(BF)
        if (c + 1) % down_group == 0 or c + 1 == n_chunks:
            rows = slice((c // down_group) * down_group * fc, (c + 1) * fc)
            acc = acc + _dot(act_ref[:, rows], wd_ref[rows, :])
    y_ref[...] = x + acc
    if seg is None:
        last = hbuf_ref[top + tm - 2:top + tm, :]
        ns_ref[...] = last
        hbuf_ref[prev, :] = last


def _conv_ffn(x, state, layer, g, w_up, conv_w, conv_b, w_down, *, seg=None):
    B, T, D = x.shape
    F2 = w_up.shape[-1]
    tm = _tiles(T)["tm"]
    fc = 256
    assert T % tm == 0 and (F2 // 2) % fc == 0
    if seg is None:
        st_spec = pl.BlockSpec((None, 2, F2), lambda b, t: (b, 0, 0))
        ns_shape = jax.ShapeDtypeStruct((B, 2, F2), F32)
        ns_spec = pl.BlockSpec((None, 2, F2), lambda b, t: (b, 0, 0))
    else:
        assert B == 1 and tm == T and T % seg == 0
        st_spec = pl.BlockSpec((2, T, F2), lambda b, t: (0, 0, 0))
        ns_shape = jax.ShapeDtypeStruct((T // seg, 2, F2), F32)
        ns_spec = pl.BlockSpec((T // seg, 2, F2), lambda b, t: (0, 0, 0))
    return pl.pallas_call(
        functools.partial(_ffn_kernel, seg=seg, fc=fc, down_group=6),
        grid=(B, T // tm),
        in_specs=[
            pl.BlockSpec((None, tm, D), lambda b, t: (b, t, 0)),
            st_spec,
            _const_spec((None, 1, D), (layer, 0, 0)),
            _const_spec((None, D, F2), (layer, 0, 0)),
            _const_spec((None, 3, F2), (layer, 0, 0)),
            _const_spec((None, 1, F2), (layer, 0, 0)),
            _const_spec((None, F2 // 2, D), (layer, 0, 0)),
        ],
        out_specs=[pl.BlockSpec((None, tm, D), lambda b, t: (b, t, 0)), ns_spec],
        out_shape=[jax.ShapeDtypeStruct((B, T, D), F32), ns_shape],
        scratch_shapes=[pltpu.VMEM((SUBLANES + tm, F2), F32), pltpu.VMEM((tm, F2 // 2), BF)],
        compiler_params=_params(("arbitrary", "arbitrary")),
        name="conv_ffn",
    )(x, state, g, w_up, conv_w, conv_b, w_down)


def _cmlp_kernel(x_ref, g_ref, win_ref, vg_ref, ws_ref, bs_ref, wout_ref, y_ref, *rest, seq, chunk):
    *v_out, z_ref = rest
    out_group = A_GROUPS // 2
    x = x_ref[...]
    tm = x.shape[0]
    d_a = wout_ref.shape[0]
    gd = d_a // A_GROUPS
    xn = _rms(x, g_ref[...]).astype(BF)
    span = ws_ref.shape[-1]
    r = _iota((span, span), 0)
    c = _iota((span, span), 1)
    if seq == chunk:
        assert chunk == span
        mix_of = lambda g: jnp.where(c <= r, ws_ref[g], 0.0).astype(BF)
        bias_of = lambda g: bs_ref[g]
    else:
        pick = (_iota((chunk, span), 0) % seq == _iota((chunk, span), 1)).astype(BF)
        same_seq = _iota((chunk, chunk), 0) // seq == _iota((chunk, chunk), 1) // seq

        def mix_of(g):
            corner = jnp.where((c <= r) & (r < seq), ws_ref[g], 0.0).astype(BF)
            return jnp.where(same_seq, _dot_nt(_dot(pick, corner).astype(BF), pick), 0.0).astype(BF)

        bias_of = lambda g: _dot_exact_lhs01(pick, jnp.broadcast_to(bs_ref[g], (span, gd)))
    acc = jnp.zeros(y_ref.shape, F32)
    proj = lambda g: [_dot(xn, win_ref[:, off + g * gd:off + (g + 1) * gd]) for off in (0, d_a)]
    uv_next = proj(0)
    for g in range(A_GROUPS):
        (u, v), uv_next = uv_next, (proj(g + 1) if g + 1 < A_GROUPS else None)
        cols = slice(g * gd, (g + 1) * gd)
        u = jax.nn.gelu(u)
        v = jax.nn.gelu(v)
        v = v * lax.rsqrt(jnp.mean(v * v, axis=-1, keepdims=True) + NORM_EPS) * vg_ref[:, cols]
        if v_out:
            v_out[0][:, cols] = v
        wm = mix_of(g)
        bias = bias_of(g)
        vb = v.astype(BF)
        parts = [_dot(wm, vb[j * chunk:(j + 1) * chunk, :]) + bias for j in range(tm // chunk)]
        mixed = parts[0] if len(parts) == 1 else jnp.concatenate(parts, axis=0)
        z_ref[:, cols] = (u * mixed).astype(BF)
        if (g + 1) % out_group == 0:
            rows = slice((g + 1 - out_group) * gd, (g + 1) * gd)
            acc = acc + _dot(z_ref[:, rows], wout_ref[rows, :])
    y_ref[...] = x + acc


def _chunk_mlp(x, j, g, w_in, v_g, w_s, b_s, w_out, *, seq, emit_v):
    B, T, D = x.shape
    span = w_s.shape[-1]
    d_a = w_out.shape[1]
    tm = _tiles(T)["tm"]
    chunk = span if seq == span else tm
    assert T % tm == 0 and tm % chunk == 0 and chunk % seq == 0
    tok = pl.BlockSpec((None, tm, D), lambda b, t: (b, t, 0))
    out_specs = [tok]
    out_shape = [jax.ShapeDtypeStruct((B, T, D), F32)]
    if emit_v:
        out_specs.append(pl.BlockSpec((None, tm, d_a), lambda b, t: (b, t, 0)))
        out_shape.append(jax.ShapeDtypeStruct((B, T, d_a), F32))
    return pl.pallas_call(
        functools.partial(_cmlp_kernel, seq=seq, chunk=chunk),
        grid=(B, T // tm),
        in_specs=[
            tok,
            _const_spec((None, 1, D), (g[1], 0, 0)),
            _const_spec((None, D, 2 * d_a), (j, 0, 0)),
            _const_spec((None, 1, d_a), (j, 0, 0)),
            _const_spec((A_GROUPS, span, span)),
            _const_spec((A_GROUPS, span, 1)),
            _const_spec((None, d_a, D), (j, 0, 0)),
        ],
        out_specs=out_specs,
        out_shape=out_shape,
        scratch_shapes=[pltpu.VMEM((tm, d_a), BF)],
        compiler_params=_params(("arbitrary", "arbitrary")),
        name="chunk_mlp",
    )(x, g[0], w_in, v_g, w_s, b_s, w_out)


def _aug_placement(n_heads):
    assert N_PIECES * n_heads < LANES
    w = np.zeros((LANES, 2, n_heads, LANES), np.float32)
    one = N_PIECES * n_heads
    for h in range(n_heads):
        for p in range(N_PIECES):
            w[p * n_heads + h, 0, h, HEAD_DIM + p] = 1.0
            w[one, 0, h, HEAD_DIM + N_PIECES + p] = 1.0
            w[one, 1, h, HEAD_DIM + p] = 1.0
            w[p * n_heads + h, 1, h, HEAD_DIM + N_PIECES + p] = -1.0
    return jnp.asarray(w.reshape(LANES, 2 * n_heads * LANES), BF)


def _fox_proj_kernel(x_ref, g_ref, w_ref, qg_ref, kg_ref, bf_ref, bd_ref, *rest, decode):
    x = x_ref[...]
    tm, D = x.shape
    H = D // HEAD_DIM
    xn = _rms(x, g_ref[...]).astype(BF)
    bd = bd_ref[...]

    def head_norm(t, gain):
        ms = _dot((t * t).astype(BF), bd) * (1.0 / HEAD_DIM)
        return t * lax.rsqrt(ms + NORM_EPS) * gain

    q = head_norm(_dot(xn, w_ref[:, 0:D]), qg_ref[...])
    k = head_norm(_dot(xn, w_ref[:, D:2 * D]), kg_ref[...])
    v = _dot(xn, w_ref[:, 2 * D:3 * D])
    og = jax.nn.sigmoid(_dot(xn, w_ref[:, 3 * D:4 * D])).astype(BF)
    lf = jax.nn.log_sigmoid(_dot(xn, w_ref[:, 4 * D:4 * D + LANES]) + bf_ref[...])
    if decode:
        q_ref, kf_ref, vf_ref, og_ref, lf_ref = rest
        q_ref[...] = (q * (HEAD_DIM ** -0.5)).astype(BF)
        kf_ref[...] = k
        vf_ref[...] = v
        og_ref[...] = og
        lf_ref[...] = lf[:, :H]
        return
    place_ref, qa_ref, ka_ref, va_ref, kf_ref, vf_ref, og_ref, lf_ref, carry_ref = rest
    og_ref[...] = og
    kf_ref[...] = k.T.reshape(H, HEAD_DIM, tm)
    vf_ref[...] = v.T.reshape(H, HEAD_DIM, tm)
    lf_ref[...] = lf.T[:H, :]

    @pl.when(pl.program_id(1) == 0)
    def _():
        carry_ref[...] = jnp.zeros_like(carry_ref)

    tri = (_iota((tm, tm), 1) <= _iota((tm, tm), 0)).astype(BF)
    cs = _dot_exact_lhs01(tri, lf) + carry_ref[...]
    carry_ref[...] = cs[tm - 1:tm, :]
    hi, mid, lo = (p.astype(F32) for p in _split3(cs * LOG2E))
    lane = _iota((tm, LANES), 1)
    pieces = jnp.where(lane < H, hi,
                       jnp.where(lane < 2 * H, pltpu.roll(mid, H, 1),
                                 jnp.where(lane < 3 * H, pltpu.roll(lo, 2 * H, 1),
                                           jnp.where(lane == 3 * H, 1.0, 0.0))))
    extras = _dot(pieces.astype(BF), place_ref[...])
    qs = q * (HEAD_DIM ** -0.5 * LOG2E)
    first = lane < HEAD_DIM
    one_lane = jnp.where(lane == HEAD_DIM, 1.0, 0.0)
    for h in range(H):
        pair = slice((h // PAIR) * LANES, (h // PAIR + 1) * LANES)

        def head_tile(t):
            tile = t[:, pair]
            return pltpu.roll(tile, HEAD_DIM, 1) if h % PAIR else tile

        qa_ref[h] = jnp.where(first, head_tile(qs), extras[:, h * LANES:(h + 1) * LANES]).astype(BF)
        ka_ref[h] = jnp.where(first, head_tile(k), extras[:, (H + h) * LANES:(H + h + 1) * LANES]).astype(BF)
        va_ref[h] = jnp.where(first, head_tile(v), one_lane).astype(BF)


def _fox_proj(x, g, w_all, q_g, k_g, b_f, bd, *, decode):
    B, T, D = x.shape
    H = D // HEAD_DIM
    tm = _tiles(T)["tm"]
    assert T % tm == 0
    tok = lambda n: pl.BlockSpec((None, tm, n), lambda b, t: (b, t, 0))
    consts = [w_all, q_g, k_g, b_f, bd]
    if decode:
        out_specs = [tok(D), tok(D), tok(D), tok(D), tok(H)]
        out_shape = [jax.ShapeDtypeStruct((B, T, D), BF), jax.ShapeDtypeStruct((B, T, D), F32),
                     jax.ShapeDtypeStruct((B, T, D), F32), jax.ShapeDtypeStruct((B, T, D), BF),
                     jax.ShapeDtypeStruct((B, T, H), F32)]
        scratch = []
    else:
        consts.append(_aug_placement(H))
        heads = pl.BlockSpec((None, H, tm, LANES), lambda b, t: (b, 0, t, 0))
        kv_t = pl.BlockSpec((None, H, HEAD_DIM, tm), lambda b, t: (b, 0, 0, t))
        out_specs = [heads] * 3 + [kv_t, kv_t, tok(D), pl.BlockSpec((None, H, tm), lambda b, t: (b, 0, t))]
        out_shape = ([jax.ShapeDtypeStruct((B, H, T, LANES), BF)] * 3
                     + [jax.ShapeDtypeStruct((B, H, HEAD_DIM, T), F32)] * 2
                     + [jax.ShapeDtypeStruct((B, T, D), BF), jax.ShapeDtypeStruct((B, H, T), F32)])
        scratch = [pltpu.VMEM((1, LANES), F32)]
    return pl.pallas_call(
        functools.partial(_fox_proj_kernel, decode=decode),
        grid=(B, T // tm),
        in_specs=[tok(D), _const_spec((None, 1, D), (g[1], 0, 0))] + [_const_spec(c.shape) for c in consts],
        out_specs=out_specs,
        out_shape=out_shape,
        scratch_shapes=scratch,
        compiler_params=_params(("arbitrary", "arbitrary")),
        name="fox_proj",
    )(x, g[0], *consts)


def _fox_attn_kernel(qi_ref, ki_ref, q_ref, k_ref, v_ref, og_ref, o_ref, m_ref, acc_ref, *, tq):
    step = pl.program_id(2)
    qi = qi_ref[step]
    ki = ki_ref[step]
    heads = range(q_ref.shape[0])

    @pl.when(ki == 0)
    def _():
        m_ref[...] = jnp.full_like(m_ref, NEG_BIG)
        acc_ref[...] = jnp.zeros_like(acc_ref)

    def update(rows, n_keys, causal_tail):
        n_rows = rows.stop - rows.start
        s = [_dot_nt(q_ref[h, rows, :], k_ref[h, 0:n_keys, :]) for h in heads]
        if causal_tail:
            shape = (n_rows, n_keys)
            visible = _iota(shape, 1) - (n_keys - n_rows) <= _iota(shape, 0)
            s = [jnp.where(visible, x, NEG_BIG) for x in s]
        m_prev = [m_ref[h, rows, :] for h in heads]
        m_new = [jnp.maximum(mp, jnp.max(x, axis=-1, keepdims=True)) for mp, x in zip(m_prev, s)]
        p = [jnp.exp2(x - jnp.concatenate([mn] * (n_keys // LANES), axis=1)).astype(BF) for x, mn in zip(s, m_new)]
        for h in heads:
            acc_ref[h, rows, :] = (jnp.exp2(m_prev[h] - m_new[h]) * acc_ref[h, rows, :]
                                   + _dot(p[h], v_ref[h, 0:n_keys, :]))
            m_ref[h, rows, :] = m_new[h]

    @pl.when(ki < qi)
    def _():
        update(slice(0, tq), tq, False)

    @pl.when(ki == qi)
    def _():
        band = tq // DIAG_BANDS
        for i in range(DIAG_BANDS):
            update(slice(i * band, (i + 1) * band), (i + 1) * band, True)
        o = [acc_ref[h] for h in heads]
        o = [x / x[:, HEAD_DIM:HEAD_DIM + 1] for x in o]
        assert PAIR == 2
        first = _iota((tq, LANES), 1) < HEAD_DIM
        pairs = [jnp.where(first, o[h], pltpu.roll(o[h + 1], HEAD_DIM, 1)) for h in heads[::PAIR]]
        o_ref[...] = (jnp.concatenate(pairs, axis=1) * og_ref[...].astype(F32)).astype(BF)


def _fox_attn(qa, ka, va, og):
    B, H, T, _ = qa.shape
    D = og.shape[-1]
    tq = _tiles(T)["tq"]
    assert T % tq == 0 and tq % (DIAG_BANDS * LANES) == 0
    nq = T // tq
    qi_tab = np.asarray([q for q in range(nq) for _ in range(q + 1)], np.int32)
    ki_tab = np.asarray([k for q in range(nq) for k in range(q + 1)], np.int32)
    hs = ATTN_HEADS
    assert H % hs == 0 and hs % PAIR == 0
    qspec = pl.BlockSpec((None, hs, tq, LANES), lambda b, p, s, qt, kt: (b, p, qt[s], 0))
    kspec = pl.BlockSpec((None, hs, tq, LANES), lambda b, p, s, qt, kt: (b, p, kt[s], 0))
    ospec = pl.BlockSpec((None, tq, hs * HEAD_DIM), lambda b, p, s, qt, kt: (b, qt[s], p))
    return pl.pallas_call(
        functools.partial(_fox_attn_kernel, tq=tq),
        grid_spec=pltpu.PrefetchScalarGridSpec(
            num_scalar_prefetch=2,
            grid=(B, H // hs, len(qi_tab)),
            in_specs=[qspec, kspec, kspec, ospec],
            out_specs=ospec,
            scratch_shapes=[pltpu.VMEM((hs, tq, LANES), F32), pltpu.VMEM((hs, tq, LANES), F32)],
        ),
        out_shape=jax.ShapeDtypeStruct((B, T, D), BF),
        compiler_params=_params(("arbitrary",) * 3),
        name="fox_attn",
    )(jnp.asarray(qi_tab), jnp.asarray(ki_tab), qa, ka, va, og)


def _fox_decode_kernel(pt_ref, q_ref, kn_ref, vn_ref, lfn_ref, *rest, n_new, group, nb):
    n_in = nb * group
    kp_refs, vp_refs, lfp_refs = rest[:n_in], rest[n_in:2 * n_in], rest[2 * n_in:3 * n_in]
    og_ref, o_ref, qbd_ref, m_ref, l_ref, acc_ref, carry_ref = rest[3 * n_in:]
    j = pl.program_id(1)
    D = q_ref.shape[-1]
    n_heads = D // HEAD_DIM
    rows = n_heads * n_new
    page = lfp_refs[0].shape[-1]
    assert rows == LANES and page == LANES
    seqs = range(nb)
    rr = _iota((rows, LANES), 0)
    ll = _iota((rows, LANES), 1)
    expand = (_iota((rows, n_heads), 0) // n_new == _iota((rows, n_heads), 1)).astype(BF)

    def online_update(s, pv):
        m_prev = [m_ref[i] for i in seqs]
        m_new = [jnp.maximum(m_prev[i], jnp.max(s[i], axis=-1, keepdims=True)) for i in seqs]
        p = [jnp.exp(s[i] - m_new[i]) for i in seqs]
        for i in seqs:
            alpha = jnp.exp(m_prev[i] - m_new[i])
            l_ref[i] = alpha * l_ref[i] + jnp.sum(p[i], axis=-1, keepdims=True)
            acc_ref[i] = alpha * acc_ref[i] + pv(i, p[i].astype(BF))
            m_ref[i] = m_new[i]

    @pl.when(j == 0)
    def _():
        m_ref[...] = jnp.full_like(m_ref, NEG_BIG)
        l_ref[...] = jnp.zeros_like(l_ref)
        acc_ref[...] = jnp.zeros_like(acc_ref)
        own = _iota((rows, D), 1) // HEAD_DIM == _iota((rows, D), 0) // n_new
        incl = (_iota((page, page), 0) <= _iota((page, page), 1)).astype(BF)
        t_of_row = rr % n_new
        pad = page - n_new
        s, vn = [], []
        for i in seqs:
            q = q_ref[i].astype(F32)
            qbd_ref[i] = jnp.where(own, jnp.concatenate([q] * n_heads, axis=0), 0.0).astype(BF)
            kn = jnp.concatenate([kn_ref[i], jnp.zeros((pad, D), F32)], axis=0).astype(BF)
            vn.append(jnp.concatenate([vn_ref[i], jnp.zeros((pad, D), F32)], axis=0).astype(BF))
            lfn = jnp.concatenate([lfn_ref[i], jnp.zeros((pad, n_heads), F32)], axis=0)
            pieces = [_dot_nt(expand, piece).astype(BF) for piece in _split3(lfn)]
            cn = sum(_dot(piece, incl) for piece in pieces)
            cn_q = jnp.sum(jnp.where(ll == t_of_row, cn, 0.0), axis=-1, keepdims=True)
            carry_ref[i] = cn_q
            s.append(jnp.where(ll <= t_of_row, _dot_nt(qbd_ref[i], kn) + (cn_q - cn), NEG_BIG))
        online_update(s, lambda i, p: _dot(p, vn[i]))

    @pl.when(j > 0)
    def _():
        n_keys = group * page
        later = (_iota((n_keys, n_keys), 0) > _iota((n_keys, n_keys), 1)).astype(BF)
        of = lambda refs, i: refs[i * group:(i + 1) * group]
        kt = [jnp.concatenate([r[...].reshape(D, page).astype(BF) for r in of(kp_refs, i)], axis=1) for i in seqs]
        vt = [jnp.concatenate([r[...].reshape(D, page).astype(BF) for r in of(vp_refs, i)], axis=1) for i in seqs]
        lft = [jnp.concatenate([r[...] for r in of(lfp_refs, i)], axis=1) for i in seqs]
        pieces = [[_dot(expand, piece).astype(BF) for piece in _split3(x)] for x in lft]
        suffix = [sum(_dot(piece, later) for piece in ps) for ps in pieces]
        total = [sum(jnp.sum(piece.astype(F32), axis=-1, keepdims=True) for piece in ps) for ps in pieces]
        s = [_dot(qbd_ref[i], kt[i]) + (suffix[i] + carry_ref[i]) for i in seqs]
        online_update(s, lambda i, p: _dot_nt(p, vt[i]))
        for i in seqs:
            carry_ref[i] = carry_ref[i] + total[i]

    @pl.when(j == pl.num_programs(1) - 1)
    def _():
        for i in seqs:
            o = acc_ref[i] / l_ref[i]
            tiles = []
            for p in range(D // LANES):
                lo = o[(PAIR * p) * n_new:(PAIR * p + 1) * n_new, p * LANES:(p + 1) * LANES]
                hi = o[(PAIR * p + 1) * n_new:(PAIR * p + 2) * n_new, p * LANES:(p + 1) * LANES]
                tiles.append(jnp.where(_iota((n_new, LANES), 1) < HEAD_DIM, lo, hi))
            o_ref[i] = (jnp.concatenate(tiles, axis=1) * og_ref[i].astype(F32)).astype(BF)


def _fox_decode(q, k_new, v_new, lf_new, og, cache_k, cache_v, cache_lf, page_table):
    DB, n_new, D = q.shape
    H = D // HEAD_DIM
    n_pool, page = cache_k.shape[:2]
    n_pages = page_table.shape[1]
    ck = cache_k.transpose(0, 2, 3, 1)
    cv = cache_v.transpose(0, 2, 3, 1)
    clf = cache_lf.transpose(0, 2, 1)
    group, nb = DECODE_PAGES, DECODE_SEQS
    assert n_pages % group == 0 and DB % nb == 0
    new = lambda n: pl.BlockSpec((nb, n_new, n), lambda b, j, pt: (b, 0, 0))

    def paged(*shape):
        zeros = (0,) * len(shape)
        return [pl.BlockSpec((None,) + shape,
                             lambda b, j, pt, i=i, g=g: (pt[nb * b + i, n_pages - group * jnp.maximum(j, 1) + g],) + zeros)
                for i in range(nb) for g in range(group)]

    rows = H * n_new
    kv_page = (H, HEAD_DIM, page)
    n_in = nb * group
    return pl.pallas_call(
        functools.partial(_fox_decode_kernel, n_new=n_new, group=group, nb=nb),
        grid_spec=pltpu.PrefetchScalarGridSpec(
            num_scalar_prefetch=1,
            grid=(DB // nb, n_pages // group + 1),
            in_specs=[new(D), new(D), new(D), new(H), *paged(*kv_page), *paged(*kv_page), *paged(H, page), new(D)],
            out_specs=new(D),
            scratch_shapes=[pltpu.VMEM((nb, rows, D), BF), pltpu.VMEM((nb, rows, 1), F32),
                            pltpu.VMEM((nb, rows, 1), F32), pltpu.VMEM((nb, rows, D), F32),
                            pltpu.VMEM((nb, rows, 1), F32)],
        ),
        out_shape=jax.ShapeDtypeStruct((DB, n_new, D), BF),
        compiler_params=_params(("arbitrary", "arbitrary")),
        name="fox_decode",
    )(page_table, q, k_new, v_new, lf_new, *[ck] * n_in, *[cv] * n_in, *[clf] * n_in, og)


def _resid_mm_kernel(x_ref, a_ref, w_ref, y_ref):
    y_ref[...] = x_ref[...] + _dot(a_ref[...], w_ref[...])


def _resid_mm(x, a, w):
    B, T, D = x.shape
    K = a.shape[-1]
    tm = _tiles(T)["tm"]
    return pl.pallas_call(
        _resid_mm_kernel,
        grid=(B, T // tm),
        in_specs=[pl.BlockSpec((None, tm, D), lambda b, t: (b, t, 0)),
                  pl.BlockSpec((None, tm, K), lambda b, t: (b, t, 0)),
                  _const_spec((K, D))],
        out_specs=pl.BlockSpec((None, tm, D), lambda b, t: (b, t, 0)),
        out_shape=jax.ShapeDtypeStruct((B, T, D), F32),
        compiler_params=_params(("arbitrary", "arbitrary")),
        name="resid_mm",
    )(x, a, w)


def _rwkv_proj_kernel(x_ref, sh_ref, g_ref, mu_ref, wr_ref, wk_ref, wv_ref, w0_ref, w1_ref, w2_ref,
                      a0_ref, a1_ref, a2_ref, g1_ref, g2_ref, kk_ref, ka_ref, rk_ref, bd_ref,
                      r_o, lw_o, k_o, v_o, kk_o, a_o, g_o, bonus_o, shift_o, *scratch, seg):
    x = x_ref[...]
    tm, D = x.shape
    xn = _rms(x, g_ref[...])
    row = _iota((tm, D), 0)
    if seg is None:
        carry_ref, = scratch

        @pl.when(pl.program_id(1) == 0)
        def _():
            carry_ref[...] = sh_ref[...]

        prev = jnp.where(row == 0, carry_ref[...], pltpu.roll(xn, 1, 0))
        carry_ref[...] = xn[tm - 1:tm, :]
        shift_o[...] = xn[tm - 1:tm, :]
    else:
        prev = jnp.where(row % seg == 0, sh_ref[...], pltpu.roll(xn, 1, 0))
        shift_o[...] = xn.reshape(tm // seg, seg, D)[:, seg - 1, :]
    xx = prev - xn
    xr, xw, xk, xv, xa, xg = ((xn + xx * mu_ref[i:i + 1, :]).astype(BF) for i in range(6))
    r = _dot(xr, wr_ref[...])
    k = _dot(xk, wk_ref[...])
    v = _dot(xv, wv_ref[...])
    lora_w = _dot(jnp.tanh(_dot(xw, w1_ref[...])).astype(BF), w2_ref[...])
    w_log = -jax.nn.softplus(-(w0_ref[...] + lora_w)) - 0.5
    lw_o[...] = -jnp.exp(w_log)
    a = jax.nn.sigmoid(a0_ref[...] + _dot(_dot(xa, a1_ref[...]).astype(BF), a2_ref[...]))
    g_o[...] = _dot(jax.nn.sigmoid(_dot(xg, g1_ref[...])).astype(BF), g2_ref[...]).astype(BF)
    bd = bd_ref[...]
    kk = k * kk_ref[...]
    kk = kk / jnp.maximum(jnp.sqrt(_dot((kk * kk).astype(BF), bd)), 1e-12)
    k = k * (1.0 + (a - 1.0) * ka_ref[...])
    r_o[...] = r
    k_o[...] = k
    v_o[...] = v.astype(BF)
    kk_o[...] = kk
    a_o[...] = a
    bonus_o[...] = _dot((r * k * rk_ref[...]).astype(BF), bd) * v


def _rwkv_proj(x, shift, g, w, bd, *, seg=None):
    B, T, D = x.shape
    tm = _tiles(T)["tm"]
    assert T % tm == 0
    tok = pl.BlockSpec((None, tm, D), lambda b, t: (b, t, 0))
    if seg is None:
        sh_spec = pl.BlockSpec((None, 1, D), lambda b, t: (b, 0, 0))
        shift_shape = jax.ShapeDtypeStruct((B, 1, D), F32)
        shift_spec = pl.BlockSpec((None, 1, D), lambda b, t: (b, 0, 0))
        scratch = [pltpu.VMEM((1, D), F32)]
    else:
        assert B == 1 and tm == T
        sh_spec = pl.BlockSpec((None, T, D), lambda b, t: (0, 0, 0))
        shift_shape = jax.ShapeDtypeStruct((T // seg, D), F32)
        shift_spec = pl.BlockSpec((T // seg, D), lambda b, t: (0, 0))
        scratch = []
    mats = [w[n] for n in ("w_r", "w_k", "w_v")]
    consts = [w["mu"], *mats, w["w0"], w["w1"], w["w2"], w["a0"], w["a1"], w["a2"], w["g1"], w["g2"],
              w["k_k"], w["k_a"], w["r_k"], bd]
    return pl.pallas_call(
        functools.partial(_rwkv_proj_kernel, seg=seg),
        grid=(B, T // tm),
        in_specs=[tok, sh_spec, _const_spec((None, 1, D), (g[1], 0, 0))] + [_const_spec(c.shape) for c in consts],
        out_specs=[tok] * 8 + [shift_spec],
        out_shape=[jax.ShapeDtypeStruct((B, T, D), dt) for dt in (F32, F32, F32, BF, F32, F32, BF, F32)]
        + [shift_shape],
        scratch_shapes=scratch,
        compiler_params=_params(("arbitrary", "arbitrary")),
        name="rwkv_proj",
    )(x, shift, g[0], *consts)


def _rwkv_scan_kernel(r_ref, lw_ref, k_ref, v_ref, kk_ref, a_ref, s0_ref, y_ref, sT_ref, S_ref, *, C):
    nb, n_rows, d_model = r_ref.shape
    n_pairs = d_model // LANES
    D = nb * d_model

    def cat(ref):
        x = jnp.concatenate([ref[i] for i in range(nb)], axis=1)
        if n_rows < C:
            x = jnp.concatenate([x.astype(F32), jnp.zeros((C - n_rows, D), F32)], axis=0).astype(x.dtype)
        return x

    @pl.when(pl.program_id(1) == 0)
    def _():
        zero = jnp.zeros((HEAD_DIM, HEAD_DIM), F32)
        for i in range(nb):
            for pr in range(n_pairs):
                top = jnp.concatenate([s0_ref[i, PAIR * pr], zero], axis=1)
                bottom = jnp.concatenate([zero, s0_ref[i, PAIR * pr + 1]], axis=1)
                S_ref[i * n_pairs + pr] = jnp.concatenate([top, bottom], axis=0)

    lw = cat(lw_ref)
    tri = (_iota((C, C), 1) <= _iota((C, C), 0)).astype(BF)
    cum = _dot_exact_lhs01(tri, lw)
    cend = cum[C - 1:C, :]
    kk = cat(kk_ref)
    kka = kk * cat(a_ref)
    k = cat(k_ref)
    inv_p = jnp.exp(-cum)
    to_end = jnp.exp(cend - cum)
    at = (-(kk * jnp.exp(cum - lw))).astype(BF)
    bt = (kka * inv_p).astype(BF)
    kt = (k * inv_p).astype(BF)
    rt = (cat(r_ref) * jnp.exp(cum)).astype(BF)
    kh = (k * to_end).astype(BF)
    bh = (kka * to_end).astype(BF)
    vb = cat(v_ref)
    p_end = jnp.exp(cend)

    first_half = _iota((C, LANES), 1) < HEAD_DIM
    n2 = PAIR * C
    ri = _iota((n2, n2), 0)
    ci = _iota((n2, n2), 1)
    same = ri // C == ci // C
    strict = same & (ci < ri)
    incl = same & (ci <= ri)
    eye = (ri == ci).astype(F32)
    same_head = _iota((LANES, LANES), 0) // HEAD_DIM == _iota((LANES, LANES), 1) // HEAD_DIM

    def stack(t):
        z = jnp.zeros_like(t)
        return jnp.concatenate([jnp.where(first_half, t, z), jnp.where(first_half, z, t)], axis=0)

    pairs = range(D // LANES)
    lanes = [slice(pr * LANES, (pr + 1) * LANES) for pr in pairs]
    at_s, rt_s, kt_s, bt_s, v_s = ([stack(t[:, ln]) for ln in lanes] for t in (at, rt, kt, bt, vb))
    aa = [_dot_nt(jnp.concatenate([at_s[pr], rt_s[pr]], axis=0), jnp.concatenate([kt_s[pr], bt_s[pr]], axis=0))
          for pr in pairs]
    a_ak = [jnp.where(strict, x[:n2, :n2], 0.0).astype(BF) for x in aa]
    a_ab = [jnp.where(strict, x[:n2, n2:], 0.0) for x in aa]
    a_rk = [jnp.where(incl, x[n2:, :n2], 0.0).astype(BF) for x in aa]
    a_rb = [jnp.where(incl, x[n2:, n2:], 0.0).astype(BF) for x in aa]
    inv = [eye + x for x in a_ab]
    npow = a_ab
    for _ in range(int(math.log2(C)) - 1):
        nbf = [x.astype(BF) for x in npow]
        npow = [_dot(x, x) for x in nbf]
        inv = [i + _dot(i.astype(BF), n.astype(BF)) for i, n in zip(inv, npow)]
    S = [S_ref[pr] for pr in pairs]
    Sb = [x.astype(BF) for x in S]
    rhs = [(_dot_nt(at_s[pr], Sb[pr]) + _dot(a_ak[pr], v_s[pr])).astype(BF) for pr in pairs]
    u = [_dot(inv[pr].astype(BF), rhs[pr]) for pr in pairs]
    y = [_dot_nt(rt_s[pr], Sb[pr]) + _dot(a_rk[pr], v_s[pr]) + _dot(a_rb[pr], u[pr].astype(BF)) for pr in pairs]
    for pr in pairs:
        y_ref[pr // n_pairs, :, lanes[pr % n_pairs]] = (y[pr][:C] + y[pr][C:])[:n_rows]
        u_pair = (u[pr][:C] + u[pr][C:]).astype(BF)
        dS = _dot_tn(vb[:, lanes[pr]], kh[:, lanes[pr]]) + _dot_tn(u_pair, bh[:, lanes[pr]])
        S_ref[pr] = S[pr] * p_end[:, lanes[pr]] + jnp.where(same_head, dS, 0.0)

    @pl.when(pl.program_id(1) == pl.num_programs(1) - 1)
    def _():
        for pr in pairs:
            i, h = pr // n_pairs, PAIR * (pr % n_pairs)
            sT_ref[i, h] = S_ref[pr][:HEAD_DIM, :HEAD_DIM]
            sT_ref[i, h + 1] = S_ref[pr][HEAD_DIM:, HEAD_DIM:]


def _rwkv_scan(r, lw, k, v, kk, a, s0):
    B, T, D = r.shape
    C, nb = RWKV_CHUNK, RWKV_SEQS
    rows = min(T, C)
    assert T % rows == 0 and B % nb == 0
    n_pairs = D // LANES
    tok = pl.BlockSpec((nb, rows, D), lambda b, t: (b, t, 0))
    st = pl.BlockSpec((nb, D // HEAD_DIM, HEAD_DIM, HEAD_DIM), lambda b, t: (b, 0, 0, 0))
    return pl.pallas_call(
        functools.partial(_rwkv_scan_kernel, C=C),
        grid=(B // nb, T // rows),
        in_specs=[tok] * 6 + [st],
        out_specs=[tok, st],
        out_shape=[jax.ShapeDtypeStruct((B, T, D), F32), jax.ShapeDtypeStruct(s0.shape, F32)],
        scratch_shapes=[pltpu.VMEM((nb * n_pairs, LANES, LANES), F32)],
        compiler_params=_params(("arbitrary", "arbitrary")),
        name="rwkv_scan",
    )(r, lw, k, v, kk, a, s0)


def _rwkv_post_kernel(x_ref, y_ref, bonus_ref, g_ref, lng_ref, lnb_ref, bd_ref, wo_ref, o_ref):
    y = y_ref[...]
    bd = bd_ref[...]
    d = y - _dot(y.astype(BF), bd) * (1.0 / HEAD_DIM)
    var = _dot((d * d).astype(BF), bd) * (1.0 / HEAD_DIM)
    yn = d * lax.rsqrt(var + LNX_EPS) * lng_ref[...] + lnb_ref[...] + bonus_ref[...]
    o_ref[...] = x_ref[...] + _dot((yn * g_ref[...].astype(F32)).astype(BF), wo_ref[...])


def _rwkv_post(x, y, bonus, g, ln_g, ln_b, bd, w_o):
    B, T, D = x.shape
    tm = _tiles(T)["tm"]
    tok = pl.BlockSpec((None, tm, D), lambda b, t: (b, t, 0))
    return pl.pallas_call(
        _rwkv_post_kernel,
        grid=(B, T // tm),
        in_specs=[tok, tok, tok, tok, _const_spec((1, D)), _const_spec((1, D)), _const_spec((D, D)),
                  _const_spec((D, D))],
        out_specs=tok,
        out_shape=jax.ShapeDtypeStruct((B, T, D), F32),
        compiler_params=_params(("arbitrary", "arbitrary")),
        name="rwkv_post",
    )(x, y, bonus, g, ln_g, ln_b, bd, w_o)


def _pad_cols(w, n):
    return jnp.pad(w, ((0, 0), (0, n - w.shape[1])))


def _pad_rows(w, n):
    return jnp.pad(w, ((0, n - w.shape[0]), (0, 0)))


def kernel(x_prompt, x_sample, cache_k, cache_v, cache_logf, page_table, state_wkv, state_shift, state_conv,
           mix_norm, a_w_in, a_v_norm, a_w_s, a_b_s, a_w_out,
           f_w_qkv, f_q_norm, f_k_norm, f_w_fgate, f_b_fgate, f_w_ogate, f_w_out,
           r_mu, r_w_r, r_w_k, r_w_v, r_w0, r_w1, r_w2, r_a0, r_a1, r_a2, r_g1, r_g2,
           r_k_k, r_k_a, r_r_k, r_lnx_g, r_lnx_b, r_w_o,
           ffn_norm, ffn_w_up, ffn_conv_w, ffn_conv_b, ffn_w_down):
    B, T, D = x_prompt.shape
    DB, n_new, _ = x_sample.shape
    H = D // HEAD_DIM
    depth = ffn_w_up.shape[0]
    F2 = ffn_w_up.shape[-1]
    M = DB * n_new
    row = lambda v: v.reshape(1, -1).astype(F32)

    mix_g = mix_norm.reshape(depth, 1, D)
    ffn_g = ffn_norm.reshape(depth, 1, D)
    ffn_up, ffn_down = ffn_w_up.astype(BF), ffn_w_down.astype(BF)
    ffn_cb = ffn_conv_b.reshape(depth, 1, F2)
    a_in, a_out = a_w_in.astype(BF), a_w_out.astype(BF)
    a_vg = a_v_norm.reshape(a_v_norm.shape[0], 1, -1)
    head_ones = (jnp.arange(D)[:, None] // HEAD_DIM == jnp.arange(D)[None, :] // HEAD_DIM).astype(BF)
    f_all = jnp.concatenate([f_w_qkv, f_w_ogate, _pad_cols(f_w_fgate, LANES)], axis=1).astype(BF)
    f_qg, f_kg = row(jnp.tile(f_q_norm, H)), row(jnp.tile(f_k_norm, H))
    f_bf = _pad_cols(row(f_b_fgate), LANES)
    f_out = f_w_out.astype(BF)
    lora = lambda w1, w2, n: (_pad_cols(w1, n).astype(BF), _pad_rows(w2, n).astype(BF))
    rw = dict(mu=r_mu, w_r=r_w_r.astype(BF), w_k=r_w_k.astype(BF), w_v=r_w_v.astype(BF),
              w0=row(r_w0), a0=row(r_a0), k_k=row(r_k_k), k_a=row(r_k_a), r_k=row(r_r_k))
    rw["w1"], rw["w2"] = lora(r_w1, r_w2, LANES)
    rw["a1"], rw["a2"] = lora(r_a1, r_a2, LANES)
    rw["g1"], rw["g2"] = lora(r_g1, r_g2, 2 * LANES)
    r_out = r_w_o.astype(BF)

    xp = x_prompt
    xs = x_sample.reshape(1, M, D)
    conv_p, conv_s, chunk_v_s = [], [], []
    outs = {}
    for i in range(depth):
        kind = i % 3
        g = (mix_g, i)
        if kind == 0:
            j = i // 3
            a_w = (a_in, a_vg, a_w_s[j], a_b_s[j][:, :, None], a_out)
            xp, = _chunk_mlp(xp, j, g, *a_w, seq=CHUNK, emit_v=False)
            xs, v_rows = _chunk_mlp(xs, j, g, *a_w, seq=n_new, emit_v=True)
            chunk_v_s.append(v_rows.reshape(DB, n_new, -1))
        elif kind == 1:
            f_w = (f_all, f_qg, f_kg, f_bf, head_ones)
            qa, ka, va, kf, vf, og, lf = _fox_proj(xp, g, *f_w, decode=False)
            outs["k_p"], outs["v_p"] = kf.transpose(0, 3, 1, 2), vf.transpose(0, 3, 1, 2)
            outs["logf_p"] = lf.transpose(0, 2, 1)
            xp = _resid_mm(xp, _fox_attn(qa, ka, va, og), f_out)
            q, kf, vf, og, lf = _fox_proj(xs, g, *f_w, decode=True)
            shp = (DB, n_new, D)
            outs["k_s"], outs["v_s"] = kf.reshape(DB, n_new, H, HEAD_DIM), vf.reshape(DB, n_new, H, HEAD_DIM)
            outs["logf_s"] = lf.reshape(DB, n_new, H)
            att = _fox_decode(q.reshape(shp), kf.reshape(shp), vf.reshape(shp), outs["logf_s"], og.reshape(shp),
                              cache_k, cache_v, cache_logf, page_table)
            xs = _resid_mm(xs, att.reshape(1, M, D), f_out)
        else:
            r, lw, k, v, kk, a, gate, bonus, shift = _rwkv_proj(xp, jnp.zeros((B, 1, D), F32), g, rw, head_ones)
            y, outs["wkv_p"] = _rwkv_scan(r, lw, k, v, kk, a, jnp.zeros((B, H, HEAD_DIM, HEAD_DIM), F32))
            outs["shift_p"] = shift.reshape(B, D)
            xp = _rwkv_post(xp, y, bonus, gate, row(r_lnx_g), row(r_lnx_b), head_ones, r_out)
            sh = jnp.pad(state_shift[:, None, :], ((0, 0), (0, n_new - 1), (0, 0))).reshape(1, M, D)
            r, lw, k, v, kk, a, gate, bonus, shift = _rwkv_proj(xs, sh, g, rw, head_ones, seg=n_new)
            seqs = [t.reshape(DB, n_new, D) for t in (r, lw, k, v, kk, a)]
            y, outs["wkv_s"] = _rwkv_scan(*seqs, state_wkv)
            outs["shift_s"] = shift
            xs = _rwkv_post(xs, y.reshape(1, M, D), bonus, gate, row(r_lnx_g), row(r_lnx_b), head_ones,
                            r_out)
        ffn_w = (ffn_g, ffn_up, ffn_conv_w, ffn_cb, ffn_down)
        xp, cp = _conv_ffn(xp, jnp.zeros((B, 2, F2), F32), i, *ffn_w)
        st = state_conv[i]
        e2 = jnp.pad(st, ((0, 0), (0, n_new - 2), (0, 0))).reshape(M, F2)
        e1 = jnp.pad(st[:, 1:2], ((0, 0), (0, n_new - 1), (0, 0))).reshape(M, F2)
        xs, cs = _conv_ffn(xs, jnp.stack([e2, e1]), i, *ffn_w, seg=n_new)
        conv_p.append(cp)
        conv_s.append(cs)
    return (xp, xs.reshape(DB, n_new, D), outs["k_p"], outs["v_p"], outs["logf_p"], outs["wkv_p"], outs["shift_p"],
            jnp.stack(conv_p), outs["k_s"], outs["v_s"], outs["logf_s"], outs["wkv_s"], outs["shift_s"],
            jnp.stack(conv_s), jnp.stack(chunk_v_s))
```

```python
import functools
import math

import jax
import jax.numpy as jnp
import numpy as np
from jax import lax
from jax.experimental import pallas as pl
from jax.experimental.pallas import tpu as pltpu

F32 = jnp.float32
BF = jnp.bfloat16

HEAD_DIM = 64
LANES = 128
SUBLANES = 8
PAIR = LANES // HEAD_DIM
A_GROUPS = 8
CHUNK = 128
RWKV_CHUNK = 64
RWKV_SEQS = 2
NORM_EPS = 1e-6
LNX_EPS = 64e-5
NEG_BIG = -1e30
LOG2E = math.log2(math.e)
N_PIECES = 3
ATTN_HEADS = 4
DIAG_BANDS = 4
DECODE_PAGES = 4
DECODE_SEQS = 2
VMEM_LIMIT = 56 * 1024 * 1024


def _tiles(n_tokens):
    tm = min(n_tokens, 512)
    return dict(tm=tm, tq=min(n_tokens, 1024))


def _dot(a, b):
    return jnp.dot(a, b, preferred_element_type=F32)


def _dot_nt(a, b):
    return lax.dot_general(a, b, (((1,), (1,)), ((), ())), preferred_element_type=F32)


def _dot_tn(a, b):
    return lax.dot_general(a, b, (((0,), (0,)), ((), ())), preferred_element_type=F32)


def _split3(x):
    hi = x.astype(BF)
    r1 = x - hi.astype(F32)
    mid = r1.astype(BF)
    lo = (r1 - mid.astype(F32)).astype(BF)
    return hi, mid, lo


def _dot_exact_lhs01(m01, x):
    hi, mid, lo = _split3(x)
    return _dot(m01, hi) + _dot(m01, mid) + _dot(m01, lo)


def _rms(x, g):
    return x * lax.rsqrt(jnp.mean(x * x, axis=-1, keepdims=True) + NORM_EPS) * g


def _iota(shape, axis):
    return lax.broadcasted_iota(jnp.int32, shape, axis)


def _const_spec(shape, index=None):
    nd = len(shape)
    idx = tuple(index) if index is not None else (0,) * nd
    return pl.BlockSpec(shape, lambda *_: idx, pipeline_mode=pl.Buffered(1))


def _params(sem):
    return pltpu.CompilerParams(dimension_semantics=sem, vmem_limit_bytes=VMEM_LIMIT)


def _ffn_kernel(x_ref, st_ref, g_ref, wu_ref, cw_ref, cb_ref, wd_ref, y_ref, ns_ref, hbuf_ref, act_ref, *, seg, fc,
                down_group):
    x = x_ref[...]
    tm = x.shape[0]
    d_ff = wd_ref.shape[0]
    xn = _rms(x, g_ref[...]).astype(BF)
    top = SUBLANES
    prev = slice(top - 2, top)
    if seg is None:
        @pl.when(pl.program_id(1) == 0)
        def _():
            hbuf_ref[prev, :] = st_ref[...]
    else:
        hbuf_ref[prev, :] = jnp.zeros((2, hbuf_ref.shape[1]), F32)
        pos = _iota((tm, fc), 0) % seg
    acc = jnp.zeros(y_ref.shape, F32)
    n_chunks = d_ff // fc
    offsets = lambda c: (c * fc, d_ff + c * fc)
    up = lambda c: [_dot(xn, wu_ref[:, off:off + fc]) for off in offsets(c)]
    h_next = up(0)
    for c in range(n_chunks):
        h_cur, h_next = h_next, (up(c + 1) if c + 1 < n_chunks else None)
        halves = []
        for off, h in zip(offsets(c), h_cur):
            cols = slice(off, off + fc)
            hbuf_ref[top:top + tm, cols] = h
            hm1 = hbuf_ref[top - 1:top - 1 + tm, cols]
            hm2 = hbuf_ref[top - 2:top - 2 + tm, cols]
            if seg is not None:
                hm1 = jnp.where(pos == 0, st_ref[1, :, cols], hm1)
                hm2 = jnp.where(pos < 2, st_ref[0, :, cols], hm2)
                ns_ref[:, :, cols] = h.reshape(tm // seg, seg, fc)[:, seg - 2:, :]
            halves.append(cb_ref[:, cols] + cw_ref[0:1, cols] * hm2 + cw_ref[1:2, cols] * hm1
                          + cw_ref[2:3, cols] * h)
        gate, val = halves
        act_ref[:, c * fc:(c + 1) * fc] = (gate * jax.nn.sigmoid(gate) * val).astype(BF)
        if (c + 1) % down_group == 0 or c + 1 == n_chunks:
            rows = slice((c // down_group) * down_group * fc, (c + 1) * fc)
            acc = acc + _dot(act_ref[:, rows], wd_ref[rows, :])
    y_ref[...] = x + acc
    if seg is None:
        last = hbuf_ref[top + tm - 2:top + tm, :]
        ns_ref[...] = last
        hbuf_ref[prev, :] = last


def _conv_ffn(x, state, layer, g, w_up, conv_w, conv_b, w_down, *, seg=None):
    B, T, D = x.shape
    F2 = w_up.shape[-1]
    tm = _tiles(T)["tm"]
    fc = 256
    assert T % tm == 0 and (F2 // 2) % fc == 0
    if seg is None:
        st_spec = pl.BlockSpec((None, 2, F2), lambda b, t: (b, 0, 0))
        ns_shape = jax.ShapeDtypeStruct((B, 2, F2), F32)
        ns_spec = pl.BlockSpec((None, 2, F2), lambda b, t: (b, 0, 0))
    else:
        assert B == 1 and tm == T and T % seg == 0
        st_spec = pl.BlockSpec((2, T, F2), lambda b, t: (0, 0, 0))
        ns_shape = jax.ShapeDtypeStruct((T // seg, 2, F2), F32)
        ns_spec = pl.BlockSpec((T // seg, 2, F2), lambda b, t: (0, 0, 0))
    return pl.pallas_call(
        functools.partial(_ffn_kernel, seg=seg, fc=fc, down_group=6),
        grid=(B, T // tm),
        in_specs=[
            pl.BlockSpec((None, tm, D), lambda b, t: (b, t, 0)),
            st_spec,
            _const_spec((None, 1, D), (layer, 0, 0)),
            _const_spec((None, D, F2), (layer, 0, 0)),
            _const_spec((None, 3, F2), (layer, 0, 0)),
            _const_spec((None, 1, F2), (layer, 0, 0)),
            _const_spec((None, F2 // 2, D), (layer, 0, 0)),
        ],
        out_specs=[pl.BlockSpec((None, tm, D), lambda b, t: (b, t, 0)), ns_spec],
        out_shape=[jax.ShapeDtypeStruct((B, T, D), F32), ns_shape],
        scratch_shapes=[pltpu.VMEM((SUBLANES + tm, F2), F32), pltpu.VMEM((tm, F2 // 2), BF)],
        compiler_params=_params(("arbitrary", "arbitrary")),
        name="conv_ffn",
    )(x, state, g, w_up, conv_w, conv_b, w_down)


def _cmlp_kernel(x_ref, g_ref, win_ref, vg_ref, ws_ref, bs_ref, wout_ref, y_ref, *rest, seq, chunk):
    *v_out, z_ref = rest
    out_group = A_GROUPS // 2
    x = x_ref[...]
    tm = x.shape[0]
    d_a = wout_ref.shape[0]
    gd = d_a // A_GROUPS
    xn = _rms(x, g_ref[...]).astype(BF)
    span = ws_ref.shape[-1]
    r = _iota((span, span), 0)
    c = _iota((span, span), 1)
    if seq == chunk:
        assert chunk == span
        mix_of = lambda g: jnp.where(c <= r, ws_ref[g], 0.0).astype(BF)
        bias_of = lambda g: bs_ref[g]
    else:
        pick = (_iota((chunk, span), 0) % seq == _iota((chunk, span), 1)).astype(BF)
        same_seq = _iota((chunk, chunk), 0) // seq == _iota((chunk, chunk), 1) // seq

        def mix_of(g):
            corner = jnp.where((c <= r) & (r < seq), ws_ref[g], 0.0).astype(BF)
            return jnp.where(same_seq, _dot_nt(_dot(pick, corner).astype(BF), pick), 0.0).astype(BF)

        bias_of = lambda g: _dot_exact_lhs01(pick, jnp.broadcast_to(bs_ref[g], (span, gd)))
    acc = jnp.zeros(y_ref.shape, F32)
    proj = lambda g: [_dot(xn, win_ref[:, off + g * gd:off + (g + 1) * gd]) for off in (0, d_a)]
    uv_next = proj(0)
    for g in range(A_GROUPS):
        (u, v), uv_next = uv_next, (proj(g + 1) if g + 1 < A_GROUPS else None)
        cols = slice(g * gd, (g + 1) * gd)
        u = jax.nn.gelu(u)
        v = jax.nn.gelu(v)
        v = v * lax.rsqrt(jnp.mean(v * v, axis=-1, keepdims=True) + NORM_EPS) * vg_ref[:, cols]
        if v_out:
            v_out[0][:, cols] = v
        wm = mix_of(g)
        bias = bias_of(g)
        vb = v.astype(BF)
        parts = [_dot(wm, vb[j * chunk:(j + 1) * chunk, :]) + bias for j in range(tm // chunk)]
        mixed = parts[0] if len(parts) == 1 else jnp.concatenate(parts, axis=0)
        z_ref[:, cols] = (u * mixed).astype(BF)
        if (g + 1) % out_group == 0:
            rows = slice((g + 1 - out_group) * gd, (g + 1) * gd)
            acc = acc + _dot(z_ref[:, rows], wout_ref[rows, :])
    y_ref[...] = x + acc


def _chunk_mlp(x, j, g, w_in, v_g, w_s, b_s, w_out, *, seq, emit_v):
    B, T, D = x.shape
    span = w_s.shape[-1]
    d_a = w_out.shape[1]
    tm = _tiles(T)["tm"]
    chunk = span if seq == span else tm
    assert T % tm == 0 and tm % chunk == 0 and chunk % seq == 0
    tok = pl.BlockSpec((None, tm, D), lambda b, t: (b, t, 0))
    out_specs = [tok]
    out_shape = [jax.ShapeDtypeStruct((B, T, D), F32)]
    if emit_v:
        out_specs.append(pl.BlockSpec((None, tm, d_a), lambda b, t: (b, t, 0)))
        out_shape.append(jax.ShapeDtypeStruct((B, T, d_a), F32))
    return pl.pallas_call(
        functools.partial(_cmlp_kernel, seq=seq, chunk=chunk),
        grid=(B, T // tm),
        in_specs=[
            tok,
            _const_spec((None, 1, D), (g[1], 0, 0)),
            _const_spec((None, D, 2 * d_a), (j, 0, 0)),
            _const_spec((None, 1, d_a), (j, 0, 0)),
            _const_spec((A_GROUPS, span, span)),
            _const_spec((A_GROUPS, span, 1)),
            _const_spec((None, d_a, D), (j, 0, 0)),
        ],
        out_specs=out_specs,
        out_shape=out_shape,
        scratch_shapes=[pltpu.VMEM((tm, d_a), BF)],
        compiler_params=_params(("arbitrary", "arbitrary")),
        name="chunk_mlp",
    )(x, g[0], w_in, v_g, w_s, b_s, w_out)


def _aug_placement(n_heads):
    assert N_PIECES * n_heads < LANES
    w = np.zeros((LANES, 2, n_heads, LANES), np.float32)
    one = N_PIECES * n_heads
    for h in range(n_heads):
        for p in range(N_PIECES):
            w[p * n_heads + h, 0, h, HEAD_DIM + p] = 1.0
            w[one, 0, h, HEAD_DIM + N_PIECES + p] = 1.0
            w[one, 1, h, HEAD_DIM + p] = 1.0
            w[p * n_heads + h, 1, h, HEAD_DIM + N_PIECES + p] = -1.0
    return jnp.asarray(w.reshape(LANES, 2 * n_heads * LANES), BF)


def _fox_proj_kernel(x_ref, g_ref, w_ref, qg_ref, kg_ref, bf_ref, bd_ref, *rest, decode):
    x = x_ref[...]
    tm, D = x.shape
    H = D // HEAD_DIM
    xn = _rms(x, g_ref[...]).astype(BF)
    bd = bd_ref[...]

    def head_norm(t, gain):
        ms = _dot((t * t).astype(BF), bd) * (1.0 / HEAD_DIM)
        return t * lax.rsqrt(ms + NORM_EPS) * gain

    q = head_norm(_dot(xn, w_ref[:, 0:D]), qg_ref[...])
    k = head_norm(_dot(xn, w_ref[:, D:2 * D]), kg_ref[...])
    v = _dot(xn, w_ref[:, 2 * D:3 * D])
    og = jax.nn.sigmoid(_dot(xn, w_ref[:, 3 * D:4 * D])).astype(BF)
    lf = jax.nn.log_sigmoid(_dot(xn, w_ref[:, 4 * D:4 * D + LANES]) + bf_ref[...])
    if decode:
        q_ref, kf_ref, vf_ref, og_ref, lf_ref = rest
        q_ref[...] = (q * (HEAD_DIM ** -0.5)).astype(BF)
        kf_ref[...] = k
        vf_ref[...] = v
        og_ref[...] = og
        lf_ref[...] = lf[:, :H]
        return
    place_ref, qa_ref, ka_ref, va_ref, kf_ref, vf_ref, og_ref, lf_ref, carry_ref = rest
    og_ref[...] = og
    kf_ref[...] = k.T.reshape(H, HEAD_DIM, tm)
    vf_ref[...] = v.T.reshape(H, HEAD_DIM, tm)
    lf_ref[...] = lf.T[:H, :]

    @pl.when(pl.program_id(1) == 0)
    def _():
        carry_ref[...] = jnp.zeros_like(carry_ref)

    tri = (_iota((tm, tm), 1) <= _iota((tm, tm), 0)).astype(BF)
    cs = _dot_exact_lhs01(tri, lf) + carry_ref[...]
    carry_ref[...] = cs[tm - 1:tm, :]
    hi, mid, lo = (p.astype(F32) for p in _split3(cs * LOG2E))
    lane = _iota((tm, LANES), 1)
    pieces = jnp.where(lane < H, hi,
                       jnp.where(lane < 2 * H, pltpu.roll(mid, H, 1),
                                 jnp.where(lane < 3 * H, pltpu.roll(lo, 2 * H, 1),
                                           jnp.where(lane == 3 * H, 1.0, 0.0))))
    extras = _dot(pieces.astype(BF), place_ref[...])
    qs = q * (HEAD_DIM ** -0.5 * LOG2E)
    first = lane < HEAD_DIM
    one_lane = jnp.where(lane == HEAD_DIM, 1.0, 0.0)
    for h in range(H):
        pair = slice((h // PAIR) * LANES, (h // PAIR + 1) * LANES)

        def head_tile(t):
            tile = t[:, pair]
            return pltpu.roll(tile, HEAD_DIM, 1) if h % PAIR else tile

        qa_ref[h] = jnp.where(first, head_tile(qs), extras[:, h * LANES:(h + 1) * LANES]).astype(BF)
        ka_ref[h] = jnp.where(first, head_tile(k), extras[:, (H + h) * LANES:(H + h + 1) * LANES]).astype(BF)
        va_ref[h] = jnp.where(first, head_tile(v), one_lane).astype(BF)


def _fox_proj(x, g, w_all, q_g, k_g, b_f, bd, *, decode):
    B, T, D = x.shape
    H = D // HEAD_DIM
    tm = _tiles(T)["tm"]
    assert T % tm == 0
    tok = lambda n: pl.BlockSpec((None, tm, n), lambda b, t: (b, t, 0))
    consts = [w_all, q_g, k_g, b_f, bd]
    if decode:
        out_specs = [tok(D), tok(D), tok(D), tok(D), tok(H)]
        out_shape = [jax.ShapeDtypeStruct((B, T, D), BF), jax.ShapeDtypeStruct((B, T, D), F32),
                     jax.ShapeDtypeStruct((B, T, D), F32), jax.ShapeDtypeStruct((B, T, D), BF),
                     jax.ShapeDtypeStruct((B, T, H), F32)]
        scratch = []
    else:
        consts.append(_aug_placement(H))
        heads = pl.BlockSpec((None, H, tm, LANES), lambda b, t: (b, 0, t, 0))
        kv_t = pl.BlockSpec((None, H, HEAD_DIM, tm), lambda b, t: (b, 0, 0, t))
        out_specs = [heads] * 3 + [kv_t, kv_t, tok(D), pl.BlockSpec((None, H, tm), lambda b, t: (b, 0, t))]
        out_shape = ([jax.ShapeDtypeStruct((B, H, T, LANES), BF)] * 3
                     + [jax.ShapeDtypeStruct((B, H, HEAD_DIM, T), F32)] * 2
                     + [jax.ShapeDtypeStruct((B, T, D), BF), jax.ShapeDtypeStruct((B, H, T), F32)])
        scratch = [pltpu.VMEM((1, LANES), F32)]
    return pl.pallas_call(
        functools.partial(_fox_proj_kernel, decode=decode),
        grid=(B, T // tm),
        in_specs=[tok(D), _const_spec((None, 1, D), (g[1], 0, 0))] + [_const_spec(c.shape) for c in consts],
        out_specs=out_specs,
        out_shape=out_shape,
        scratch_shapes=scratch,
        compiler_params=_params(("arbitrary", "arbitrary")),
        name="fox_proj",
    )(x, g[0], *consts)


def _fox_attn_kernel(qi_ref, ki_ref, q_ref, k_ref, v_ref, og_ref, o_ref, m_ref, acc_ref, *, tq):
    step = pl.program_id(2)
    qi = qi_ref[step]
    ki = ki_ref[step]
    heads = range(q_ref.shape[0])

    @pl.when(ki == 0)
    def _():
        m_ref[...] = jnp.full_like(m_ref, NEG_BIG)
        acc_ref[...] = jnp.zeros_like(acc_ref)

    def update(rows, n_keys, causal_tail):
        n_rows = rows.stop - rows.start
        s = [_dot_nt(q_ref[h, rows, :], k_ref[h, 0:n_keys, :]) for h in heads]
        if causal_tail:
            shape = (n_rows, n_keys)
            visible = _iota(shape, 1) - (n_keys - n_rows) <= _iota(shape, 0)
            s = [jnp.where(visible, x, NEG_BIG) for x in s]
        m_prev = [m_ref[h, rows, :] for h in heads]
        m_new = [jnp.maximum(mp, jnp.max(x, axis=-1, keepdims=True)) for mp, x in zip(m_prev, s)]
        p = [jnp.exp2(x - jnp.concatenate([mn] * (n_keys // LANES), axis=1)).astype(BF) for x, mn in zip(s, m_new)]
        for h in heads:
            acc_ref[h, rows, :] = (jnp.exp2(m_prev[h] - m_new[h]) * acc_ref[h, rows, :]
                                   + _dot(p[h], v_ref[h, 0:n_keys, :]))
            m_ref[h, rows, :] = m_new[h]

    @pl.when(ki < qi)
    def _():
        for i in range(2):
            update(slice(i * (tq // 2), (i + 1) * (tq // 2)), tq, False)

    @pl.when(ki == qi)
    def _():
        band = tq // DIAG_BANDS
        for i in range(DIAG_BANDS):
            update(slice(i * band, (i + 1) * band), (i + 1) * band, True)
        o = [acc_ref[h] for h in heads]
        o = [x / x[:, HEAD_DIM:HEAD_DIM + 1] for x in o]
        assert PAIR == 2
        first = _iota((tq, LANES), 1) < HEAD_DIM
        pairs = [jnp.where(first, o[h], pltpu.roll(o[h + 1], HEAD_DIM, 1)) for h in heads[::PAIR]]
        o_ref[...] = (jnp.concatenate(pairs, axis=1) * og_ref[...].astype(F32)).astype(BF)


def _fox_attn(qa, ka, va, og):
    B, H, T, _ = qa.shape
    D = og.shape[-1]
    tq = _tiles(T)["tq"]
    assert T % tq == 0 and tq % (DIAG_BANDS * LANES) == 0
    nq = T // tq
    qi_tab = np.asarray([q for q in range(nq) for _ in range(q + 1)], np.int32)
    ki_tab = np.asarray([k for q in range(nq) for k in range(q + 1)], np.int32)
    hs = ATTN_HEADS
    assert H % hs == 0 and hs % PAIR == 0
    qspec = pl.BlockSpec((None, hs, tq, LANES), lambda b, p, s, qt, kt: (b, p, qt[s], 0))
    kspec = pl.BlockSpec((None, hs, tq, LANES), lambda b, p, s, qt, kt: (b, p, kt[s], 0))
    ospec = pl.BlockSpec((None, tq, hs * HEAD_DIM), lambda b, p, s, qt, kt: (b, qt[s], p))
    return pl.pallas_call(
        functools.partial(_fox_attn_kernel, tq=tq),
        grid_spec=pltpu.PrefetchScalarGridSpec(
            num_scalar_prefetch=2,
            grid=(B, H // hs, len(qi_tab)),
            in_specs=[qspec, kspec, kspec, ospec],
            out_specs=ospec,
            scratch_shapes=[pltpu.VMEM((hs, tq, LANES), F32), pltpu.VMEM((hs, tq, LANES), F32)],
        ),
        out_shape=jax.ShapeDtypeStruct((B, T, D), BF),
        compiler_params=_params(("arbitrary",) * 3),
        name="fox_attn",
    )(jnp.asarray(qi_tab), jnp.asarray(ki_tab), qa, ka, va, og)


def _fox_decode_kernel(pt_ref, q_ref, kn_ref, vn_ref, lfn_ref, *rest, n_new, group, nb):
    n_in = nb * group
    kp_refs, vp_refs, lfp_refs = rest[:n_in], rest[n_in:2 * n_in], rest[2 * n_in:3 * n_in]
    og_ref, o_ref, qbd_ref, m_ref, l_ref, acc_ref, carry_ref = rest[3 * n_in:]
    j = pl.program_id(1)
    D = q_ref.shape[-1]
    n_heads = D // HEAD_DIM
    rows = n_heads * n_new
    page = lfp_refs[0].shape[-1]
    assert rows == LANES and page == LANES
    seqs = range(nb)
    rr = _iota((rows, LANES), 0)
    ll = _iota((rows, LANES), 1)
    expand = (_iota((rows, n_heads), 0) // n_new == _iota((rows, n_heads), 1)).astype(BF)

    def online_update(s, pv):
        m_prev = [m_ref[i] for i in seqs]
        m_new = [jnp.maximum(m_prev[i], jnp.max(s[i], axis=-1, keepdims=True)) for i in seqs]
        p = [jnp.exp(s[i] - m_new[i]) for i in seqs]
        for i in seqs:
            alpha = jnp.exp(m_prev[i] - m_new[i])
            l_ref[i] = alpha * l_ref[i] + jnp.sum(p[i], axis=-1, keepdims=True)
            acc_ref[i] = alpha * acc_ref[i] + pv(i, p[i].astype(BF))
            m_ref[i] = m_new[i]

    @pl.when(j == 0)
    def _():
        m_ref[...] = jnp.full_like(m_ref, NEG_BIG)
        l_ref[...] = jnp.zeros_like(l_ref)
        acc_ref[...] = jnp.zeros_like(acc_ref)
        own = _iota((rows, D), 1) // HEAD_DIM == _iota((rows, D), 0) // n_new
        incl = (_iota((page, page), 0) <= _iota((page, page), 1)).astype(BF)
        t_of_row = rr % n_new
        pad = page - n_new
        s, vn = [], []
        for i in seqs:
            q = q_ref[i].astype(F32)
            qbd_ref[i] = jnp.where(own, jnp.concatenate([q] * n_heads, axis=0), 0.0).astype(BF)
            kn = jnp.concatenate([kn_ref[i], jnp.zeros((pad, D), F32)], axis=0).astype(BF)
            vn.append(jnp.concatenate([vn_ref[i], jnp.zeros((pad, D), F32)], axis=0).astype(BF))
            lfn = jnp.concatenate([lfn_ref[i], jnp.zeros((pad, n_heads), F32)], axis=0)
            pieces = [_dot_nt(expand, piece).astype(BF) for piece in _split3(lfn)]
            cn = sum(_dot(piece, incl) for piece in pieces)
            cn_q = jnp.sum(jnp.where(ll == t_of_row, cn, 0.0), axis=-1, keepdims=True)
            carry_ref[i] = cn_q
            s.append(jnp.where(ll <= t_of_row, _dot_nt(qbd_ref[i], kn) + (cn_q - cn), NEG_BIG))
        online_update(s, lambda i, p: _dot(p, vn[i]))

    @pl.when(j > 0)
    def _():
        n_keys = group * page
        later = (_iota((n_keys, n_keys), 0) > _iota((n_keys, n_keys), 1)).astype(BF)
        of = lambda refs, i: refs[i * group:(i + 1) * group]
        kt = [jnp.concatenate([r[...].reshape(D, page).astype(BF) for r in of(kp_refs, i)], axis=1) for i in seqs]
        vt = [jnp.concatenate([r[...].reshape(D, page).astype(BF) for r in of(vp_refs, i)], axis=1) for i in seqs]
        lft = [jnp.concatenate([r[...] for r in of(lfp_refs, i)], axis=1) for i in seqs]
        pieces = [[_dot(expand, piece).astype(BF) for piece in _split3(x)] for x in lft]
        suffix = [sum(_dot(piece, later) for piece in ps) for ps in pieces]
        total = [sum(jnp.sum(piece.astype(F32), axis=-1, keepdims=True) for piece in ps) for ps in pieces]
        s = [_dot(qbd_ref[i], kt[i]) + (suffix[i] + carry_ref[i]) for i in seqs]
        online_update(s, lambda i, p: _dot_nt(p, vt[i]))
        for i in seqs:
            carry_ref[i] = carry_ref[i] + total[i]

    @pl.when(j == pl.num_programs(1) - 1)
    def _():
        for i in seqs:
            o = acc_ref[i] / l_ref[i]
            tiles = []
            for p in range(D // LANES):
                lo = o[(PAIR * p) * n_new:(PAIR * p + 1) * n_new, p * LANES:(p + 1) * LANES]
                hi = o[(PAIR * p + 1) * n_new:(PAIR * p + 2) * n_new, p * LANES:(p + 1) * LANES]
                tiles.append(jnp.where(_iota((n_new, LANES), 1) < HEAD_DIM, lo, hi))
            o_ref[i] = (jnp.concatenate(tiles, axis=1) * og_ref[i].astype(F32)).astype(BF)


def _fox_decode(q, k_new, v_new, lf_new, og, cache_k, cache_v, cache_lf, page_table):
    DB, n_new, D = q.shape
    H = D // HEAD_DIM
    n_pool, page = cache_k.shape[:2]
    n_pages = page_table.shape[1]
    ck = cache_k.transpose(0, 2, 3, 1)
    cv = cache_v.transpose(0, 2, 3, 1)
    clf = cache_lf.transpose(0, 2, 1)
    group, nb = DECODE_PAGES, DECODE_SEQS
    assert n_pages % group == 0 and DB % nb == 0
    new = lambda n: pl.BlockSpec((nb, n_new, n), lambda b, j, pt: (b, 0, 0))

    def paged(*shape):
        zeros = (0,) * len(shape)
        return [pl.BlockSpec((None,) + shape,
                             lambda b, j, pt, i=i, g=g: (pt[nb * b + i, n_pages - group * jnp.maximum(j, 1) + g],) + zeros)
                for i in range(nb) for g in range(group)]

    rows = H * n_new
    kv_page = (H, HEAD_DIM, page)
    n_in = nb * group
    return pl.pallas_call(
        functools.partial(_fox_decode_kernel, n_new=n_new, group=group, nb=nb),
        grid_spec=pltpu.PrefetchScalarGridSpec(
            num_scalar_prefetch=1,
            grid=(DB // nb, n_pages // group + 1),
            in_specs=[new(D), new(D), new(D), new(H), *paged(*kv_page), *paged(*kv_page), *paged(H, page), new(D)],
            out_specs=new(D),
            scratch_shapes=[pltpu.VMEM((nb, rows, D), BF), pltpu.VMEM((nb, rows, 1), F32),
                            pltpu.VMEM((nb, rows, 1), F32), pltpu.VMEM((nb, rows, D), F32),
                            pltpu.VMEM((nb, rows, 1), F32)],
        ),
        out_shape=jax.ShapeDtypeStruct((DB, n_new, D), BF),
        compiler_params=_params(("arbitrary", "arbitrary")),
        name="fox_decode",
    )(page_table, q, k_new, v_new, lf_new, *[ck] * n_in, *[cv] * n_in, *[clf] * n_in, og)


def _resid_mm_kernel(x_ref, a_ref, w_ref, y_ref):
    y_ref[...] = x_ref[...] + _dot(a_ref[...], w_ref[...])


def _resid_mm(x, a, w):
    B, T, D = x.shape
    K = a.shape[-1]
    tm = _tiles(T)["tm"]
    return pl.pallas_call(
        _resid_mm_kernel,
        grid=(B, T // tm),
        in_specs=[pl.BlockSpec((None, tm, D), lambda b, t: (b, t, 0)),
                  pl.BlockSpec((None, tm, K), lambda b, t: (b, t, 0)),
                  _const_spec((K, D))],
        out_specs=pl.BlockSpec((None, tm, D), lambda b, t: (b, t, 0)),
        out_shape=jax.ShapeDtypeStruct((B, T, D), F32),
        compiler_params=_params(("arbitrary", "arbitrary")),
        name="resid_mm",
    )(x, a, w)


def _rwkv_proj_kernel(x_ref, sh_ref, g_ref, mu_ref, wr_ref, wk_ref, wv_ref, w0_ref, w1_ref, w2_ref,
                      a0_ref, a1_ref, a2_ref, g1_ref, g2_ref, kk_ref, ka_ref, rk_ref, bd_ref,
                      r_o, lw_o, k_o, v_o, kk_o, a_o, g_o, bonus_o, shift_o, *scratch, seg):
    x = x_ref[...]
    tm, D = x.shape
    xn = _rms(x, g_ref[...])
    row = _iota((tm, D), 0)
    if seg is None:
        carry_ref, = scratch

        @pl.when(pl.program_id(1) == 0)
        def _():
            carry_ref[...] = sh_ref[...]

        prev = jnp.where(row == 0, carry_ref[...], pltpu.roll(xn, 1, 0))
        carry_ref[...] = xn[tm - 1:tm, :]
        shift_o[...] = xn[tm - 1:tm, :]
    else:
        prev = jnp.where(row % seg == 0, sh_ref[...], pltpu.roll(xn, 1, 0))
        shift_o[...] = xn.reshape(tm // seg, seg, D)[:, seg - 1, :]
    xx = prev - xn
    xr, xw, xk, xv, xa, xg = ((xn + xx * mu_ref[i:i + 1, :]).astype(BF) for i in range(6))
    r = _dot(xr, wr_ref[...])
    k = _dot(xk, wk_ref[...])
    v = _dot(xv, wv_ref[...])
    lora_w = _dot(jnp.tanh(_dot(xw, w1_ref[...])).astype(BF), w2_ref[...])
    w_log = -jax.nn.softplus(-(w0_ref[...] + lora_w)) - 0.5
    lw_o[...] = -jnp.exp(w_log)
    a = jax.nn.sigmoid(a0_ref[...] + _dot(_dot(xa, a1_ref[...]).astype(BF), a2_ref[...]))
    g_o[...] = _dot(jax.nn.sigmoid(_dot(xg, g1_ref[...])).astype(BF), g2_ref[...]).astype(BF)
    bd = bd_ref[...]
    kk = k * kk_ref[...]
    kk = kk / jnp.maximum(jnp.sqrt(_dot((kk * kk).astype(BF), bd)), 1e-12)
    k = k * (1.0 + (a - 1.0) * ka_ref[...])
    r_o[...] = r
    k_o[...] = k
    v_o[...] = v.astype(BF)
    kk_o[...] = kk
    a_o[...] = a
    bonus_o[...] = _dot((r * k * rk_ref[...]).astype(BF), bd) * v


def _rwkv_proj(x, shift, g, w, bd, *, seg=None):
    B, T, D = x.shape
    tm = _tiles(T)["tm"]
    assert T % tm == 0
    tok = pl.BlockSpec((None, tm, D), lambda b, t: (b, t, 0))
    if seg is None:
        sh_spec = pl.BlockSpec((None, 1, D), lambda b, t: (b, 0, 0))
        shift_shape = jax.ShapeDtypeStruct((B, 1, D), F32)
        shift_spec = pl.BlockSpec((None, 1, D), lambda b, t: (b, 0, 0))
        scratch = [pltpu.VMEM((1, D), F32)]
    else:
        assert B == 1 and tm == T
        sh_spec = pl.BlockSpec((None, T, D), lambda b, t: (0, 0, 0))
        shift_shape = jax.ShapeDtypeStruct((T // seg, D), F32)
        shift_spec = pl.BlockSpec((T // seg, D), lambda b, t: (0, 0))
        scratch = []
    mats = [w[n] for n in ("w_r", "w_k", "w_v")]
    consts = [w["mu"], *mats, w["w0"], w["w1"], w["w2"], w["a0"], w["a1"], w["a2"], w["g1"], w["g2"],
              w["k_k"], w["k_a"], w["r_k"], bd]
    return pl.pallas_call(
        functools.partial(_rwkv_proj_kernel, seg=seg),
        grid=(B, T // tm),
        in_specs=[tok, sh_spec, _const_spec((None, 1, D), (g[1], 0, 0))] + [_const_spec(c.shape) for c in consts],
        out_specs=[tok] * 8 + [shift_spec],
        out_shape=[jax.ShapeDtypeStruct((B, T, D), dt) for dt in (F32, F32, F32, BF, F32, F32, BF, F32)]
        + [shift_shape],
        scratch_shapes=scratch,
        compiler_params=_params(("arbitrary", "arbitrary")),
        name="rwkv_proj",
    )(x, shift, g[0], *consts)


def _rwkv_scan_kernel(r_ref, lw_ref, k_ref, v_ref, kk_ref, a_ref, s0_ref, y_ref, sT_ref, S_ref, *, C):
    nb, n_rows, d_model = r_ref.shape
    n_pairs = d_model // LANES
    D = nb * d_model

    def cat(ref):
        x = jnp.concatenate([ref[i] for i in range(nb)], axis=1)
        if n_rows < C:
            x = jnp.concatenate([x.astype(F32), jnp.zeros((C - n_rows, D), F32)], axis=0).astype(x.dtype)
        return x

    @pl.when(pl.program_id(1) == 0)
    def _():
        zero = jnp.zeros((HEAD_DIM, HEAD_DIM), F32)
        for i in range(nb):
            for pr in range(n_pairs):
                top = jnp.concatenate([s0_ref[i, PAIR * pr], zero], axis=1)
                bottom = jnp.concatenate([zero, s0_ref[i, PAIR * pr + 1]], axis=1)
                S_ref[i * n_pairs + pr] = jnp.concatenate([top, bottom], axis=0)

    lw = cat(lw_ref)
    tri = (_iota((C, C), 1) <= _iota((C, C), 0)).astype(BF)
    cum = _dot_exact_lhs01(tri, lw)
    cend = cum[C - 1:C, :]
    kk = cat(kk_ref)
    kka = kk * cat(a_ref)
    k = cat(k_ref)
    inv_p = jnp.exp(-cum)
    to_end = jnp.exp(cend - cum)
    at = (-(kk * jnp.exp(cum - lw))).astype(BF)
    bt = (kka * inv_p).astype(BF)
    kt = (k * inv_p).astype(BF)
    rt = (cat(r_ref) * jnp.exp(cum)).astype(BF)
    kh = (k * to_end).astype(BF)
    bh = (kka * to_end).astype(BF)
    vb = cat(v_ref)
    p_end = jnp.exp(cend)

    first_half = _iota((C, LANES), 1) < HEAD_DIM
    n2 = PAIR * C
    ri = _iota((n2, n2), 0)
    ci = _iota((n2, n2), 1)
    same = ri // C == ci // C
    strict = same & (ci < ri)
    incl = same & (ci <= ri)
    eye = (ri == ci).astype(F32)
    same_head = _iota((LANES, LANES), 0) // HEAD_DIM == _iota((LANES, LANES), 1) // HEAD_DIM

    def stack(t):
        z = jnp.zeros_like(t)
        return jnp.concatenate([jnp.where(first_half, t, z), jnp.where(first_half, z, t)], axis=0)

    pairs = range(D // LANES)
    lanes = [slice(pr * LANES, (pr + 1) * LANES) for pr in pairs]
    at_s, rt_s, kt_s, bt_s, v_s = ([stack(t[:, ln]) for ln in lanes] for t in (at, rt, kt, bt, vb))
    aa = [_dot_nt(jnp.concatenate([at_s[pr], rt_s[pr]], axis=0), jnp.concatenate([kt_s[pr], bt_s[pr]], axis=0))
          for pr in pairs]
    a_ak = [jnp.where(strict, x[:n2, :n2], 0.0).astype(BF) for x in aa]
    a_ab = [jnp.where(strict, x[:n2, n2:], 0.0) for x in aa]
    a_rk = [jnp.where(incl, x[n2:, :n2], 0.0).astype(BF) for x in aa]
    a_rb = [jnp.where(incl, x[n2:, n2:], 0.0).astype(BF) for x in aa]
    inv = [eye + x for x in a_ab]
    npow = a_ab
    for _ in range(int(math.log2(C)) - 1):
        nbf = [x.astype(BF) for x in npow]
        npow = [_dot(x, x) for x in nbf]
        inv = [i + _dot(i.astype(BF), n.astype(BF)) for i, n in zip(inv, npow)]
    S = [S_ref[pr] for pr in pairs]
    Sb = [x.astype(BF) for x in S]
    rhs = [(_dot_nt(at_s[pr], Sb[pr]) + _dot(a_ak[pr], v_s[pr])).astype(BF) for pr in pairs]
    u = [_dot(inv[pr].astype(BF), rhs[pr]) for pr in pairs]
    y = [_dot_nt(rt_s[pr], Sb[pr]) + _dot(a_rk[pr], v_s[pr]) + _dot(a_rb[pr], u[pr].astype(BF)) for pr in pairs]
    for pr in pairs:
        y_ref[pr // n_pairs, :, lanes[pr % n_pairs]] = (y[pr][:C] + y[pr][C:])[:n_rows]
        u_pair = (u[pr][:C] + u[pr][C:]).astype(BF)
        dS = _dot_tn(vb[:, lanes[pr]], kh[:, lanes[pr]]) + _dot_tn(u_pair, bh[:, lanes[pr]])
        S_ref[pr] = S[pr] * p_end[:, lanes[pr]] + jnp.where(same_head, dS, 0.0)

    @pl.when(pl.program_id(1) == pl.num_programs(1) - 1)
    def _():
        for pr in pairs:
            i, h = pr // n_pairs, PAIR * (pr % n_pairs)
            sT_ref[i, h] = S_ref[pr][:HEAD_DIM, :HEAD_DIM]
            sT_ref[i, h + 1] = S_ref[pr][HEAD_DIM:, HEAD_DIM:]


def _rwkv_scan(r, lw, k, v, kk, a, s0):
    B, T, D = r.shape
    C, nb = RWKV_CHUNK, RWKV_SEQS
    rows = min(T, C)
    assert T % rows == 0 and B % nb == 0
    n_pairs = D // LANES
    tok = pl.BlockSpec((nb, rows, D), lambda b, t: (b, t, 0))
    st = pl.BlockSpec((nb, D // HEAD_DIM, HEAD_DIM, HEAD_DIM), lambda b, t: (b, 0, 0, 0))
    return pl.pallas_call(
        functools.partial(_rwkv_scan_kernel, C=C),
        grid=(B // nb, T // rows),
        in_specs=[tok] * 6 + [st],
        out_specs=[tok, st],
        out_shape=[jax.ShapeDtypeStruct((B, T, D), F32), jax.ShapeDtypeStruct(s0.shape, F32)],
        scratch_shapes=[pltpu.VMEM((nb * n_pairs, LANES, LANES), F32)],
        compiler_params=_params(("arbitrary", "arbitrary")),
        name="rwkv_scan",
    )(r, lw, k, v, kk, a, s0)


def _rwkv_post_kernel(x_ref, y_ref, bonus_ref, g_ref, lng_ref, lnb_ref, bd_ref, wo_ref, o_ref):
    y = y_ref[...]
    bd = bd_ref[...]
    d = y - _dot(y.astype(BF), bd) * (1.0 / HEAD_DIM)
    var = _dot((d * d).astype(BF), bd) * (1.0 / HEAD_DIM)
    yn = d * lax.rsqrt(var + LNX_EPS) * lng_ref[...] + lnb_ref[...] + bonus_ref[...]
    o_ref[...] = x_ref[...] + _dot((yn * g_ref[...].astype(F32)).astype(BF), wo_ref[...])


def _rwkv_post(x, y, bonus, g, ln_g, ln_b, bd, w_o):
    B, T, D = x.shape
    tm = _tiles(T)["tm"]
    tok = pl.BlockSpec((None, tm, D), lambda b, t: (b, t, 0))
    return pl.pallas_call(
        _rwkv_post_kernel,
        grid=(B, T // tm),
        in_specs=[tok, tok, tok, tok, _const_spec((1, D)), _const_spec((1, D)), _const_spec((D, D)),
                  _const_spec((D, D))],
        out_specs=tok,
        out_shape=jax.ShapeDtypeStruct((B, T, D), F32),
        compiler_params=_params(("arbitrary", "arbitrary")),
        name="rwkv_post",
    )(x, y, bonus, g, ln_g, ln_b, bd, w_o)


def _pad_cols(w, n):
    return jnp.pad(w, ((0, 0), (0, n - w.shape[1])))


def _pad_rows(w, n):
    return jnp.pad(w, ((0, n - w.shape[0]), (0, 0)))


def kernel(x_prompt, x_sample, cache_k, cache_v, cache_logf, page_table, state_wkv, state_shift, state_conv,
           mix_norm, a_w_in, a_v_norm, a_w_s, a_b_s, a_w_out,
           f_w_qkv, f_q_norm, f_k_norm, f_w_fgate, f_b_fgate, f_w_ogate, f_w_out,
           r_mu, r_w_r, r_w_k, r_w_v, r_w0, r_w1, r_w2, r_a0, r_a1, r_a2, r_g1, r_g2,
           r_k_k, r_k_a, r_r_k, r_lnx_g, r_lnx_b, r_w_o,
           ffn_norm, ffn_w_up, ffn_conv_w, ffn_conv_b, ffn_w_down):
    B, T, D = x_prompt.shape
    DB, n_new, _ = x_sample.shape
    H = D // HEAD_DIM
    depth = ffn_w_up.shape[0]
    F2 = ffn_w_up.shape[-1]
    M = DB * n_new
    row = lambda v: v.reshape(1, -1).astype(F32)

    mix_g = mix_norm.reshape(depth, 1, D)
    ffn_g = ffn_norm.reshape(depth, 1, D)
    ffn_up, ffn_down = ffn_w_up.astype(BF), ffn_w_down.astype(BF)
    ffn_cb = ffn_conv_b.reshape(depth, 1, F2)
    a_in, a_out = a_w_in.astype(BF), a_w_out.astype(BF)
    a_vg = a_v_norm.reshape(a_v_norm.shape[0], 1, -1)
    head_ones = (jnp.arange(D)[:, None] // HEAD_DIM == jnp.arange(D)[None, :] // HEAD_DIM).astype(BF)
    f_all = jnp.concatenate([f_w_qkv, f_w_ogate, _pad_cols(f_w_fgate, LANES)], axis=1).astype(BF)
    f_qg, f_kg = row(jnp.tile(f_q_norm, H)), row(jnp.tile(f_k_norm, H))
    f_bf = _pad_cols(row(f_b_fgate), LANES)
    f_out = f_w_out.astype(BF)
    lora = lambda w1, w2, n: (_pad_cols(w1, n).astype(BF), _pad_rows(w2, n).astype(BF))
    rw = dict(mu=r_mu, w_r=r_w_r.astype(BF), w_k=r_w_k.astype(BF), w_v=r_w_v.astype(BF),
              w0=row(r_w0), a0=row(r_a0), k_k=row(r_k_k), k_a=row(r_k_a), r_k=row(r_r_k))
    rw["w1"], rw["w2"] = lora(r_w1, r_w2, LANES)
    rw["a1"], rw["a2"] = lora(r_a1, r_a2, LANES)
    rw["g1"], rw["g2"] = lora(r_g1, r_g2, 2 * LANES)
    r_out = r_w_o.astype(BF)

    xp = x_prompt
    xs = x_sample.reshape(1, M, D)
    conv_p, conv_s, chunk_v_s = [], [], []
    outs = {}
    for i in range(depth):
        kind = i % 3
        g = (mix_g, i)
        if kind == 0:
            j = i // 3
            a_w = (a_in, a_vg, a_w_s[j], a_b_s[j][:, :, None], a_out)
            xp, = _chunk_mlp(xp, j, g, *a_w, seq=CHUNK, emit_v=False)
            xs, v_rows = _chunk_mlp(xs, j, g, *a_w, seq=n_new, emit_v=True)
            chunk_v_s.append(v_rows.reshape(DB, n_new, -1))
        elif kind == 1:
            f_w = (f_all, f_qg, f_kg, f_bf, head_ones)
            qa, ka, va, kf, vf, og, lf = _fox_proj(xp, g, *f_w, decode=False)
            outs["k_p"], outs["v_p"] = kf.transpose(0, 3, 1, 2), vf.transpose(0, 3, 1, 2)
            outs["logf_p"] = lf.transpose(0, 2, 1)
            xp = _resid_mm(xp, _fox_attn(qa, ka, va, og), f_out)
            q, kf, vf, og, lf = _fox_proj(xs, g, *f_w, decode=True)
            shp = (DB, n_new, D)
            outs["k_s"], outs["v_s"] = kf.reshape(DB, n_new, H, HEAD_DIM), vf.reshape(DB, n_new, H, HEAD_DIM)
            outs["logf_s"] = lf.reshape(DB, n_new, H)
            att = _fox_decode(q.reshape(shp), kf.reshape(shp), vf.reshape(shp), outs["logf_s"], og.reshape(shp),
                              cache_k, cache_v, cache_logf, page_table)
            xs = _resid_mm(xs, att.reshape(1, M, D), f_out)
        else:
            r, lw, k, v, kk, a, gate, bonus, shift = _rwkv_proj(xp, jnp.zeros((B, 1, D), F32), g, rw, head_ones)
            y, outs["wkv_p"] = _rwkv_scan(r, lw, k, v, kk, a, jnp.zeros((B, H, HEAD_DIM, HEAD_DIM), F32))
            outs["shift_p"] = shift.reshape(B, D)
            xp = _rwkv_post(xp, y, bonus, gate, row(r_lnx_g), row(r_lnx_b), head_ones, r_out)
            sh = jnp.pad(state_shift[:, None, :], ((0, 0), (0, n_new - 1), (0, 0))).reshape(1, M, D)
            r, lw, k, v, kk, a, gate, bonus, shift = _rwkv_proj(xs, sh, g, rw, head_ones, seg=n_new)
            seqs = [t.reshape(DB, n_new, D) for t in (r, lw, k, v, kk, a)]
            y, outs["wkv_s"] = _rwkv_scan(*seqs, state_wkv)
            outs["shift_s"] = shift
            xs = _rwkv_post(xs, y.reshape(1, M, D), bonus, gate, row(r_lnx_g), row(r_lnx_b), head_ones,
                            r_out)
        ffn_w = (ffn_g, ffn_up, ffn_conv_w, ffn_cb, ffn_down)
        xp, cp = _conv_ffn(xp, jnp.zeros((B, 2, F2), F32), i, *ffn_w)
        st = state_conv[i]
        e2 = jnp.pad(st, ((0, 0), (0, n_new - 2), (0, 0))).reshape(M, F2)
        e1 = jnp.pad(st[:, 1:2], ((0, 0), (0, n_new - 1), (0, 0))).reshape(M, F2)
        xs, cs = _conv_ffn(xs, jnp.stack([e2, e1]), i, *ffn_w, seg=n_new)
        conv_p.append(cp)
        conv_s.append(cs)
    return (xp, xs.reshape(DB, n_new, D), outs["k_p"], outs["v_p"], outs["logf_p"], outs["wkv_p"], outs["shift_p"],
            jnp.stack(conv_p), outs["k_s"], outs["v_s"], outs["logf_s"], outs["wkv_s"], outs["shift_s"],
            jnp.stack(conv_s), jnp.stack(chunk_v_s))
```
